```python
import math
import jax
import jax.numpy as jnp
from jax import lax
import numpy as np

D_MODEL = 2048
BATCH = 4
SEQ = 2048
DEPTH = 4
DEC_BATCH = 8
DEC_SEQ = 8
PAST_LEN = 16384
PAGE_SIZE = 128

N_MIXERS = 3
N_RET_LAYERS = (DEPTH + 2) // 3
N_MOBA_LAYERS = (DEPTH + 1) // 3
N_HGRN_LAYERS = DEPTH // 3

RET_HEADS = 8
RET_DK = D_MODEL // RET_HEADS
RET_DV = 2 * D_MODEL // RET_HEADS
RET_QK_W = RET_HEADS * RET_DK
RET_V_W = RET_HEADS * RET_DV
RET_IN_W = 2 * RET_QK_W + 2 * RET_V_W
RET_CHUNK = 128

MOBA_HEADS = 16
MOBA_HD = D_MODEL // MOBA_HEADS
MOBA_W = MOBA_HEADS * MOBA_HD
MOBA_BLOCK = 256
MOBA_TOPK = 3
MOBA_QBLOCK = 32

HG_HEADS = 16
HG_DK = 128
HG_DV = D_MODEL // HG_HEADS
HG_KW = HG_HEADS * HG_DK
HG_VW = HG_HEADS * HG_DV
HG_CHUNK = 64

EPS = 1e-6

kernel_name = 'hybrid_retention_moba_hgrn2_step'


def rms_norm(x, gain):
    x32 = x.astype(jnp.float32)
    y = x32 * lax.rsqrt(jnp.mean(x32 * x32, axis=-1, keepdims=True) + EPS)
    return (y * gain.astype(jnp.float32)).astype(x.dtype)


def head_rms(a):
    return a * lax.rsqrt(jnp.mean(a * a, axis=-1, keepdims=True) + EPS)


def to_heads(a, n_heads, d):
    b, t, _ = a.shape
    return a.reshape(b, t, n_heads, d).transpose(0, 2, 1, 3).astype(jnp.float32)


def from_heads(a):
    b, h, t, d = a.shape
    return a.transpose(0, 2, 1, 3).reshape(b, t, h * d)


def to_chunks(a, c):
    b, h, t, d = a.shape
    return a.reshape(b, h, t // c, c, d).transpose(2, 0, 1, 3, 4)


def from_chunks(a):
    n, b, h, c, d = a.shape
    return a.transpose(1, 2, 0, 3, 4).reshape(b, h, n * c, d)


def alibi_slopes(n_heads):
    return jnp.exp2(-8.0 * jnp.arange(1, n_heads + 1, dtype=jnp.float32) / n_heads)


def retention_chunkwise(q, k, v, s0):
    t = q.shape[2]
    c = math.gcd(t, RET_CHUNK)
    log_g = jnp.log1p(-jnp.exp2(-5.0 - jnp.arange(RET_HEADS, dtype=jnp.float32)))
    i = jnp.arange(c, dtype=jnp.float32)
    rel = i[:, None] - i[None, :]
    intra = jnp.exp(jnp.where(rel >= 0, log_g[:, None, None] * rel, -jnp.inf))
    q_dec = jnp.exp(log_g[:, None] * (i + 1.0))[:, :, None]
    k_dec = jnp.exp(log_g[:, None] * (c - 1.0 - i))[:, :, None]
    c_dec = jnp.exp(log_g * c)[:, None, None]

    def step(s, blk):
        qc, kc, vc = blk
        att = jnp.einsum('bhtd,bhsd->bhts', qc, kc) * intra
        o = jnp.einsum('bhts,bhsv->bhtv', att, vc) + jnp.einsum('bhtd,bhdv->bhtv', qc * q_dec, s)
        s = s * c_dec + jnp.einsum('bhsd,bhsv->bhdv', kc * k_dec, vc)
        return s, o

    s, o = lax.scan(step, s0, (to_chunks(q, c), to_chunks(k, c), to_chunks(v, c)))
    return from_chunks(o), s


def gla_chunkwise(q, k, v, log_f, s0):
    t = q.shape[2]
    c = math.gcd(t, HG_CHUNK)
    causal = jnp.tril(jnp.ones((c, c), dtype=bool))[:, :, None]

    def step(s, blk):
        qc, kc, vc, gc = blk
        b = jnp.cumsum(gc, axis=2)
        o = jnp.einsum('bhtd,bhdv->bhtv', qc * jnp.exp(b), s)
        decay = jnp.exp(jnp.where(causal, b[:, :, :, None, :] - b[:, :, None, :, :], -jnp.inf))
        att = jnp.einsum('bhtsd,bhsd->bhts', qc[:, :, :, None, :] * decay, kc)
        o = o + jnp.einsum('bhts,bhsv->bhtv', att, vc)
        b_end = b[:, :, -1:, :]
        s = s * jnp.exp(b[:, :, -1, :])[..., None] + jnp.einsum('bhsd,bhsv->bhdv', kc * jnp.exp(b_end - b), vc)
        return s, o

    s, o = lax.scan(step, s0, (to_chunks(q, c), to_chunks(k, c), to_chunks(v, c), to_chunks(log_f, c)))
    return from_chunks(o), s


def retention_mixer(x, w_in, w_out, s0):
    q, k, v, g = jnp.split(x @ w_in, [RET_QK_W, 2 * RET_QK_W, 2 * RET_QK_W + RET_V_W], axis=-1)
    q = to_heads(q, RET_HEADS, RET_DK)
    k = to_heads(k, RET_HEADS, RET_DK) * (RET_DK ** -0.5)
    v = to_heads(v, RET_HEADS, RET_DV)
    o, s = retention_chunkwise(q, k, v, s0.astype(jnp.float32))
    o = from_heads(head_rms(o)).astype(x.dtype)
    return (o * jax.nn.silu(g)) @ w_out, s.astype(x.dtype)


def hgrn_lower_bounds(logits):
    p = jax.nn.softmax(logits.astype(jnp.float32), axis=0)
    return jnp.cumsum(p, axis=0) - p[0]


def hgrn2_mixer(x, w_in, lb, o_gain, w_out, s0):
    q, f, i, g = jnp.split(x @ w_in, [HG_KW, 2 * HG_KW, 2 * HG_KW + HG_VW], axis=-1)
    q = jax.nn.silu(to_heads(q, HG_HEADS, HG_DK))
    lb = lb.reshape(HG_HEADS, 1, HG_DK)
    log_f = jnp.logaddexp(jnp.log(lb), jnp.log1p(-lb) + jax.nn.log_sigmoid(to_heads(f, HG_HEADS, HG_DK)))
    k = -jnp.expm1(log_f)
    o, s = gla_chunkwise(q, k, to_heads(i, HG_HEADS, HG_DV), log_f, s0.astype(jnp.float32))
    o = from_heads(head_rms(o) * o_gain.astype(jnp.float32)).astype(x.dtype)
    return (o * jax.nn.silu(g)) @ w_out, s.astype(x.dtype)


def moba_attend(q, kb, vb, q_start):
    b, t, h, hd = q.shape
    n_kb = kb.shape[1]
    n_sel = min(MOBA_TOPK, n_kb - 1)
    qb = math.gcd(t, MOBA_QBLOCK)
    n_qb = t // qb
    scale = hd ** -0.5
    slopes = alibi_slopes(h)[:, None, None]
    k_mean = jnp.mean(kb, axis=2)
    offs = jnp.arange(MOBA_BLOCK)
    head_ix = jnp.arange(h)[:, None, None]

    def one_block(args):
        bi, j = args
        t0 = q_start + j * qb
        c = t0 // MOBA_BLOCK
        t_pos = t0 + jnp.arange(qb)
        qh = lax.dynamic_slice_in_dim(q[bi], j * qb, qb, axis=0)
        k_own = kb[bi, c]
        v_own = vb[bi, c]
        dist = t_pos[:, None] - (c * MOBA_BLOCK + offs)[None, :]
        s_own = jnp.einsum('qhd,khd->hqk', qh, k_own) * scale - slopes * dist
        scores = jnp.where(dist >= 0, s_own, -jnp.inf)
        if n_sel > 0:
            gate = jnp.einsum('qhd,nhd->hqn', qh, k_mean[bi])
            gate = jnp.where(jnp.arange(n_kb) < c, gate, -jnp.inf)
            _, idx = lax.top_k(gate, n_sel)
            k_sel = kb[bi, idx, :, head_ix]
            v_sel = vb[bi, idx, :, head_ix]
            dist_sel = t_pos[None, :, None, None] - (idx[..., None] * MOBA_BLOCK + offs)
            s_sel = jnp.einsum('qhd,hqnkd->hqnk', qh, k_sel) * scale - slopes[..., None] * dist_sel
            s_sel = jnp.where((jnp.arange(n_sel) < c)[:, None], s_sel, -jnp.inf)
            scores = jnp.concatenate([scores, s_sel.reshape(h, qb, n_sel * MOBA_BLOCK)], axis=-1)
        p = jax.nn.softmax(scores, axis=-1)
        out = jnp.einsum('hqk,khd->qhd', p[..., :MOBA_BLOCK], v_own)
        if n_sel > 0:
            p_sel = p[..., MOBA_BLOCK:].reshape(h, qb, n_sel, MOBA_BLOCK)
            out = out + jnp.einsum('hqnk,hqnkd->qhd', p_sel, v_sel)
        return out

    bis = jnp.repeat(jnp.arange(b), n_qb)
    js = jnp.tile(jnp.arange(n_qb), b)
    out = lax.map(one_block, (bis, js))
    return out.reshape(b, t, h, hd)


def moba_mixer(x, w_in, q_gain, k_gain, w_out, k_past, v_past):
    b, t, _ = x.shape
    q, k, v, g = jnp.split(x @ w_in, 4, axis=-1)
    shp = (b, t, MOBA_HEADS, MOBA_HD)
    q = rms_norm(q.reshape(shp), q_gain)
    k = rms_norm(k.reshape(shp), k_gain)
    v = v.reshape(shp)
    q_start = k_past.shape[1]
    total = q_start + t
    n_kb = -(-total // MOBA_BLOCK)
    pad = jnp.zeros((b, n_kb * MOBA_BLOCK - total, MOBA_HEADS, MOBA_HD), jnp.float32)
    blk = (b, n_kb, MOBA_BLOCK, MOBA_HEADS, MOBA_HD)
    kb = jnp.concatenate([k_past.astype(jnp.float32), k.astype(jnp.float32), pad], axis=1).reshape(blk)
    vb = jnp.concatenate([v_past.astype(jnp.float32), v.astype(jnp.float32), pad], axis=1).reshape(blk)
    o = moba_attend(q.astype(jnp.float32), kb, vb, q_start).reshape(b, t, MOBA_W).astype(x.dtype)
    return (o * jax.nn.silu(g)) @ w_out, k, v


def gather_pages(cache, slot, page_table):
    db, n_pages = page_table.shape
    rows = cache[slot, page_table]
    return rows.reshape(db, n_pages * PAGE_SIZE, MOBA_HEADS, MOBA_HD)


def setup_inputs(seed: int = 0) -> dict:
    key = jax.random.key(seed)
    ks = jax.random.split(key, 18)
    f32 = jnp.float32
    n_pages = PAST_LEN // PAGE_SIZE
    n_used = DEC_BATCH * n_pages
    n_pool = n_used + n_used // 4

    def w(k, shape, fan_in):
        return jax.random.normal(k, shape, f32) * (fan_in ** -0.5)

    def gain(k, shape):
        return 1.0 + 0.02 * jax.random.normal(k, shape, f32)

    kv_shape = (N_MOBA_LAYERS, n_pool, PAGE_SIZE, MOBA_HEADS, MOBA_HD)
    return {
        'x_prompt': jax.random.normal(ks[0], (BATCH, SEQ, D_MODEL), f32),
        'x_sample': jax.random.normal(ks[1], (DEC_BATCH, DEC_SEQ, D_MODEL), f32),
        'state_ret': 0.1 * jax.random.normal(ks[2], (N_RET_LAYERS, DEC_BATCH, RET_HEADS, RET_DK, RET_DV), f32),
        'cache_k': jax.random.normal(ks[3], kv_shape, f32),
        'cache_v': jax.random.normal(ks[4], kv_shape, f32),
        'state_hgrn': 0.3 * jax.random.normal(ks[5], (N_HGRN_LAYERS, DEC_BATCH, HG_HEADS, HG_DK, HG_DV), f32),
        'page_table': jax.random.permutation(ks[6], n_pool)[:n_used].reshape(DEC_BATCH, n_pages).astype(jnp.int32),
        'norm_gain': gain(ks[7], (DEPTH, D_MODEL)),
        'ret_w_in': w(ks[8], (N_RET_LAYERS, D_MODEL, RET_IN_W), D_MODEL),
        'ret_w_out': w(ks[9], (N_RET_LAYERS, RET_V_W, D_MODEL), RET_V_W),
        'moba_w_in': w(ks[10], (N_MOBA_LAYERS, D_MODEL, 4 * MOBA_W), D_MODEL),
        'moba_q_gain': gain(ks[11], (N_MOBA_LAYERS, MOBA_HD)),
        'moba_k_gain': gain(ks[12], (N_MOBA_LAYERS, MOBA_HD)),
        'moba_w_out': w(ks[13], (N_MOBA_LAYERS, MOBA_W, D_MODEL), MOBA_W),
        'hgrn_w_in': w(ks[14], (N_HGRN_LAYERS, D_MODEL, 2 * HG_KW + 2 * HG_VW), D_MODEL),
        'hgrn_lb_logits': 0.1 * jax.random.normal(ks[15], (DEPTH, HG_KW), f32),
        'hgrn_o_gain': gain(ks[16], (N_HGRN_LAYERS, HG_DV)),
        'hgrn_w_out': w(ks[17], (N_HGRN_LAYERS, HG_VW, D_MODEL), HG_VW),
    }


def reference(x_prompt, x_sample, state_ret, cache_k, cache_v, state_hgrn, page_table,
              norm_gain, ret_w_in, ret_w_out, moba_w_in, moba_q_gain, moba_k_gain, moba_w_out,
              hgrn_w_in, hgrn_lb_logits, hgrn_o_gain, hgrn_w_out):
    bp = x_prompt.shape[0]
    lower_bounds = hgrn_lower_bounds(hgrn_lb_logits)
    h_p, h_s = x_prompt, x_sample
    kp_l, vp_l, ks_l, vs_l, rp_l, rs_l, gp_l, gs_l = [], [], [], [], [], [], [], []
    for layer in range(DEPTH):
        kind, slot = layer % N_MIXERS, layer // N_MIXERS
        xp = rms_norm(h_p, norm_gain[layer])
        xs = rms_norm(h_s, norm_gain[layer])
        if kind == 0:
            s0 = jnp.zeros((bp, RET_HEADS, RET_DK, RET_DV), jnp.float32)
            yp, st_p = retention_mixer(xp, ret_w_in[slot], ret_w_out[slot], s0)
            ys, st_s = retention_mixer(xs, ret_w_in[slot], ret_w_out[slot], state_ret[slot])
            rp_l.append(st_p)
            rs_l.append(st_s)
        elif kind == 1:
            no_past = jnp.zeros((bp, 0, MOBA_HEADS, MOBA_HD), xp.dtype)
            yp, kp, vp = moba_mixer(xp, moba_w_in[slot], moba_q_gain[slot], moba_k_gain[slot],
                                    moba_w_out[slot], no_past, no_past)
            k_past = gather_pages(cache_k, slot, page_table)
            v_past = gather_pages(cache_v, slot, page_table)
            ys, ks_, vs_ = moba_mixer(xs, moba_w_in[slot], moba_q_gain[slot], moba_k_gain[slot],
                                      moba_w_out[slot], k_past, v_past)
            kp_l.append(kp)
            vp_l.append(vp)
            ks_l.append(ks_)
            vs_l.append(vs_)
        else:
            s0 = jnp.zeros((bp, HG_HEADS, HG_DK, HG_DV), jnp.float32)
            lb = lower_bounds[layer]
            yp, st_p = hgrn2_mixer(xp, hgrn_w_in[slot], lb, hgrn_o_gain[slot], hgrn_w_out[slot], s0)
            ys, st_s = hgrn2_mixer(xs, hgrn_w_in[slot], lb, hgrn_o_gain[slot], hgrn_w_out[slot], state_hgrn[slot])
            gp_l.append(st_p)
            gs_l.append(st_s)
        h_p = h_p + yp
        h_s = h_s + ys
    return (h_p, h_s, jnp.stack(kp_l), jnp.stack(vp_l), jnp.stack(ks_l), jnp.stack(vs_l),
            jnp.stack(rp_l), jnp.stack(rs_l), jnp.stack(gp_l), jnp.stack(gs_l))
```

```python
import functools
import math

import numpy as np
import jax
import jax.numpy as jnp
from jax import lax
from jax.experimental import pallas as pl
from jax.experimental.pallas import tpu as pltpu

F32 = jnp.float32
BF16 = jnp.bfloat16
EPS = 1e-6
NEG_INF = float("-inf")

N_MIXERS = 3
RET_HEADS = 8
RET_CHUNK = 128
MOBA_HEADS = 16
MOBA_HD = 128
MOBA_BLOCK = 256
MOBA_TOPK = 3
HG_HEADS = 16
HG_DK = 128
HG_CHUNK = 128
HG_SUB = 16

VMEM_LIMIT_BYTES = 52 * 1024 * 1024
LANES = 128


def _cparams(sem):
    return pltpu.CompilerParams(dimension_semantics=sem, vmem_limit_bytes=VMEM_LIMIT_BYTES)


def _dot(a, b):
    return jnp.dot(a, b, preferred_element_type=F32)


def _dot_nt(a, b, precision=None):
    return lax.dot_general(a, b, (((1,), (1,)), ((), ())), preferred_element_type=F32, precision=precision)


def _dot_tn(a, b):
    return lax.dot_general(a, b, (((0,), (0,)), ((), ())), preferred_element_type=F32)


def _silu(x):
    return x * jax.nn.sigmoid(x)


def _pad_rows(x, rows):
    if x.shape[0] == rows:
        return x
    return jnp.concatenate([x, jnp.zeros((rows - x.shape[0], x.shape[1]), x.dtype)], axis=0)


def _proj_kernel(x_ref, gain_ref, w_ref, o_ref, xn_ref):
    @pl.when(pl.program_id(1) == 0)
    def _():
        x = x_ref[...]
        ms = jnp.mean(x * x, axis=-1, keepdims=True)
        xn_ref[...] = (x * lax.rsqrt(ms + EPS) * gain_ref[...]).astype(BF16)

    o_ref[...] = _dot(xn_ref[...], w_ref[...]).astype(o_ref.dtype)


def _proj(x, gain, w, col0, ncols, out_dtype):
    m, d = x.shape
    tm = min(m, 1024)
    tn = 512
    assert m % tm == 0 and ncols % tn == 0 and col0 % tn == 0
    jb = col0 // tn
    return pl.pallas_call(
        _proj_kernel,
        out_shape=jax.ShapeDtypeStruct((m, ncols), out_dtype),
        grid=(m // tm, ncols // tn),
        in_specs=[
            pl.BlockSpec((tm, d), lambda i, j: (i, 0)),
            pl.BlockSpec((1, d), lambda i, j: (0, 0)),
            pl.BlockSpec((d, tn), lambda i, j: (0, j + jb)),
        ],
        out_specs=pl.BlockSpec((tm, tn), lambda i, j: (i, j)),
        scratch_shapes=[pltpu.VMEM((tm, d), BF16)],
        compiler_params=_cparams(("parallel", "arbitrary")),
        name="norm_in_proj",
    )(x, gain.reshape(1, d), w)


def _oproj_kernel(a_ref, w_ref, r_ref, o_ref):
    o_ref[...] = _dot(a_ref[...].astype(BF16), w_ref[...]) + r_ref[...]


def _oproj(a, w, resid):
    m, k = a.shape
    n = w.shape[1]
    tm = min(m, 1024)
    tn = 512
    assert m % tm == 0 and n % tn == 0
    return pl.pallas_call(
        _oproj_kernel,
        out_shape=jax.ShapeDtypeStruct((m, n), F32),
        grid=(m // tm, n // tn),
        in_specs=[
            pl.BlockSpec((tm, k), lambda i, j: (i, 0)),
            pl.BlockSpec((k, tn), lambda i, j: (0, j)),
            pl.BlockSpec((tm, tn), lambda i, j: (i, j)),
        ],
        out_specs=pl.BlockSpec((tm, tn), lambda i, j: (i, j)),
        compiler_params=_cparams(("parallel", "parallel")),
        name="out_proj_residual",
    )(a, w, resid)


def _ret_tables(c, dk):
    scale = dk ** -0.5
    log_g = np.log1p(-np.exp2(-5.0 - np.arange(RET_HEADS, dtype=np.float64)))
    i = np.arange(c, dtype=np.float64)
    rel = i[:, None] - i[None, :]
    intra = np.where(rel >= 0, np.exp(log_g[:, None, None] * np.maximum(rel, 0.0)), 0.0) * scale
    q_dec = np.exp(log_g[:, None] * (i + 1.0))
    k_dec = np.exp(log_g[:, None] * (c - 1.0 - i)) * scale
    c_dec = np.exp(log_g * c)
    big = RET_CHUNK
    intra_p = np.zeros((RET_HEADS, big, big), np.float32)
    intra_p[:, :c, :c] = intra
    qd = np.zeros((RET_HEADS, big, 1), np.float32)
    qd[:, :c, 0] = q_dec
    kd = np.zeros((RET_HEADS, big, 1), np.float32)
    kd[:, :c, 0] = k_dec
    cd = c_dec.astype(np.float32).reshape(RET_HEADS, 1, 1)
    return jnp.asarray(intra_p), jnp.asarray(qd), jnp.asarray(kd), jnp.asarray(cd)


def _ret_kernel(*refs, t_rows, has_s0):
    if has_s0:
        q_ref, k_ref, v_ref, g_ref, intra_ref, qd_ref, kd_ref, cd_ref, s0_ref, o_ref, s_ref, st = refs
        st[...] = s0_ref[...]
    else:
        q_ref, k_ref, v_ref, g_ref, intra_ref, qd_ref, kd_ref, cd_ref, o_ref, s_ref, st = refs
        st[...] = jnp.zeros_like(st)
    c = RET_CHUNK
    t_pad = max(t_rows, c)
    intra = intra_ref[...]
    qd = qd_ref[...]
    kd = kd_ref[...]
    cd = cd_ref[...]

    def chunk(ci, carry):
        if t_rows >= c:
            rows = pl.ds(pl.multiple_of(ci * c, c), c)
        else:
            rows = pl.ds(0, t_rows)
        qc = _pad_rows(q_ref[rows, :], c).astype(BF16)
        kf = _pad_rows(k_ref[rows, :], c).astype(F32)
        vc = _pad_rows(v_ref[rows, :], c).astype(BF16)
        s = st[...]
        att = _dot_nt(qc, kf.astype(BF16)) * intra
        o = _dot(att.astype(BF16), vc) + qd * _dot(qc, s.astype(BF16))
        st[...] = s * cd + _dot_tn((kf * kd).astype(BF16), vc)
        o = o * lax.rsqrt(jnp.mean(o * o, axis=-1, keepdims=True) + EPS)
        gate = g_ref[rows, :].astype(F32)
        o_ref[rows, :] = (o[: gate.shape[0]] * _silu(gate)).astype(o_ref.dtype)
        return carry

    lax.fori_loop(0, t_pad // c, chunk, 0)
    s_ref[...] = st[...]


def _retention(p, s0, batch, t_rows, out_dtype):
    m, n = p.shape
    h = RET_HEADS
    dk = n // (6 * h)
    dv = 2 * dk
    c_eff = math.gcd(t_rows, RET_CHUNK)
    assert t_rows % c_eff == 0 and (t_rows >= RET_CHUNK or t_rows == c_eff)
    intra, qd, kd, cd = _ret_tables(c_eff, dk)
    has_s0 = s0 is not None
    in_specs = [
        pl.BlockSpec((t_rows, dk), lambda b, hh: (b, hh)),
        pl.BlockSpec((t_rows, dk), lambda b, hh: (b, h + hh)),
        pl.BlockSpec((t_rows, dv), lambda b, hh: (b, h + hh)),
        pl.BlockSpec((t_rows, dv), lambda b, hh: (b, 2 * h + hh)),
        pl.BlockSpec((None, RET_CHUNK, RET_CHUNK), lambda b, hh: (hh, 0, 0)),
        pl.BlockSpec((None, RET_CHUNK, 1), lambda b, hh: (hh, 0, 0)),
        pl.BlockSpec((None, RET_CHUNK, 1), lambda b, hh: (hh, 0, 0)),
        pl.BlockSpec((None, 1, 1), lambda b, hh: (hh, 0, 0)),
    ]
    args = [p, p, p, p, intra, qd, kd, cd]
    if has_s0:
        in_specs.append(pl.BlockSpec((None, None, dk, dv), lambda b, hh: (b, hh, 0, 0)))
        args.append(s0)
    return pl.pallas_call(
        functools.partial(_ret_kernel, t_rows=t_rows, has_s0=has_s0),
        out_shape=(jax.ShapeDtypeStruct((m, h * dv), out_dtype),
                   jax.ShapeDtypeStruct((batch, h, dk, dv), F32)),
        grid=(batch, h),
        in_specs=in_specs,
        out_specs=(pl.BlockSpec((t_rows, dv), lambda b, hh: (b, hh)),
                   pl.BlockSpec((None, None, dk, dv), lambda b, hh: (b, hh, 0, 0))),
        scratch_shapes=[pltpu.VMEM((dk, dv), F32)],
        compiler_params=_cparams(("parallel", "parallel")),
        name="retention",
    )(*args)


def _hgrn_kernel(*refs, layer, t_rows, has_s0):
    if has_s0:
        (q_ref, f_ref, i_ref, g_ref, lbl_ref, og_ref, s0_ref, o_ref, s_ref,
         st, b_scr, q_scr, k_scr, v_scr, od_scr) = refs
        st[...] = s0_ref[...].T
    else:
        (q_ref, f_ref, i_ref, g_ref, lbl_ref, og_ref, o_ref, s_ref,
         st, b_scr, q_scr, k_scr, v_scr, od_scr) = refs
        st[...] = jnp.zeros_like(st)
    c = HG_CHUNK
    r = HG_SUB
    n_sub = c // r
    t_pad = max(t_rows, c)

    logits = lbl_ref[...]
    e = jnp.exp(logits - jnp.max(logits, axis=0, keepdims=True))
    prob = e / jnp.sum(e, axis=0, keepdims=True)
    lrow = lax.broadcasted_iota(jnp.int32, prob.shape, 0)
    in_lb = (lrow >= 1) & (lrow <= layer)
    lb = jnp.sum(jnp.where(in_lb, prob, 0.0), axis=0, keepdims=True)
    one_m_lb = jnp.sum(jnp.where(in_lb, 0.0, prob), axis=0, keepdims=True)

    ri = lax.broadcasted_iota(jnp.int32, (c, c), 0)
    cj = lax.broadcasted_iota(jnp.int32, (c, c), 1)
    tri = (ri >= cj).astype(F32)
    row_c = lax.broadcasted_iota(jnp.int32, (c, 1), 0)
    row_r = lax.broadcasted_iota(jnp.int32, (r, 1), 0)
    og = og_ref[...]

    def chunk(ci, carry):
        if t_rows >= c:
            rows = pl.ds(pl.multiple_of(ci * c, c), c)
        else:
            rows = pl.ds(0, t_rows)
        z = _pad_rows(f_ref[rows, :].astype(F32), c)
        qraw = _pad_rows(q_ref[rows, :].astype(F32), c)
        v = _pad_rows(i_ref[rows, :].astype(F32), c)
        f = lb + one_m_lb * jax.nn.sigmoid(z)
        logf = jnp.log(f)
        kk = one_m_lb * jax.nn.sigmoid(-z)
        if t_rows < c:
            valid = row_c < t_rows
            logf = jnp.where(valid, logf, 0.0)
            kk = jnp.where(valid, kk, 0.0)
        b = jnp.dot(tri, logf, precision=lax.Precision.HIGHEST, preferred_element_type=F32)
        qs = _silu(qraw)
        b_scr[...] = b
        q_scr[...] = qs
        k_scr[...] = kk
        v_scr[...] = v
        s_t = st[...]
        vb = v.astype(BF16)
        o = _dot_nt((qs * jnp.exp(b)).astype(BF16), s_t.astype(BF16))

        a_rows = [jnp.zeros((r, c), F32)]
        for sb in range(1, n_sub):
            lo = sb * r
            b0 = b[lo - 1:lo, :]
            qh = qs[lo:lo + r, :] * jnp.exp(b[lo:lo + r, :] - b0)
            kh = kk[:lo, :] * jnp.exp(b0 - b[:lo, :])
            a_rows.append(_dot_nt(qh.astype(BF16), _pad_rows(kh, c).astype(BF16)))
        o = o + _dot(jnp.concatenate(a_rows, axis=0).astype(BF16), vb)

        def diag(sb, carry2):
            base = pl.multiple_of(sb * r, r)
            bi = b_scr[pl.ds(base, r), :]
            qi = q_scr[pl.ds(base, r), :]
            acc = jnp.zeros((r, b.shape[1]), F32)
            for s in range(r):
                bs = b_scr[pl.ds(base + s, 1), :]
                ks = k_scr[pl.ds(base + s, 1), :]
                vs = v_scr[pl.ds(base + s, 1), :]
                w = qi * jnp.exp(jnp.minimum(bi - bs, 0.0)) * ks
                col = jnp.sum(w, axis=-1, keepdims=True)
                acc = acc + jnp.where(row_r >= s, col, 0.0) * vs
            od_scr[pl.ds(base, r), :] = acc
            return carry2

        lax.fori_loop(0, n_sub, diag, 0)
        o = o + od_scr[...]

        b_end = b[c - 1:c, :]
        kd = kk * jnp.exp(b_end - b)
        st[...] = s_t * jnp.exp(b_end) + _dot_tn(vb, kd.astype(BF16))

        o = o * lax.rsqrt(jnp.mean(o * o, axis=-1, keepdims=True) + EPS) * og
        gate = g_ref[rows, :].astype(F32)
        o_ref[rows, :] = (o[: gate.shape[0]] * _silu(gate)).astype(o_ref.dtype)
        return carry

    lax.fori_loop(0, t_pad // c, chunk, 0)
    s_ref[...] = st[...].T


def _hgrn(p, lb_logits, o_gain, s0, layer, batch, t_rows, out_dtype):
    m, n = p.shape
    h = HG_HEADS
    dk = HG_DK
    dv = n // (4 * h)
    depth = lb_logits.shape[0]
    assert dv == dk and (t_rows % HG_CHUNK == 0 or t_rows < HG_CHUNK)
    has_s0 = s0 is not None
    in_specs = [
        pl.BlockSpec((t_rows, dk), lambda b, hh: (b, hh)),
        pl.BlockSpec((t_rows, dk), lambda b, hh: (b, h + hh)),
        pl.BlockSpec((t_rows, dv), lambda b, hh: (b, 2 * h + hh)),
        pl.BlockSpec((t_rows, dv), lambda b, hh: (b, 3 * h + hh)),
        pl.BlockSpec((depth, dk), lambda b, hh: (0, hh)),
        pl.BlockSpec((1, dv), lambda b, hh: (0, 0)),
    ]
    args = [p, p, p, p, lb_logits, o_gain.reshape(1, dv)]
    if has_s0:
        in_specs.append(pl.BlockSpec((None, None, dk, dv), lambda b, hh: (b, hh, 0, 0)))
        args.append(s0)
    cshape = (HG_CHUNK, dk)
    return pl.pallas_call(
        functools.partial(_hgrn_kernel, layer=layer, t_rows=t_rows, has_s0=has_s0),
        out_shape=(jax.ShapeDtypeStruct((m, h * dv), out_dtype),
                   jax.ShapeDtypeStruct((batch, h, dk, dv), F32)),
        grid=(batch, h),
        in_specs=in_specs,
        out_specs=(pl.BlockSpec((t_rows, dv), lambda b, hh: (b, hh)),
                   pl.BlockSpec((None, None, dk, dv), lambda b, hh: (b, hh, 0, 0))),
        scratch_shapes=[pltpu.VMEM((dv, dk), F32)] + [pltpu.VMEM(cshape, F32)] * 5,
        compiler_params=_cparams(("parallel", "parallel")),
        name="hgrn2",
    )(*args)


def _head_norm(x, gain):
    return x * lax.rsqrt(jnp.mean(x * x, axis=-1, keepdims=True) + EPS) * gain


def _alibi_slope(h_index, n_heads):
    hv = jnp.full((1, 1), h_index + 1, jnp.int32).astype(F32)
    return jnp.exp2(hv * (-8.0 / n_heads))


def _moba_prompt_kernel(q_ref, k_ref, v_ref, g_ref, qg_ref, kg_ref, o_ref, kn_ref,
                        kb_scr, vb_scr, km_scr, m_scr, l_scr, acc_scr, *, n_blocks):
    blk = MOBA_BLOCK
    h = pl.program_id(1)
    c = pl.program_id(2)
    hd = q_ref.shape[1]
    scale = hd ** -0.5

    @pl.when(c == 0)
    def _():
        kn = _head_norm(k_ref[...], kg_ref[...])
        kn_ref[...] = kn
        kb_scr[...] = kn.astype(BF16)
        vb_scr[...] = v_ref[...].astype(BF16)
        means = [jnp.mean(kn[n * blk:(n + 1) * blk, :], axis=0, keepdims=True) for n in range(n_blocks)]
        km_scr[...] = _pad_rows(jnp.concatenate(means, axis=0), LANES)

    qn = _head_norm(q_ref[...], qg_ref[...])
    qb = qn.astype(BF16)
    slope = _alibi_slope(h, MOBA_HEADS)

    gate = _dot_nt(qn, km_scr[...], precision=lax.Precision.HIGHEST)
    lane = lax.broadcasted_iota(jnp.int32, gate.shape, 1)
    gm = jnp.where(lane < c, gate, NEG_INF)
    rank = jnp.zeros(gate.shape, jnp.int32)
    for sh in range(1, n_blocks):
        other = pltpu.roll(gm, sh, axis=1)
        ahead = (other > gm) | ((other == gm) & (lane >= sh))
        rank = rank + ahead.astype(jnp.int32)
    bias = jnp.where((lane < c) & (rank < MOBA_TOPK), 0.0, NEG_INF)

    ti = lax.broadcasted_iota(jnp.int32, (blk, blk), 0)
    sj = lax.broadcasted_iota(jnp.int32, (blk, blk), 1)
    rel = (ti - sj).astype(F32)

    c0 = pl.multiple_of(c * blk, blk)
    s_own = _dot_nt(qb, kb_scr[pl.ds(c0, blk), :]) * scale - slope * rel
    s_own = jnp.where(ti >= sj, s_own, NEG_INF)
    m0 = jnp.max(s_own, axis=-1, keepdims=True)
    p0 = jnp.exp(s_own - m0)
    m_scr[...] = m0
    l_scr[...] = jnp.sum(p0, axis=-1, keepdims=True)
    acc_scr[...] = _dot(p0.astype(BF16), vb_scr[pl.ds(c0, blk), :])

    for n in range(n_blocks - 1):
        @pl.when(n < c)
        def _(n=n):
            dist = rel + ((c - n) * blk).astype(F32)
            s = _dot_nt(qb, kb_scr[n * blk:(n + 1) * blk, :]) * scale - slope * dist + bias[:, n:n + 1]
            m_old = m_scr[...]
            m_new = jnp.maximum(m_old, jnp.max(s, axis=-1, keepdims=True))
            alpha = jnp.exp(m_old - m_new)
            p = jnp.exp(s - m_new)
            m_scr[...] = m_new
            l_scr[...] = alpha * l_scr[...] + jnp.sum(p, axis=-1, keepdims=True)
            acc_scr[...] = alpha * acc_scr[...] + _dot(p.astype(BF16), vb_scr[n * blk:(n + 1) * blk, :])

    out = acc_scr[...] / l_scr[...]
    o_ref[...] = (out * _silu(g_ref[...].astype(F32))).astype(o_ref.dtype)


def _moba_prompt(p, q_gain, k_gain, batch, t_rows, out_dtype):
    m, n = p.shape
    h = MOBA_HEADS
    hd = MOBA_HD
    assert n == 4 * h * hd and t_rows % MOBA_BLOCK == 0
    nb = t_rows // MOBA_BLOCK
    blk = MOBA_BLOCK
    return pl.pallas_call(
        functools.partial(_moba_prompt_kernel, n_blocks=nb),
        out_shape=(jax.ShapeDtypeStruct((m, h * hd), out_dtype),
                   jax.ShapeDtypeStruct((m, h * hd), F32)),
        grid=(batch, h, nb),
        in_specs=[
            pl.BlockSpec((blk, hd), lambda b, hh, c: (b * nb + c, hh)),
            pl.BlockSpec((t_rows, hd), lambda b, hh, c: (b, h + hh)),
            pl.BlockSpec((t_rows, hd), lambda b, hh, c: (b, 2 * h + hh)),
            pl.BlockSpec((blk, hd), lambda b, hh, c: (b * nb + c, 3 * h + hh)),
            pl.BlockSpec((1, hd), lambda b, hh, c: (0, 0)),
            pl.BlockSpec((1, hd), lambda b, hh, c: (0, 0)),
        ],
        out_specs=(pl.BlockSpec((blk, hd), lambda b, hh, c: (b * nb + c, hh)),
                   pl.BlockSpec((t_rows, hd), lambda b, hh, c: (b, hh))),
        scratch_shapes=[
            pltpu.VMEM((t_rows, hd), BF16),
            pltpu.VMEM((t_rows, hd), BF16),
            pltpu.VMEM((LANES, hd), F32),
            pltpu.VMEM((blk, 1), F32),
            pltpu.VMEM((blk, 1), F32),
            pltpu.VMEM((blk, hd), F32),
        ],
        compiler_params=_cparams(("parallel", "parallel", "arbitrary")),
        name="moba_prompt",
    )(p, p, p, p, q_gain.reshape(1, hd), k_gain.reshape(1, hd))


def _kmean_kernel(pt_ref, pa_ref, pb_ref, o_ref):
    del pt_ref
    n = pl.program_id(1)
    tot = jnp.sum(pa_ref[...], axis=0) + jnp.sum(pb_ref[...], axis=0)
    o_ref[n] = tot * (1.0 / MOBA_BLOCK)


def _moba_kmean(cache, page_table):
    _, page, heads, hd = cache.shape
    db, n_pages = page_table.shape
    assert MOBA_BLOCK == 2 * page and n_pages % 2 == 0
    nb = n_pages // 2
    return pl.pallas_call(
        _kmean_kernel,
        out_shape=jax.ShapeDtypeStruct((db, nb, heads, hd), F32),
        grid_spec=pltpu.PrefetchScalarGridSpec(
            num_scalar_prefetch=1,
            grid=(db, nb),
            in_specs=[
                pl.BlockSpec((None, page, heads, hd), lambda b, n, pt: (pt[b, 2 * n], 0, 0, 0)),
                pl.BlockSpec((None, page, heads, hd), lambda b, n, pt: (pt[b, 2 * n + 1], 0, 0, 0)),
            ],
            out_specs=pl.BlockSpec((None, nb, heads, hd), lambda b, n, pt: (b, 0, 0, 0)),
        ),
        compiler_params=_cparams(("parallel", "arbitrary")),
        name="moba_past_block_means",
    )(page_table, cache, cache)


def _moba_select_kernel(q_ref, k_ref, km_ref, qg_ref, kg_ref, qn_ref, kn_ref, idx_ref, *, n_blocks):
    hd = MOBA_HD
    t = q_ref.shape[0]
    lane = lax.broadcasted_iota(jnp.int32, (t, LANES), 1)
    for h in range(MOBA_HEADS):
        cols = slice(h * hd, (h + 1) * hd)
        qn = _head_norm(q_ref[:, cols], qg_ref[...])
        kn = _head_norm(k_ref[:, cols], kg_ref[...])
        qn_ref[:, cols] = qn
        kn_ref[:, cols] = kn
        km = _pad_rows(km_ref[:, h, :], LANES)
        gate = _dot_nt(qn, km, precision=lax.Precision.HIGHEST)
        gm = jnp.where(lane < n_blocks, gate, NEG_INF)
        picks = jnp.zeros((t, LANES), jnp.int32)
        for j in range(MOBA_TOPK):
            best = jnp.max(gm, axis=-1, keepdims=True)
            ix = jnp.min(jnp.where(gm == best, lane, LANES), axis=-1, keepdims=True)
            picks = jnp.where(lane == j, ix, picks)
            gm = jnp.where(lane == ix, NEG_INF, gm)
        idx_ref[h] = picks


def _moba_select(p, kmean, q_gain, k_gain, batch, t_rows):
    m, n = p.shape
    w = MOBA_HEADS * MOBA_HD
    nb = kmean.shape[1]
    assert nb >= MOBA_TOPK and nb <= LANES
    return pl.pallas_call(
        functools.partial(_moba_select_kernel, n_blocks=nb),
        out_shape=(jax.ShapeDtypeStruct((m, w), F32),
                   jax.ShapeDtypeStruct((m, w), F32),
                   jax.ShapeDtypeStruct((batch, MOBA_HEADS, t_rows, LANES), jnp.int32)),
        grid=(batch,),
        in_specs=[
            pl.BlockSpec((t_rows, w), lambda b: (b, 0)),
            pl.BlockSpec((t_rows, w), lambda b: (b, 1)),
            pl.BlockSpec((None, nb, MOBA_HEADS, MOBA_HD), lambda b: (b, 0, 0, 0)),
            pl.BlockSpec((1, MOBA_HD), lambda b: (0, 0)),
            pl.BlockSpec((1, MOBA_HD), lambda b: (0, 0)),
        ],
        out_specs=(pl.BlockSpec((t_rows, w), lambda b: (b, 0)),
                   pl.BlockSpec((t_rows, w), lambda b: (b, 0)),
                   pl.BlockSpec((None, MOBA_HEADS, t_rows, LANES), lambda b: (b, 0, 0, 0))),
        compiler_params=_cparams(("parallel",)),
        name="moba_sample_select",
    )(p, p, kmean, q_gain.reshape(1, MOBA_HD), k_gain.reshape(1, MOBA_HD))


def _moba_sample_attn_kernel(idx_ref, pt_ref, ck_ref, cv_ref, qn_ref, kn_ref, vn_ref, g_ref, o_ref,
                             kbuf, vbuf, sem, *, past_len, t_rows):
    step = pl.program_id(0)
    n_steps = pl.num_programs(0)
    h = step % MOBA_HEADS
    hd = MOBA_HD
    blk = MOBA_BLOCK
    page = blk // 2
    n_sel = t_rows * MOBA_TOPK
    scale = hd ** -0.5
    slope = _alibi_slope(h, MOBA_HEADS)

    def for_each_copy(s, slot, fn):
        sb = s // MOBA_HEADS
        sh = s % MOBA_HEADS

        def body(u, carry):
            blk_id = idx_ref[s * n_sel + u]
            for half in range(2):
                pg = pt_ref[sb, 2 * blk_id + half]
                rows = pl.ds(half * page, page)
                fn(pltpu.make_async_copy(ck_ref.at[pg, :, sh, :], kbuf.at[slot, u, rows, :], sem.at[slot]))
                fn(pltpu.make_async_copy(cv_ref.at[pg, :, sh, :], vbuf.at[slot, u, rows, :], sem.at[slot]))
            return carry

        lax.fori_loop(0, n_sel, body, 0)

    slot = step % 2

    @pl.when(step == 0)
    def _():
        for_each_copy(step, slot, lambda cp: cp.start())

    @pl.when(step + 1 < n_steps)
    def _():
        for_each_copy(step + 1, 1 - slot, lambda cp: cp.start())

    for_each_copy(step, slot, lambda cp: cp.wait())

    offs = lax.broadcasted_iota(jnp.int32, (1, blk), 1)
    key_i = lax.broadcasted_iota(jnp.int32, (t_rows, 1), 0)
    kn = kn_ref[...]
    vn = vn_ref[...]

    def per_query(qi, carry):
        qrow = qn_ref[pl.ds(qi, 1), :]
        q8 = jnp.broadcast_to(qrow, (8, hd)).astype(BF16)
        t_pos = past_len + qi
        s_own = jnp.sum(kn * qrow, axis=-1, keepdims=True) * scale - slope * (qi - key_i).astype(F32)
        s_own = jnp.where(key_i <= qi, s_own, NEG_INF)
        m = jnp.max(s_own, axis=0, keepdims=True)
        scores = []
        for j in range(MOBA_TOPK):
            u = qi * MOBA_TOPK + j
            dist = (t_pos - idx_ref[step * n_sel + u] * blk - offs).astype(F32)
            s = _dot_nt(q8, kbuf[slot, u].astype(BF16))[0:1, :] * scale - slope * dist
            scores.append(s)
            m = jnp.maximum(m, jnp.max(s, axis=-1, keepdims=True))
        p_own = jnp.exp(s_own - m)
        denom = jnp.sum(p_own, axis=0, keepdims=True)
        out = jnp.sum(p_own * vn, axis=0, keepdims=True)
        for j, s in enumerate(scores):
            u = qi * MOBA_TOPK + j
            pj = jnp.exp(s - m)
            denom = denom + jnp.sum(pj, axis=-1, keepdims=True)
            out = out + _dot(jnp.broadcast_to(pj, (8, blk)).astype(BF16), vbuf[slot, u].astype(BF16))[0:1, :]
        gate = g_ref[pl.ds(qi, 1), :]
        o_ref[pl.ds(qi, 1), :] = out / denom * _silu(gate)
        return carry

    lax.fori_loop(0, t_rows, per_query, 0)


def _moba_sample_attn(idx, page_table, cache_k, cache_v, qn, kn, p, batch, t_rows):
    h = MOBA_HEADS
    hd = MOBA_HD
    _, page, heads, _ = cache_k.shape
    n_pages = page_table.shape[1]
    assert MOBA_BLOCK == 2 * page and heads == h
    n_sel = t_rows * MOBA_TOPK
    row_spec = lambda off: pl.BlockSpec((t_rows, hd), lambda s, ix, pt: (s // h, off * h + s % h))
    return pl.pallas_call(
        functools.partial(_moba_sample_attn_kernel, past_len=n_pages * page, t_rows=t_rows),
        out_shape=jax.ShapeDtypeStruct((batch * t_rows, h * hd), F32),
        grid_spec=pltpu.PrefetchScalarGridSpec(
            num_scalar_prefetch=2,
            grid=(batch * h,),
            in_specs=[pl.BlockSpec(memory_space=pl.ANY), pl.BlockSpec(memory_space=pl.ANY),
                      row_spec(0), row_spec(0), row_spec(2), row_spec(3)],
            out_specs=row_spec(0),
            scratch_shapes=[pltpu.VMEM((2, n_sel, MOBA_BLOCK, hd), F32),
                            pltpu.VMEM((2, n_sel, MOBA_BLOCK, hd), F32),
                            pltpu.SemaphoreType.DMA((2,))],
        ),
        compiler_params=_cparams(("arbitrary",)),
        name="moba_sample_attention",
    )(idx, page_table, cache_k, cache_v, qn, kn, p, p)


def _retention_layer(h2d, gain, w_in, w_out, s0, batch, t_rows):
    big = t_rows >= RET_CHUNK
    p = _proj(h2d, gain, w_in, 0, w_in.shape[1], BF16 if big else F32)
    o, s = _retention(p, s0, batch, t_rows, BF16 if big else F32)
    return _oproj(o, w_out, h2d), s


def _hgrn_layer(h2d, gain, w_in, lb_logits, o_gain, w_out, s0, layer, batch, t_rows):
    p = _proj(h2d, gain, w_in, 0, w_in.shape[1], F32)
    o, s = _hgrn(p, lb_logits, o_gain, s0, layer, batch, t_rows, BF16 if t_rows >= HG_CHUNK else F32)
    return _oproj(o, w_out, h2d), s


def _moba_prompt_layer(h2d, gain, w_in, q_gain, k_gain, w_out, batch, t_rows):
    w = MOBA_HEADS * MOBA_HD
    p = _proj(h2d, gain, w_in, 0, w_in.shape[1], F32)
    o, kn = _moba_prompt(p, q_gain, k_gain, batch, t_rows, BF16)
    return _oproj(o, w_out, h2d), kn, p[:, 2 * w:3 * w]


def _moba_sample_layer(h2d, gain, w_in, q_gain, k_gain, w_out, cache_k, cache_v, page_table, batch, t_rows):
    w = MOBA_HEADS * MOBA_HD
    p = _proj(h2d, gain, w_in, 0, w_in.shape[1], F32)
    kmean = _moba_kmean(cache_k, page_table)
    qn, kn, picks = _moba_select(p, kmean, q_gain, k_gain, batch, t_rows)
    idx = picks[..., :MOBA_TOPK].reshape(-1)
    o = _moba_sample_attn(idx, page_table, cache_k, cache_v, qn, kn, p, batch, t_rows)
    return _oproj(o, w_out, h2d), kn, p[:, 2 * w:3 * w]


def kernel(x_prompt, x_sample, state_ret, cache_k, cache_v, state_hgrn, page_table, norm_gain, ret_w_in, ret_w_out, moba_w_in, moba_q_gain, moba_k_gain, moba_w_out, hgrn_w_in, hgrn_lb_logits, hgrn_o_gain, hgrn_w_out):
    bp, tp, d = x_prompt.shape
    bs, ts, _ = x_sample.shape
    depth = norm_gain.shape[0]
    n_pool = cache_k.shape[1]
    hp = x_prompt.reshape(bp * tp, d)
    hs = x_sample.reshape(bs * ts, d)
    kp_l, vp_l, ks_l, vs_l, rp_l, rs_l, gp_l, gs_l = [], [], [], [], [], [], [], []
    for layer in range(depth):
        kind, slot = layer % N_MIXERS, layer // N_MIXERS
        gain = norm_gain[layer]
        if kind == 0:
            w_in = ret_w_in[slot].astype(BF16)
            w_out = ret_w_out[slot].astype(BF16)
            hp, st_p = _retention_layer(hp, gain, w_in, w_out, None, bp, tp)
            hs, st_s = _retention_layer(hs, gain, w_in, w_out, state_ret[slot], bs, ts)
            rp_l.append(st_p)
            rs_l.append(st_s)
        elif kind == 1:
            w_in = moba_w_in[slot].astype(BF16)
            w_out = moba_w_out[slot].astype(BF16)
            ck = cache_k.reshape((-1,) + cache_k.shape[2:])
            cv = cache_v.reshape((-1,) + cache_v.shape[2:])
            hp, kp, vp = _moba_prompt_layer(hp, gain, w_in, moba_q_gain[slot], moba_k_gain[slot], w_out, bp, tp)
            hs, ks_, vs_ = _moba_sample_layer(hs, gain, w_in, moba_q_gain[slot], moba_k_gain[slot], w_out,
                                              ck, cv, page_table + slot * n_pool, bs, ts)
            shp = (MOBA_HEADS, MOBA_HD)
            kp_l.append(kp.reshape(bp, tp, *shp))
            vp_l.append(vp.reshape(bp, tp, *shp))
            ks_l.append(ks_.reshape(bs, ts, *shp))
            vs_l.append(vs_.reshape(bs, ts, *shp))
        else:
            w_in = hgrn_w_in[slot].astype(BF16)
            w_out = hgrn_w_out[slot].astype(BF16)
            hp, st_p = _hgrn_layer(hp, gain, w_in, hgrn_lb_logits, hgrn_o_gain[slot], w_out, None, layer, bp, tp)
            hs, st_s = _hgrn_layer(hs, gain, w_in, hgrn_lb_logits, hgrn_o_gain[slot], w_out,
                                   state_hgrn[slot], layer, bs, ts)
            gp_l.append(st_p)
            gs_l.append(st_s)
    return (hp.reshape(bp, tp, d), hs.reshape(bs, ts, d),
            jnp.stack(kp_l), jnp.stack(vp_l), jnp.stack(ks_l), jnp.stack(vs_l),
            jnp.stack(rp_l), jnp.stack(rs_l), jnp.stack(gp_l), jnp.stack(gs_l))
```

```python
import functools
import math

import numpy as np
import jax
import jax.numpy as jnp
from jax import lax
from jax.experimental import pallas as pl
from jax.experimental.pallas import tpu as pltpu

F32 = jnp.float32
BF16 = jnp.bfloat16
EPS = 1e-6
NEG_INF = float("-inf")
LOG2E = 1.4426950408889634
MASK_NEG = -1e30

N_MIXERS = 3
RET_HEADS = 8
RET_CHUNK = 128
MOBA_HEADS = 16
MOBA_HD = 128
MOBA_BLOCK = 256
MOBA_TOPK = 3
HG_HEADS = 16
HG_DK = 128
HG_CHUNK = 128
HG_SUB = 16

VMEM_LIMIT_BYTES = 52 * 1024 * 1024
LANES = 128


def _cparams(sem):
    return pltpu.CompilerParams(dimension_semantics=sem, vmem_limit_bytes=VMEM_LIMIT_BYTES)


def _dot(a, b):
    return jnp.dot(a, b, preferred_element_type=F32)


def _dot_nt(a, b, precision=None):
    return lax.dot_general(a, b, (((1,), (1,)), ((), ())), preferred_element_type=F32, precision=precision)


def _dot_tn(a, b):
    return lax.dot_general(a, b, (((0,), (0,)), ((), ())), preferred_element_type=F32)


def _silu(x):
    return x * jax.nn.sigmoid(x)


def _pad_rows(x, rows):
    if x.shape[0] == rows:
        return x
    return jnp.concatenate([x, jnp.zeros((rows - x.shape[0], x.shape[1]), x.dtype)], axis=0)


def _proj_kernel(x_ref, gain_ref, w_ref, o_ref, xn_ref):
    @pl.when(pl.program_id(1) == 0)
    def _():
        x = x_ref[...]
        ms = jnp.mean(x * x, axis=-1, keepdims=True)
        xn_ref[...] = (x * lax.rsqrt(ms + EPS) * gain_ref[...]).astype(BF16)

    o_ref[...] = _dot(xn_ref[...], w_ref[...]).astype(o_ref.dtype)


def _proj(x, gain, w, col0, ncols, out_dtype):
    m, d = x.shape
    tm = min(m, 1024)
    tn = 512
    assert m % tm == 0 and ncols % tn == 0 and col0 % tn == 0
    jb = col0 // tn
    return pl.pallas_call(
        _proj_kernel,
        out_shape=jax.ShapeDtypeStruct((m, ncols), out_dtype),
        grid=(m // tm, ncols // tn),
        in_specs=[
            pl.BlockSpec((tm, d), lambda i, j: (i, 0)),
            pl.BlockSpec((1, d), lambda i, j: (0, 0)),
            pl.BlockSpec((d, tn), lambda i, j: (0, j + jb)),
        ],
        out_specs=pl.BlockSpec((tm, tn), lambda i, j: (i, j)),
        scratch_shapes=[pltpu.VMEM((tm, d), BF16)],
        compiler_params=_cparams(("parallel", "arbitrary")),
        name="norm_in_proj",
    )(x, gain.reshape(1, d), w)


def _oproj_kernel(a_ref, w_ref, r_ref, o_ref):
    o_ref[...] = _dot(a_ref[...].astype(BF16), w_ref[...]) + r_ref[...]


def _oproj(a, w, resid):
    m, k = a.shape
    n = w.shape[1]
    tm = min(m, 1024)
    tn = 512
    assert m % tm == 0 and n % tn == 0
    return pl.pallas_call(
        _oproj_kernel,
        out_shape=jax.ShapeDtypeStruct((m, n), F32),
        grid=(m // tm, n // tn),
        in_specs=[
            pl.BlockSpec((tm, k), lambda i, j: (i, 0)),
            pl.BlockSpec((k, tn), lambda i, j: (0, j)),
            pl.BlockSpec((tm, tn), lambda i, j: (i, j)),
        ],
        out_specs=pl.BlockSpec((tm, tn), lambda i, j: (i, j)),
        compiler_params=_cparams(("parallel", "parallel")),
        name="out_proj_residual",
    )(a, w, resid)


def _ret_tables(c, dk):
    scale = dk ** -0.5
    log_g = np.log1p(-np.exp2(-5.0 - np.arange(RET_HEADS, dtype=np.float64)))
    i = np.arange(c, dtype=np.float64)
    rel = i[:, None] - i[None, :]
    intra = np.where(rel >= 0, np.exp(log_g[:, None, None] * np.maximum(rel, 0.0)), 0.0) * scale
    q_dec = np.exp(log_g[:, None] * (i + 1.0))
    k_dec = np.exp(log_g[:, None] * (c - 1.0 - i)) * scale
    c_dec = np.exp(log_g * c)
    big = RET_CHUNK
    intra_p = np.zeros((RET_HEADS, big, big), np.float32)
    intra_p[:, :c, :c] = intra
    qd = np.zeros((RET_HEADS, big, 1), np.float32)
    qd[:, :c, 0] = q_dec
    kd = np.zeros((RET_HEADS, big, 1), np.float32)
    kd[:, :c, 0] = k_dec
    cd = c_dec.astype(np.float32).reshape(RET_HEADS, 1, 1)
    return jnp.asarray(intra_p), jnp.asarray(qd), jnp.asarray(kd), jnp.asarray(cd)


def _ret_kernel(*refs, t_rows, has_s0):
    if has_s0:
        q_ref, k_ref, v_ref, g_ref, intra_ref, qd_ref, kd_ref, cd_ref, s0_ref, o_ref, s_ref, st = refs
        st[...] = s0_ref[...]
    else:
        q_ref, k_ref, v_ref, g_ref, intra_ref, qd_ref, kd_ref, cd_ref, o_ref, s_ref, st = refs
        st[...] = jnp.zeros_like(st)
    c = RET_CHUNK
    t_pad = max(t_rows, c)
    intra = intra_ref[...]
    qd = qd_ref[...]
    kd = kd_ref[...]
    cd = cd_ref[...]

    def chunk(ci, carry):
        if t_rows >= c:
            rows = pl.ds(pl.multiple_of(ci * c, c), c)
        else:
            rows = pl.ds(0, t_rows)
        qc = _pad_rows(q_ref[rows, :], c).astype(BF16)
        kf = _pad_rows(k_ref[rows, :], c).astype(F32)
        vc = _pad_rows(v_ref[rows, :], c).astype(BF16)
        s = st[...]
        att = _dot_nt(qc, kf.astype(BF16)) * intra
        o = _dot(att.astype(BF16), vc) + qd * _dot(qc, s.astype(BF16))
        st[...] = s * cd + _dot_tn((kf * kd).astype(BF16), vc)
        o = o * lax.rsqrt(jnp.mean(o * o, axis=-1, keepdims=True) + EPS)
        gate = g_ref[rows, :].astype(F32)
        o_ref[rows, :] = (o[: gate.shape[0]] * _silu(gate)).astype(o_ref.dtype)
        return carry

    n_chunks = t_pad // c
    lax.fori_loop(0, n_chunks, chunk, 0, unroll=2 if n_chunks % 2 == 0 else 1)
    s_ref[...] = st[...]


def _retention(p, s0, batch, t_rows, out_dtype):
    m, n = p.shape
    h = RET_HEADS
    dk = n // (6 * h)
    dv = 2 * dk
    c_eff = math.gcd(t_rows, RET_CHUNK)
    assert t_rows % c_eff == 0 and (t_rows >= RET_CHUNK or t_rows == c_eff)
    intra, qd, kd, cd = _ret_tables(c_eff, dk)
    has_s0 = s0 is not None
    in_specs = [
        pl.BlockSpec((t_rows, dk), lambda b, hh: (b, hh)),
        pl.BlockSpec((t_rows, dk), lambda b, hh: (b, h + hh)),
        pl.BlockSpec((t_rows, dv), lambda b, hh: (b, h + hh)),
        pl.BlockSpec((t_rows, dv), lambda b, hh: (b, 2 * h + hh)),
        pl.BlockSpec((None, RET_CHUNK, RET_CHUNK), lambda b, hh: (hh, 0, 0)),
        pl.BlockSpec((None, RET_CHUNK, 1), lambda b, hh: (hh, 0, 0)),
        pl.BlockSpec((None, RET_CHUNK, 1), lambda b, hh: (hh, 0, 0)),
        pl.BlockSpec((None, 1, 1), lambda b, hh: (hh, 0, 0)),
    ]
    args = [p, p, p, p, intra, qd, kd, cd]
    if has_s0:
        in_specs.append(pl.BlockSpec((None, None, dk, dv), lambda b, hh: (b, hh, 0, 0)))
        args.append(s0)
    return pl.pallas_call(
        functools.partial(_ret_kernel, t_rows=t_rows, has_s0=has_s0),
        out_shape=(jax.ShapeDtypeStruct((m, h * dv), out_dtype),
                   jax.ShapeDtypeStruct((batch, h, dk, dv), F32)),
        grid=(batch, h),
        in_specs=in_specs,
        out_specs=(pl.BlockSpec((t_rows, dv), lambda b, hh: (b, hh)),
                   pl.BlockSpec((None, None, dk, dv), lambda b, hh: (b, hh, 0, 0))),
        scratch_shapes=[pltpu.VMEM((dk, dv), F32)],
        compiler_params=_cparams(("parallel", "parallel")),
        name="retention",
    )(*args)


def _hgrn_kernel(*refs, layer, t_rows, chunk, has_s0):
    n_in = 7 if has_s0 else 6
    q_ref, f_ref, i_ref, g_ref, lbl_ref, og_ref = refs[:6]
    o_ref, s_ref = refs[n_in:n_in + 2]
    scratch_sets = (refs[n_in + 2:n_in + 5], refs[n_in + 5:n_in + 8])
    c = chunk
    r = HG_SUB
    n_sub = c // r
    n_chunks = max(t_rows, c) // c
    dk = q_ref.shape[1]

    logits = lbl_ref[...]
    e = jnp.exp(logits - jnp.max(logits, axis=0, keepdims=True))
    prob = e / jnp.sum(e, axis=0, keepdims=True)
    lrow = lax.broadcasted_iota(jnp.int32, prob.shape, 0)
    in_lb = (lrow >= 1) & (lrow <= layer)
    lb = jnp.sum(jnp.where(in_lb, prob, 0.0), axis=0, keepdims=True)
    one_m_lb = jnp.sum(jnp.where(in_lb, 0.0, prob), axis=0, keepdims=True)

    ri = lax.broadcasted_iota(jnp.int32, (c, LANES), 0)
    cj = lax.broadcasted_iota(jnp.int32, (c, LANES), 1)
    tri = (ri >= cj).astype(F32)
    row_c = lax.broadcasted_iota(jnp.int32, (c, dk), 0)
    row_r = lax.broadcasted_iota(jnp.int32, (r, dk), 0)
    og = og_ref[...]

    def one_chunk(ci, s_t, b_scr, k_scr, v_scr):
        if t_rows >= c:
            rows = pl.ds(pl.multiple_of(ci * c, c), c)
        else:
            rows = pl.ds(0, t_rows)
        z = _pad_rows(f_ref[rows, :].astype(F32), c)
        qraw = _pad_rows(q_ref[rows, :].astype(F32), c)
        v = _pad_rows(i_ref[rows, :].astype(F32), c)
        f = lb + one_m_lb * jax.nn.sigmoid(z)
        logf = jnp.log(f)
        kk = one_m_lb * jax.nn.sigmoid(-z)
        if t_rows < c:
            valid = row_c < t_rows
            logf = jnp.where(valid, logf, 0.0)
            kk = jnp.where(valid, kk, 0.0)
        b = jnp.dot(tri, _pad_rows(logf, LANES), precision=lax.Precision.HIGHEST,
                    preferred_element_type=F32) * LOG2E
        qs = _silu(qraw)
        b_scr[...] = b
        k_scr[...] = kk
        v_scr[...] = v
        vb = v.astype(BF16)
        o = _dot_nt((qs * jnp.exp2(b)).astype(BF16), s_t.astype(BF16))

        if n_sub > 1:
            a_rows = [jnp.zeros((r, c), F32)]
            for sb in range(1, n_sub):
                lo = sb * r
                b0 = b[lo - 1:lo, :]
                qh = qs[lo:lo + r, :] * jnp.exp2(b[lo:lo + r, :] - b0)
                kh = kk[:lo, :] * jnp.exp2(b0 - b[:lo, :])
                a_rows.append(_dot_nt(qh.astype(BF16), _pad_rows(kh, c).astype(BF16)))
            o = o + _dot(jnp.concatenate(a_rows, axis=0).astype(BF16), vb)

        o_diag = []
        for sb in range(n_sub):
            lo = sb * r
            bi = b[lo:lo + r, :]
            qi = qs[lo:lo + r, :]
            acc = jnp.zeros((r, dk), F32)
            for s in range(min(r, t_rows)):
                bs = b_scr[lo + s:lo + s + 1, :]
                ks = k_scr[lo + s:lo + s + 1, :]
                vs = v_scr[lo + s:lo + s + 1, :]
                d = bi - bs
                if s > 0:
                    d = jnp.where(row_r >= s, d, NEG_INF)
                col = jnp.sum(qi * jnp.exp2(d) * ks, axis=-1, keepdims=True)
                acc = acc + col * vs
            o_diag.append(acc)
        o = o + jnp.concatenate(o_diag, axis=0)

        b_end = b[c - 1:c, :]
        kd = kk * jnp.exp2(b_end - b)
        s_new = s_t * jnp.exp2(b_end) + _dot_tn(_pad_rows(vb, LANES), _pad_rows(kd.astype(BF16), LANES))

        o = o * lax.rsqrt(jnp.mean(o * o, axis=-1, keepdims=True) + EPS) * og
        gate = g_ref[rows, :].astype(F32)
        o_ref[rows, :] = (o[: gate.shape[0]] * _silu(gate)).astype(o_ref.dtype)
        return s_new

    s_init = refs[6][...].T if has_s0 else jnp.zeros((dk, dk), F32)
    if n_chunks == 1:
        s_fin = one_chunk(0, s_init, *scratch_sets[0])
    else:
        def pair(i, s_t):
            s_t = one_chunk(2 * i, s_t, *scratch_sets[0])
            return one_chunk(2 * i + 1, s_t, *scratch_sets[1])

        s_fin = lax.fori_loop(0, n_chunks // 2, pair, s_init)
    s_ref[...] = s_fin.T


def _hgrn(p, lb_logits, o_gain, s0, layer, batch, t_rows, out_dtype):
    m, n = p.shape
    h = HG_HEADS
    dk = HG_DK
    dv = n // (4 * h)
    depth = lb_logits.shape[0]
    chunk = HG_CHUNK if t_rows >= HG_CHUNK else HG_SUB
    n_chunks = max(t_rows, chunk) // chunk
    assert dv == dk == LANES and (t_rows % chunk == 0 or t_rows < chunk) and (n_chunks == 1 or n_chunks % 2 == 0)
    has_s0 = s0 is not None
    in_specs = [
        pl.BlockSpec((t_rows, dk), lambda b, hh: (b, hh)),
        pl.BlockSpec((t_rows, dk), lambda b, hh: (b, h + hh)),
        pl.BlockSpec((t_rows, dv), lambda b, hh: (b, 2 * h + hh)),
        pl.BlockSpec((t_rows, dv), lambda b, hh: (b, 3 * h + hh)),
        pl.BlockSpec((depth, dk), lambda b, hh: (0, hh)),
        pl.BlockSpec((1, dv), lambda b, hh: (0, 0)),
    ]
    args = [p, p, p, p, lb_logits, o_gain.reshape(1, dv)]
    if has_s0:
        in_specs.append(pl.BlockSpec((None, None, dk, dv), lambda b, hh: (b, hh, 0, 0)))
        args.append(s0)
    cshape = (chunk, dk)
    return pl.pallas_call(
        functools.partial(_hgrn_kernel, layer=layer, t_rows=t_rows, chunk=chunk, has_s0=has_s0),
        out_shape=(jax.ShapeDtypeStruct((m, h * dv), out_dtype),
                   jax.ShapeDtypeStruct((batch, h, dk, dv), F32)),
        grid=(batch, h),
        in_specs=in_specs,
        out_specs=(pl.BlockSpec((t_rows, dv), lambda b, hh: (b, hh)),
                   pl.BlockSpec((None, None, dk, dv), lambda b, hh: (b, hh, 0, 0))),
        scratch_shapes=[pltpu.VMEM(cshape, F32)] * 6,
        compiler_params=_cparams(("parallel", "parallel")),
        name="hgrn2",
    )(*args)


def _head_norm(x, gain):
    return x * lax.rsqrt(jnp.mean(x * x, axis=-1, keepdims=True) + EPS) * gain


def _alibi_slope(h_index, n_heads):
    hv = jnp.full((1, 1), h_index + 1, jnp.int32).astype(F32)
    return jnp.exp2(hv * (-8.0 / n_heads))


def _moba_prompt_kernel(q_ref, k_ref, v_ref, g_ref, qg_ref, kg_ref, o_ref, kn_ref,
                        ka_scr, vb_scr, km_scr, *, n_blocks):
    blk = MOBA_BLOCK
    h = pl.program_id(1)
    c = pl.program_id(2)
    hd = q_ref.shape[1]
    t_all = k_ref.shape[0]

    @pl.when(c == 0)
    def _():
        kn = _head_norm(k_ref[...], kg_ref[...])
        kn_ref[...] = kn
        row_blk = lax.broadcasted_iota(jnp.int32, (t_all, hd), 0) // blk
        lane_k = lax.broadcasted_iota(jnp.int32, (t_all, hd), 1)
        one_hot = jnp.where(lane_k == row_blk, 1.0, 0.0).astype(BF16)
        ka_scr[...] = jnp.concatenate([kn.astype(BF16), one_hot], axis=1)
        vb_scr[...] = v_ref[...].astype(BF16)
        means = [jnp.mean(kn[n * blk:(n + 1) * blk, :], axis=0, keepdims=True) for n in range(n_blocks)]
        km_scr[...] = _pad_rows(jnp.concatenate(means, axis=0), LANES)

    qn = _head_norm(q_ref[...], qg_ref[...])
    qs = (qn * (hd ** -0.5 * LOG2E)).astype(BF16)
    slope2 = _alibi_slope(h, MOBA_HEADS) * LOG2E
    key_off = lax.broadcasted_iota(jnp.int32, (1, blk), 1).astype(F32)
    ti = lax.broadcasted_iota(jnp.int32, (blk, blk), 0)
    sj = lax.broadcasted_iota(jnp.int32, (blk, blk), 1)

    def tile(k):
        gated = k > MOBA_TOPK
        if gated:
            gate = _dot_nt(qn, km_scr[...], precision=lax.Precision.HIGHEST)
            lane = lax.broadcasted_iota(jnp.int32, gate.shape, 1)
            gm = jnp.where(lane < k, gate, NEG_INF)
            rank = jnp.zeros(gate.shape, jnp.int32)
            for sh in range(1, k):
                other = pltpu.roll(gm, sh, axis=1)
                ahead = (other > gm) | ((other == gm) & (lane >= sh))
                rank = rank + ahead.astype(jnp.int32)
            drop = jnp.where((lane < k) & (rank >= MOBA_TOPK), MASK_NEG, 0.0).astype(BF16)
            q_aug = jnp.concatenate([qs, drop], axis=1)
        scores = []
        for n in range(k):
            keys = ka_scr[n * blk:(n + 1) * blk, :] if gated else ka_scr[n * blk:(n + 1) * blk, :hd]
            bias = slope2 * (key_off + float((n - k) * blk))
            scores.append(_dot_nt(q_aug if gated else qs, keys) + bias)
        s_own = _dot_nt(qs, ka_scr[k * blk:(k + 1) * blk, :hd]) + slope2 * key_off
        scores.append(jnp.where(ti >= sj, s_own, NEG_INF))

        m_el = scores[0]
        for s in scores[1:]:
            m_el = jnp.maximum(m_el, s)
        m = jnp.broadcast_to(jnp.max(m_el, axis=-1, keepdims=True), (blk, blk))
        l_el = jnp.zeros((blk, blk), F32)
        acc = jnp.zeros((blk, hd), F32)
        for n, s in enumerate(scores):
            p = jnp.exp2(s - m)
            l_el = l_el + p
            acc = acc + _dot(p.astype(BF16), vb_scr[n * blk:(n + 1) * blk, :])
        out = acc / jnp.sum(l_el, axis=-1, keepdims=True)
        o_ref[...] = (out * _silu(g_ref[...].astype(F32))).astype(o_ref.dtype)

    for k in range(n_blocks):
        pl.when(c == k)(functools.partial(tile, k))


def _moba_prompt(p, q_gain, k_gain, batch, t_rows, out_dtype):
    m, n = p.shape
    h = MOBA_HEADS
    hd = MOBA_HD
    assert n == 4 * h * hd and t_rows % MOBA_BLOCK == 0
    nb = t_rows // MOBA_BLOCK
    blk = MOBA_BLOCK
    return pl.pallas_call(
        functools.partial(_moba_prompt_kernel, n_blocks=nb),
        out_shape=(jax.ShapeDtypeStruct((m, h * hd), out_dtype),
                   jax.ShapeDtypeStruct((m, h * hd), F32)),
        grid=(batch, h, nb),
        in_specs=[
            pl.BlockSpec((blk, hd), lambda b, hh, c: (b * nb + c, hh)),
            pl.BlockSpec((t_rows, hd), lambda b, hh, c: (b, h + hh)),
            pl.BlockSpec((t_rows, hd), lambda b, hh, c: (b, 2 * h + hh)),
            pl.BlockSpec((blk, hd), lambda b, hh, c: (b * nb + c, 3 * h + hh)),
            pl.BlockSpec((1, hd), lambda b, hh, c: (0, 0)),
            pl.BlockSpec((1, hd), lambda b, hh, c: (0, 0)),
        ],
        out_specs=(pl.BlockSpec((blk, hd), lambda b, hh, c: (b * nb + c, hh)),
                   pl.BlockSpec((t_rows, hd), lambda b, hh, c: (b, hh))),
        scratch_shapes=[
            pltpu.VMEM((t_rows, 2 * hd), BF16),
            pltpu.VMEM((t_rows, hd), BF16),
            pltpu.VMEM((LANES, hd), F32),
        ],
        compiler_params=_cparams(("parallel", "parallel", "arbitrary")),
        name="moba_prompt",
    )(p, p, p, p, q_gain.reshape(1, hd), k_gain.reshape(1, hd))


KMEAN_PAGES_PER_STEP = 8


def _kmean_kernel(pt_ref, *refs):
    del pt_ref
    page_refs, o_ref = refs[:-1], refs[-1]
    n = pl.program_id(1)
    blocks_per_step = len(page_refs) // 2
    for j in range(blocks_per_step):
        tot = jnp.sum(page_refs[2 * j][...], axis=0) + jnp.sum(page_refs[2 * j + 1][...], axis=0)
        o_ref[n * blocks_per_step + j] = tot * (1.0 / MOBA_BLOCK)


def _moba_kmean(cache, page_table):
    _, page, heads, hd = cache.shape
    db, n_pages = page_table.shape
    pps = math.gcd(n_pages, KMEAN_PAGES_PER_STEP)
    assert MOBA_BLOCK == 2 * page and pps % 2 == 0
    nb = n_pages // 2

    def page_spec(j):
        return pl.BlockSpec((None, page, heads, hd), lambda b, n, pt: (pt[b, pps * n + j], 0, 0, 0))

    return pl.pallas_call(
        _kmean_kernel,
        out_shape=jax.ShapeDtypeStruct((db, nb, heads, hd), F32),
        grid_spec=pltpu.PrefetchScalarGridSpec(
            num_scalar_prefetch=1,
            grid=(db, n_pages // pps),
            in_specs=[page_spec(j) for j in range(pps)],
            out_specs=pl.BlockSpec((None, nb, heads, hd), lambda b, n, pt: (b, 0, 0, 0)),
        ),
        compiler_params=_cparams(("parallel", "arbitrary")),
        name="moba_past_block_means",
    )(page_table, *([cache] * pps))


def _moba_select_kernel(q_ref, k_ref, km_ref, qg_ref, kg_ref, qn_ref, kn_ref, idx_ref, *, n_blocks):
    hd = MOBA_HD
    t = q_ref.shape[0]
    lane = lax.broadcasted_iota(jnp.int32, (t, LANES), 1)
    for h in range(MOBA_HEADS):
        cols = slice(h * hd, (h + 1) * hd)
        qn = _head_norm(q_ref[:, cols], qg_ref[...])
        kn = _head_norm(k_ref[:, cols], kg_ref[...])
        qn_ref[:, cols] = qn
        kn_ref[:, cols] = kn
        km = _pad_rows(km_ref[:, h, :], LANES)
        gate = _dot_nt(qn, km, precision=lax.Precision.HIGHEST)
        gm = jnp.where(lane < n_blocks, gate, NEG_INF)
        picks = jnp.zeros((t, LANES), jnp.int32)
        for j in range(MOBA_TOPK):
            best = jnp.max(gm, axis=-1, keepdims=True)
            ix = jnp.min(jnp.where(gm == best, lane, LANES), axis=-1, keepdims=True)
            picks = jnp.where(lane == j, ix, picks)
            gm = jnp.where(lane == ix, NEG_INF, gm)
        idx_ref[h] = picks


def _moba_select(p, kmean, q_gain, k_gain, batch, t_rows):
    m, n = p.shape
    w = MOBA_HEADS * MOBA_HD
    nb = kmean.shape[1]
    assert nb >= MOBA_TOPK and nb <= LANES
    return pl.pallas_call(
        functools.partial(_moba_select_kernel, n_blocks=nb),
        out_shape=(jax.ShapeDtypeStruct((m, w), F32),
                   jax.ShapeDtypeStruct((m, w), F32),
                   jax.ShapeDtypeStruct((batch, MOBA_HEADS, t_rows, LANES), jnp.int32)),
        grid=(batch,),
        in_specs=[
            pl.BlockSpec((t_rows, w), lambda b: (b, 0)),
            pl.BlockSpec((t_rows, w), lambda b: (b, 1)),
            pl.BlockSpec((None, nb, MOBA_HEADS, MOBA_HD), lambda b: (b, 0, 0, 0)),
            pl.BlockSpec((1, MOBA_HD), lambda b: (0, 0)),
            pl.BlockSpec((1, MOBA_HD), lambda b: (0, 0)),
        ],
        out_specs=(pl.BlockSpec((t_rows, w), lambda b: (b, 0)),
                   pl.BlockSpec((t_rows, w), lambda b: (b, 0)),
                   pl.BlockSpec((None, MOBA_HEADS, t_rows, LANES), lambda b: (b, 0, 0, 0))),
        compiler_params=_cparams(("parallel",)),
        name="moba_sample_select",
    )(p, p, kmean, q_gain.reshape(1, MOBA_HD), k_gain.reshape(1, MOBA_HD))


def _moba_sample_attn_kernel(idx_ref, pt_ref, ck_ref, cv_ref, qn_ref, kn_ref, vn_ref, g_ref, o_ref,
                             kbuf, vbuf, sem, *, past_len, t_rows):
    step = pl.program_id(0)
    n_steps = pl.num_programs(0)
    h = step % MOBA_HEADS
    hd = MOBA_HD
    blk = MOBA_BLOCK
    page = blk // 2
    n_sel = t_rows * MOBA_TOPK
    scale = hd ** -0.5
    slope = _alibi_slope(h, MOBA_HEADS)

    def for_each_copy(s, slot, fn):
        sb = s // MOBA_HEADS
        sh = s % MOBA_HEADS

        def body(u, carry):
            blk_id = idx_ref[s * n_sel + u]
            for half in range(2):
                pg = pt_ref[sb, 2 * blk_id + half]
                rows = pl.ds(half * page, page)
                fn(pltpu.make_async_copy(ck_ref.at[pg, :, sh, :], kbuf.at[slot, u, rows, :], sem.at[slot]))
                fn(pltpu.make_async_copy(cv_ref.at[pg, :, sh, :], vbuf.at[slot, u, rows, :], sem.at[slot]))
            return carry

        lax.fori_loop(0, n_sel, body, 0)

    slot = step % 2

    @pl.when(step == 0)
    def _():
        for_each_copy(step, slot, lambda cp: cp.start())

    @pl.when(step + 1 < n_steps)
    def _():
        for_each_copy(step + 1, 1 - slot, lambda cp: cp.start())

    for_each_copy(step, slot, lambda cp: cp.wait())

    offs = lax.broadcasted_iota(jnp.int32, (1, blk), 1)
    key_i = lax.broadcasted_iota(jnp.int32, (t_rows, 1), 0)
    kn = kn_ref[...]
    vn = vn_ref[...]

    for qi in range(t_rows):
        qrow = qn_ref[qi:qi + 1, :]
        q8 = jnp.broadcast_to(qrow, (8, hd)).astype(BF16)
        t_pos = past_len + qi
        s_own = jnp.sum(kn * qrow, axis=-1, keepdims=True) * scale - slope * (qi - key_i).astype(F32)
        s_own = jnp.where(key_i <= qi, s_own, NEG_INF)
        m = jnp.max(s_own, axis=0, keepdims=True)
        scores = []
        for j in range(MOBA_TOPK):
            u = qi * MOBA_TOPK + j
            dist = (t_pos - idx_ref[step * n_sel + u] * blk - offs).astype(F32)
            s = _dot_nt(q8, kbuf[slot, u].astype(BF16))[0:1, :] * scale - slope * dist
            scores.append(s)
            m = jnp.maximum(m, jnp.max(s, axis=-1, keepdims=True))
        p_own = jnp.exp(s_own - m)
        denom = jnp.sum(p_own, axis=0, keepdims=True)
        out = jnp.sum(p_own * vn, axis=0, keepdims=True)
        for j, s in enumerate(scores):
            u = qi * MOBA_TOPK + j
            pj = jnp.exp(s - m)
            denom = denom + jnp.sum(pj, axis=-1, keepdims=True)
            out = out + _dot(jnp.broadcast_to(pj, (8, blk)).astype(BF16), vbuf[slot, u].astype(BF16))[0:1, :]
        gate = g_ref[qi:qi + 1, :]
        o_ref[qi:qi + 1, :] = out / denom * _silu(gate)


def _moba_sample_attn(idx, page_table, cache_k, cache_v, qn, kn, p, batch, t_rows):
    h = MOBA_HEADS
    hd = MOBA_HD
    _, page, heads, _ = cache_k.shape
    n_pages = page_table.shape[1]
    assert MOBA_BLOCK == 2 * page and heads == h
    n_sel = t_rows * MOBA_TOPK
    row_spec = lambda off: pl.BlockSpec((t_rows, hd), lambda s, ix, pt: (s // h, off * h + s % h))
    return pl.pallas_call(
        functools.partial(_moba_sample_attn_kernel, past_len=n_pages * page, t_rows=t_rows),
        out_shape=jax.ShapeDtypeStruct((batch * t_rows, h * hd), F32),
        grid_spec=pltpu.PrefetchScalarGridSpec(
            num_scalar_prefetch=2,
            grid=(batch * h,),
            in_specs=[pl.BlockSpec(memory_space=pl.ANY), pl.BlockSpec(memory_space=pl.ANY),
                      row_spec(0), row_spec(0), row_spec(2), row_spec(3)],
            out_specs=row_spec(0),
            scratch_shapes=[pltpu.VMEM((2, n_sel, MOBA_BLOCK, hd), F32),
                            pltpu.VMEM((2, n_sel, MOBA_BLOCK, hd), F32),
                            pltpu.SemaphoreType.DMA((2,))],
        ),
        compiler_params=_cparams(("arbitrary",)),
        name="moba_sample_attention",
    )(idx, page_table, cache_k, cache_v, qn, kn, p, p)


def _retention_layer(h2d, gain, w_in, w_out, s0, batch, t_rows):
    big = t_rows >= RET_CHUNK
    p = _proj(h2d, gain, w_in, 0, w_in.shape[1], BF16 if big else F32)
    o, s = _retention(p, s0, batch, t_rows, BF16 if big else F32)
    return _oproj(o, w_out, h2d), s


def _hgrn_layer(h2d, gain, w_in, lb_logits, o_gain, w_out, s0, layer, batch, t_rows):
    p = _proj(h2d, gain, w_in, 0, w_in.shape[1], F32)
    o, s = _hgrn(p, lb_logits, o_gain, s0, layer, batch, t_rows, BF16 if t_rows >= HG_CHUNK else F32)
    return _oproj(o, w_out, h2d), s


def _moba_prompt_layer(h2d, gain, w_in, q_gain, k_gain, w_out, batch, t_rows):
    w = MOBA_HEADS * MOBA_HD
    p = _proj(h2d, gain, w_in, 0, w_in.shape[1], F32)
    o, kn = _moba_prompt(p, q_gain, k_gain, batch, t_rows, BF16)
    return _oproj(o, w_out, h2d), kn, p[:, 2 * w:3 * w]


def _moba_sample_layer(h2d, gain, w_in, q_gain, k_gain, w_out, cache_k, cache_v, page_table, batch, t_rows):
    w = MOBA_HEADS * MOBA_HD
    p = _proj(h2d, gain, w_in, 0, w_in.shape[1], F32)
    kmean = _moba_kmean(cache_k, page_table)
    qn, kn, picks = _moba_select(p, kmean, q_gain, k_gain, batch, t_rows)
    idx = picks[..., :MOBA_TOPK].reshape(-1)
    o = _moba_sample_attn(idx, page_table, cache_k, cache_v, qn, kn, p, batch, t_rows)
    return _oproj(o, w_out, h2d), kn, p[:, 2 * w:3 * w]


def kernel(x_prompt, x_sample, state_ret, cache_k, cache_v, state_hgrn, page_table, norm_gain, ret_w_in, ret_w_out, moba_w_in, moba_q_gain, moba_k_gain, moba_w_out, hgrn_w_in, hgrn_lb_logits, hgrn_o_gain, hgrn_w_out):
    bp, tp, d = x_prompt.shape
    bs, ts, _ = x_sample.shape
    depth = norm_gain.shape[0]
    n_pool = cache_k.shape[1]
    hp = x_prompt.reshape(bp * tp, d)
    hs = x_sample.reshape(bs * ts, d)
    kp_l, vp_l, ks_l, vs_l, rp_l, rs_l, gp_l, gs_l = [], [], [], [], [], [], [], []
    for layer in range(depth):
        kind, slot = layer % N_MIXERS, layer // N_MIXERS
        gain = norm_gain[layer]
        if kind == 0:
            w_in = ret_w_in[slot].astype(BF16)
            w_out = ret_w_out[slot].astype(BF16)
            hp, st_p = _retention_layer(hp, gain, w_in, w_out, None, bp, tp)
            hs, st_s = _retention_layer(hs, gain, w_in, w_out, state_ret[slot], bs, ts)
            rp_l.append(st_p)
            rs_l.append(st_s)
        elif kind == 1:
            w_in = moba_w_in[slot].astype(BF16)
            w_out = moba_w_out[slot].astype(BF16)
            ck = cache_k.reshape((-1,) + cache_k.shape[2:])
            cv = cache_v.reshape((-1,) + cache_v.shape[2:])
            hp, kp, vp = _moba_prompt_layer(hp, gain, w_in, moba_q_gain[slot], moba_k_gain[slot], w_out, bp, tp)
            hs, ks_, vs_ = _moba_sample_layer(hs, gain, w_in, moba_q_gain[slot], moba_k_gain[slot], w_out,
                                              ck, cv, page_table + slot * n_pool, bs, ts)
            shp = (MOBA_HEADS, MOBA_HD)
            kp_l.append(kp.reshape(bp, tp, *shp))
            vp_l.append(vp.reshape(bp, tp, *shp))
            ks_l.append(ks_.reshape(bs, ts, *shp))
            vs_l.append(vs_.reshape(bs, ts, *shp))
        else:
            w_in = hgrn_w_in[slot].astype(BF16)
            w_out = hgrn_w_out[slot].astype(BF16)
            hp, st_p = _hgrn_layer(hp, gain, w_in, hgrn_lb_logits, hgrn_o_gain[slot], w_out, None, layer, bp, tp)
            hs, st_s = _hgrn_layer(hs, gain, w_in, hgrn_lb_logits, hgrn_o_gain[slot], w_out,
                                   state_hgrn[slot], layer, bs, ts)
            gp_l.append(st_p)
            gs_l.append(st_s)
    return (hp.reshape(bp, tp, d), hs.reshape(bs, ts, d),
            jnp.stack(kp_l), jnp.stack(vp_l), jnp.stack(ks_l), jnp.stack(vs_l),
            jnp.stack(rp_l), jnp.stack(rs_l), jnp.stack(gp_l), jnp.stack(gs_l))
```

```python
import functools
import math

import numpy as np
import jax
import jax.numpy as jnp
from jax import lax
from jax.experimental import pallas as pl
from jax.experimental.pallas import tpu as pltpu

F32 = jnp.float32
BF16 = jnp.bfloat16
EPS = 1e-6
NEG_INF = float("-inf")
LOG2E = 1.4426950408889634
MASK_NEG = -1e30

N_MIXERS = 3
RET_HEADS = 8
RET_CHUNK = 128
MOBA_HEADS = 16
MOBA_HD = 128
MOBA_BLOCK = 256
MOBA_TOPK = 3
HG_HEADS = 16
HG_DK = 128
HG_CHUNK = 128
HG_SUB = 16
HG_FACTORED_SUB = 64
HG_FACTORED_MAX_LOG2_SPAN = 100.0

VMEM_LIMIT_BYTES = 52 * 1024 * 1024
LANES = 128


def _cparams(sem):
    return pltpu.CompilerParams(dimension_semantics=sem, vmem_limit_bytes=VMEM_LIMIT_BYTES)


def _dot(a, b):
    return jnp.dot(a, b, preferred_element_type=F32)


def _dot_nt(a, b):
    return lax.dot_general(a, b, (((1,), (1,)), ((), ())), preferred_element_type=F32)


def _dot_tn(a, b):
    return lax.dot_general(a, b, (((0,), (0,)), ((), ())), preferred_element_type=F32)


def _split_bf16(x, parts):
    pieces = []
    for _ in range(parts):
        hi = x.astype(BF16)
        pieces.append(hi)
        x = x - hi.astype(F32)
    return pieces


def _dot_nt_3pass(a, b):
    a_hi, a_lo = _split_bf16(a, 2)
    b_hi, b_lo = _split_bf16(b, 2)
    return _dot_nt(a_hi, b_hi) + (_dot_nt(a_hi, b_lo) + _dot_nt(a_lo, b_hi))


def _silu(x):
    return x * jax.nn.sigmoid(x)


def _pad_rows(x, rows):
    if x.shape[0] == rows:
        return x
    return jnp.concatenate([x, jnp.zeros((rows - x.shape[0], x.shape[1]), x.dtype)], axis=0)


IN_TM, IN_TN = 2048, 256
OUT_TM, OUT_TN = 1024, 512


def _in_proj_kernel(xp_ref, xs_ref, gain_ref, w_ref, *refs, n_i, bounds):
    n_g = len(bounds)
    outs_p, outs_s, xn_ref = refs[:n_g], refs[n_g:2 * n_g], refs[2 * n_g]
    i = pl.program_id(0)
    j = pl.program_id(1)
    ms_rows = xs_ref.shape[0]

    def normed(x):
        return (x * lax.rsqrt(jnp.mean(x * x, axis=-1, keepdims=True) + EPS) * gain_ref[...]).astype(BF16)

    @pl.when((j == 0) & (i < n_i))
    def _():
        xn_ref[...] = normed(xp_ref[...])

    @pl.when((j == 0) & (i == n_i))
    def _():
        xn_ref[:ms_rows, :] = normed(xs_ref[...])

    wb = w_ref[...].astype(BF16)
    for g, (j0, j1) in enumerate(bounds):
        in_group = (j >= j0) & (j < j1)

        @pl.when(in_group & (i < n_i))
        def _(g=g):
            outs_p[g][...] = _dot(xn_ref[...], wb).astype(outs_p[g].dtype)

        @pl.when(in_group & (i == n_i))
        def _(g=g):
            outs_s[g][...] = _dot(xn_ref[:ms_rows, :], wb)


def _in_proj(xp, xs, gain, w, slot, groups):
    mp, d = xp.shape
    ms = xs.shape[0]
    n = w.shape[2]
    tm, tn = min(IN_TM, mp), IN_TN
    assert mp % tm == 0 and ms <= tm and sum(c for c, _ in groups) == n and all(c % tn == 0 for c, _ in groups)
    n_i, n_j = mp // tm, n // tn
    bounds, j0 = [], 0
    for c, _ in groups:
        bounds.append((j0, j0 + c // tn))
        j0 += c // tn

    def prompt_map(j0, j1):
        def imap(i, j):
            jj = jnp.where(i < n_i, j, n_j - 1)
            return (jnp.minimum(i, n_i - 1), jnp.clip(jj - j0, 0, j1 - j0 - 1))
        return imap

    def sample_map(j0, j1):
        def imap(i, j):
            jj = jnp.where(i < n_i, 0, j)
            return (0, jnp.clip(jj - j0, 0, j1 - j0 - 1))
        return imap

    out_shape = ([jax.ShapeDtypeStruct((mp, c), dt) for c, dt in groups]
                 + [jax.ShapeDtypeStruct((ms, c), F32) for c, _ in groups])
    out_specs = ([pl.BlockSpec((tm, tn), prompt_map(*b)) for b in bounds]
                 + [pl.BlockSpec((ms, tn), sample_map(*b)) for b in bounds])
    outs = pl.pallas_call(
        functools.partial(_in_proj_kernel, n_i=n_i, bounds=tuple(bounds)),
        out_shape=out_shape,
        grid=(n_i + 1, n_j),
        in_specs=[
            pl.BlockSpec((tm, d), lambda i, j: (jnp.minimum(i, n_i - 1), 0), pipeline_mode=pl.Buffered(1)),
            pl.BlockSpec((ms, d), lambda i, j: (0, 0)),
            pl.BlockSpec((1, d), lambda i, j: (0, 0)),
            pl.BlockSpec((None, d, tn), lambda i, j: (slot, 0, j)),
        ],
        out_specs=out_specs,
        scratch_shapes=[pltpu.VMEM((tm, d), BF16)],
        compiler_params=_cparams(("arbitrary", "arbitrary")),
        name="norm_in_proj",
    )(xp, xs, gain.reshape(1, d), w)
    return outs[:len(groups)], outs[len(groups):]


def _out_proj_kernel(ap_ref, as_ref, w_ref, rp_ref, rs_ref, op_ref, os_ref, wb_ref, *, n_i):
    i = pl.program_id(1)

    @pl.when(i == 0)
    def _():
        wb_ref[...] = w_ref[...].astype(BF16)

    @pl.when(i < n_i)
    def _():
        op_ref[...] = _dot(ap_ref[...], wb_ref[...]) + rp_ref[...]

    @pl.when(i == n_i)
    def _():
        os_ref[...] = _dot(as_ref[...].astype(BF16), wb_ref[...]) + rs_ref[...]


def _out_proj(ap, a_s, w, slot, rp, rs):
    mp, k = ap.shape
    ms = a_s.shape[0]
    n = w.shape[2]
    tm, tn = min(OUT_TM, mp), OUT_TN
    assert mp % tm == 0 and n % tn == 0
    n_i = mp // tm
    row = lambda i: jnp.minimum(i, n_i - 1)
    return pl.pallas_call(
        functools.partial(_out_proj_kernel, n_i=n_i),
        out_shape=(jax.ShapeDtypeStruct((mp, n), F32), jax.ShapeDtypeStruct((ms, n), F32)),
        grid=(n // tn, n_i + 1),
        in_specs=[
            pl.BlockSpec((tm, k), lambda j, i: (row(i), 0)),
            pl.BlockSpec((ms, k), lambda j, i: (0, 0)),
            pl.BlockSpec((None, k, tn), lambda j, i: (slot, 0, j)),
            pl.BlockSpec((tm, tn), lambda j, i: (row(i), j)),
            pl.BlockSpec((ms, tn), lambda j, i: (0, j)),
        ],
        out_specs=(pl.BlockSpec((tm, tn), lambda j, i: (row(i), j)),
                   pl.BlockSpec((ms, tn), lambda j, i: (0, j))),
        scratch_shapes=[pltpu.VMEM((k, tn), BF16)],
        compiler_params=_cparams(("arbitrary", "arbitrary")),
        name="out_proj_residual",
    )(ap, a_s, w, rp, rs)


def _ret_tables(c, dk):
    scale = dk ** -0.5
    log_g = np.log1p(-np.exp2(-5.0 - np.arange(RET_HEADS, dtype=np.float64)))
    i = np.arange(c, dtype=np.float64)
    rel = i[:, None] - i[None, :]
    intra = np.where(rel >= 0, np.exp(log_g[:, None, None] * np.maximum(rel, 0.0)), 0.0) * scale
    q_dec = np.exp(log_g[:, None] * (i + 1.0))
    k_dec = np.exp(log_g[:, None] * (c - 1.0 - i)) * scale
    c_dec = np.exp(log_g * c)
    big = RET_CHUNK
    intra_p = np.zeros((RET_HEADS, big, big), np.float32)
    intra_p[:, :c, :c] = intra
    qd = np.zeros((RET_HEADS, big, 1), np.float32)
    qd[:, :c, 0] = q_dec
    kd = np.zeros((RET_HEADS, big, 1), np.float32)
    kd[:, :c, 0] = k_dec
    cd = c_dec.astype(np.float32).reshape(RET_HEADS, 1, 1)
    return jnp.asarray(intra_p), jnp.asarray(qd), jnp.asarray(kd), jnp.asarray(cd)


def _ret_kernel(*refs, t_rows, has_s0):
    if has_s0:
        q_ref, k_ref, v_ref, g_ref, intra_ref, qd_ref, kd_ref, cd_ref, s0_ref, o_ref, s_ref, st = refs
        st[...] = s0_ref[...]
    else:
        q_ref, k_ref, v_ref, g_ref, intra_ref, qd_ref, kd_ref, cd_ref, o_ref, s_ref, st = refs
        st[...] = jnp.zeros_like(st)
    c = RET_CHUNK
    t_pad = max(t_rows, c)
    intra = intra_ref[...]
    qd = qd_ref[...]
    kd = kd_ref[...]
    cd = cd_ref[...]

    def chunk(ci, carry):
        if t_rows >= c:
            rows = pl.ds(pl.multiple_of(ci * c, c), c)
        else:
            rows = pl.ds(0, t_rows)
        qc = _pad_rows(q_ref[rows, :], c).astype(BF16)
        kf = _pad_rows(k_ref[rows, :], c).astype(F32)
        vc = _pad_rows(v_ref[rows, :], c).astype(BF16)
        s = st[...]
        att = _dot_nt(qc, kf.astype(BF16)) * intra
        o = _dot(att.astype(BF16), vc) + qd * _dot(qc, s.astype(BF16))
        st[...] = s * cd + _dot_tn((kf * kd).astype(BF16), vc)
        o = o * lax.rsqrt(jnp.mean(o * o, axis=-1, keepdims=True) + EPS)
        gate = g_ref[rows, :].astype(F32)
        o_ref[rows, :] = (o[: gate.shape[0]] * _silu(gate)).astype(o_ref.dtype)
        return carry

    n_chunks = t_pad // c
    lax.fori_loop(0, n_chunks, chunk, 0, unroll=2 if n_chunks % 2 == 0 else 1)
    s_ref[...] = st[...]


def _retention(p, s0, slot, batch, t_rows, out_dtype):
    m, n = p.shape
    h = RET_HEADS
    dk = n // (6 * h)
    dv = 2 * dk
    c_eff = math.gcd(t_rows, RET_CHUNK)
    assert t_rows % c_eff == 0 and (t_rows >= RET_CHUNK or t_rows == c_eff)
    intra, qd, kd, cd = _ret_tables(c_eff, dk)
    has_s0 = s0 is not None
    in_specs = [
        pl.BlockSpec((t_rows, dk), lambda b, hh: (b, hh)),
        pl.BlockSpec((t_rows, dk), lambda b, hh: (b, h + hh)),
        pl.BlockSpec((t_rows, dv), lambda b, hh: (b, h + hh)),
        pl.BlockSpec((t_rows, dv), lambda b, hh: (b, 2 * h + hh)),
        pl.BlockSpec((None, RET_CHUNK, RET_CHUNK), lambda b, hh: (hh, 0, 0)),
        pl.BlockSpec((None, RET_CHUNK, 1), lambda b, hh: (hh, 0, 0)),
        pl.BlockSpec((None, RET_CHUNK, 1), lambda b, hh: (hh, 0, 0)),
        pl.BlockSpec((None, 1, 1), lambda b, hh: (hh, 0, 0)),
    ]
    args = [p, p, p, p, intra, qd, kd, cd]
    if has_s0:
        in_specs.append(pl.BlockSpec((None, None, None, dk, dv), lambda b, hh: (slot, b, hh, 0, 0)))
        args.append(s0)
    return pl.pallas_call(
        functools.partial(_ret_kernel, t_rows=t_rows, has_s0=has_s0),
        out_shape=(jax.ShapeDtypeStruct((m, h * dv), out_dtype),
                   jax.ShapeDtypeStruct((batch, h, dk, dv), F32)),
        grid=(batch, h),
        in_specs=in_specs,
        out_specs=(pl.BlockSpec((t_rows, dv), lambda b, hh: (b, hh)),
                   pl.BlockSpec((None, None, dk, dv), lambda b, hh: (b, hh, 0, 0))),
        scratch_shapes=[pltpu.VMEM((dk, dv), F32)],
        compiler_params=_cparams(("parallel", "parallel")),
        name="retention",
    )(*args)


def _hgrn_kernel(*refs, layer, t_rows, chunk, has_s0):
    n_in = 7 if has_s0 else 6
    q_ref, f_ref, i_ref, g_ref, lbl_ref, og_ref = refs[:6]
    o_ref, s_ref = refs[n_in:n_in + 2]
    scratch_sets = (refs[n_in + 2:n_in + 5], refs[n_in + 5:n_in + 8])
    c = chunk
    r = HG_SUB
    n_sub = c // r
    n_chunks = max(t_rows, c) // c
    dk = q_ref.shape[1]

    logits = lbl_ref[...]
    e = jnp.exp(logits - jnp.max(logits, axis=0, keepdims=True))
    prob = e / jnp.sum(e, axis=0, keepdims=True)
    lrow = lax.broadcasted_iota(jnp.int32, prob.shape, 0)
    in_lb = (lrow >= 1) & (lrow <= layer)
    lb = jnp.sum(jnp.where(in_lb, prob, 0.0), axis=0, keepdims=True)
    one_m_lb = jnp.sum(jnp.where(in_lb, 0.0, prob), axis=0, keepdims=True)

    ri = lax.broadcasted_iota(jnp.int32, (c, LANES), 0)
    cj = lax.broadcasted_iota(jnp.int32, (c, LANES), 1)
    tri_mask = ri >= cj
    tri = jnp.where(tri_mask, 1.0, 0.0).astype(BF16)
    row_c = lax.broadcasted_iota(jnp.int32, (c, dk), 0)
    row_r = lax.broadcasted_iota(jnp.int32, (r, dk), 0)
    og = og_ref[...]

    def one_chunk(ci, s_t, b_scr, k_scr, v_scr, factored):
        if t_rows >= c:
            rows = pl.ds(pl.multiple_of(ci * c, c), c)
        else:
            rows = pl.ds(0, t_rows)
        z = _pad_rows(f_ref[rows, :].astype(F32), c)
        qraw = _pad_rows(q_ref[rows, :].astype(F32), c)
        v = _pad_rows(i_ref[rows, :].astype(F32), c)
        f = lb + one_m_lb * jax.nn.sigmoid(z)
        logf = jnp.log(f)
        kk = one_m_lb * jax.nn.sigmoid(-z)
        if t_rows < c:
            valid = row_c < t_rows
            logf = jnp.where(valid, logf, 0.0)
            kk = jnp.where(valid, kk, 0.0)
        lf_hi, lf_mid, lf_lo = _split_bf16(_pad_rows(logf, LANES), 3)
        b = (_dot(tri, lf_hi) + (_dot(tri, lf_mid) + _dot(tri, lf_lo))) * LOG2E
        qs = _silu(qraw)
        vb = v.astype(BF16)
        o = _dot_nt((qs * jnp.exp2(b)).astype(BF16), s_t.astype(BF16))

        def sub_block_rows(rq, with_diagonal):
            a_rows = []
            for lo in range(0, c, rq):
                hi = lo + rq if with_diagonal else lo
                if hi == 0:
                    a_rows.append(jnp.zeros((rq, c), F32))
                    continue
                b0 = b[lo - 1:lo, :] if lo else jnp.zeros((1, dk), F32)
                qh = qs[lo:lo + rq, :] * jnp.exp2(b[lo:lo + rq, :] - b0)
                kh = kk[:hi, :] * jnp.exp2(b0 - b[:hi, :])
                a_rows.append(_dot_nt(qh.astype(BF16), _pad_rows(kh, c).astype(BF16)))
            return jnp.concatenate(a_rows, axis=0)

        def intra_factored():
            att = jnp.where(tri_mask, sub_block_rows(HG_FACTORED_SUB, True), 0.0)
            return _dot(att.astype(BF16), vb)

        def intra_pairwise():
            b_scr[...] = b
            k_scr[...] = kk
            v_scr[...] = v
            o_diag = []
            for sb in range(n_sub):
                lo = sb * r
                bi = b[lo:lo + r, :]
                qi = qs[lo:lo + r, :]
                acc = jnp.zeros((r, dk), F32)
                for s in range(min(r, t_rows)):
                    bs = b_scr[lo + s:lo + s + 1, :]
                    ks = k_scr[lo + s:lo + s + 1, :]
                    vs = v_scr[lo + s:lo + s + 1, :]
                    d = bi - bs
                    if s > 0:
                        d = jnp.where(row_r >= s, d, NEG_INF)
                    col = jnp.sum(qi * jnp.exp2(d) * ks, axis=-1, keepdims=True)
                    acc = acc + col * vs
                o_diag.append(acc)
            out = jnp.concatenate(o_diag, axis=0)
            if n_sub > 1:
                out = out + _dot(sub_block_rows(r, False).astype(BF16), vb)
            return out

        o = o + (intra_factored() if factored else intra_pairwise())

        b_end = b[c - 1:c, :]
        kd = kk * jnp.exp2(b_end - b)
        s_new = s_t * jnp.exp2(b_end) + _dot_tn(_pad_rows(vb, LANES), _pad_rows(kd.astype(BF16), LANES))

        o = o * lax.rsqrt(jnp.mean(o * o, axis=-1, keepdims=True) + EPS) * og
        gate = g_ref[rows, :].astype(F32)
        o_ref[rows, :] = (o[: gate.shape[0]] * _silu(gate)).astype(o_ref.dtype)
        return s_new

    s_init = refs[6][...].T if has_s0 else jnp.zeros((dk, dk), F32)
    def run(factored):
        if n_chunks == 1:
            return one_chunk(0, s_init, *scratch_sets[0], factored)

        def pair(i, s_t):
            s_t = one_chunk(2 * i, s_t, *scratch_sets[0], factored)
            return one_chunk(2 * i + 1, s_t, *scratch_sets[1], factored)

        return lax.fori_loop(0, n_chunks // 2, pair, s_init)

    if n_sub > 1:
        factored_ok = jnp.min(lb) >= 2.0 ** (-HG_FACTORED_MAX_LOG2_SPAN / HG_FACTORED_SUB)
        s_fin = lax.cond(factored_ok, functools.partial(run, True), functools.partial(run, False))
    else:
        s_fin = run(False)
    s_ref[...] = s_fin.T


def _hgrn(pq, pf, pig, lb_logits, o_gain, s0, slot, layer, batch, t_rows, out_dtype):
    m, n = pq.shape
    h = HG_HEADS
    dk = HG_DK
    dv = n // h
    depth = lb_logits.shape[0]
    chunk = HG_CHUNK if t_rows >= HG_CHUNK else HG_SUB
    n_chunks = max(t_rows, chunk) // chunk
    assert dv == dk == LANES and (t_rows % chunk == 0 or t_rows < chunk) and (n_chunks == 1 or n_chunks % 2 == 0)
    has_s0 = s0 is not None
    in_specs = [
        pl.BlockSpec((t_rows, dk), lambda b, hh: (b, hh)),
        pl.BlockSpec((t_rows, dk), lambda b, hh: (b, hh)),
        pl.BlockSpec((t_rows, dv), lambda b, hh: (b, hh)),
        pl.BlockSpec((t_rows, dv), lambda b, hh: (b, h + hh)),
        pl.BlockSpec((depth, dk), lambda b, hh: (0, hh)),
        pl.BlockSpec((1, dv), lambda b, hh: (0, 0)),
    ]
    args = [pq, pf, pig, pig, lb_logits, o_gain.reshape(1, dv)]
    if has_s0:
        in_specs.append(pl.BlockSpec((None, None, None, dk, dv), lambda b, hh: (slot, b, hh, 0, 0)))
        args.append(s0)
    cshape = (chunk, dk)
    return pl.pallas_call(
        functools.partial(_hgrn_kernel, layer=layer, t_rows=t_rows, chunk=chunk, has_s0=has_s0),
        out_shape=(jax.ShapeDtypeStruct((m, h * dv), out_dtype),
                   jax.ShapeDtypeStruct((batch, h, dk, dv), F32)),
        grid=(batch, h),
        in_specs=in_specs,
        out_specs=(pl.BlockSpec((t_rows, dv), lambda b, hh: (b, hh)),
                   pl.BlockSpec((None, None, dk, dv), lambda b, hh: (b, hh, 0, 0))),
        scratch_shapes=[pltpu.VMEM(cshape, F32)] * 6,
        compiler_params=_cparams(("parallel", "parallel")),
        name="hgrn2",
    )(*args)


def _head_norm(x, gain):
    return x * lax.rsqrt(jnp.mean(x * x, axis=-1, keepdims=True) + EPS) * gain


def _alibi_slope(h_index, n_heads):
    hv = jnp.full((1, 1), h_index + 1, jnp.int32).astype(F32)
    return jnp.exp2(hv * (-8.0 / n_heads))


def _moba_prompt_kernel(q_ref, k_ref, v_ref, g_ref, qg_ref, kg_ref, o_ref, kn_ref,
                        ka_scr, vb_scr, km_scr, *, n_blocks):
    blk = MOBA_BLOCK
    h = pl.program_id(1)
    c = pl.program_id(2)
    hd = q_ref.shape[1]
    t_all = k_ref.shape[0]

    @pl.when(c == 0)
    def _():
        kn = _head_norm(k_ref[...], kg_ref[...])
        kn_ref[...] = kn
        row_blk = lax.broadcasted_iota(jnp.int32, (t_all, hd), 0) // blk
        lane_k = lax.broadcasted_iota(jnp.int32, (t_all, hd), 1)
        one_hot = jnp.where(lane_k == row_blk, 1.0, 0.0).astype(BF16)
        ka_scr[...] = jnp.concatenate([kn.astype(BF16), one_hot], axis=1)
        vb_scr[...] = v_ref[...].astype(BF16)
        means = [jnp.mean(kn[n * blk:(n + 1) * blk, :], axis=0, keepdims=True) for n in range(n_blocks)]
        km_scr[...] = _pad_rows(jnp.concatenate(means, axis=0), LANES)

    qn = _head_norm(q_ref[...], qg_ref[...])
    qs = (qn * (hd ** -0.5 * LOG2E)).astype(BF16)
    slope2 = _alibi_slope(h, MOBA_HEADS) * LOG2E
    key_off = lax.broadcasted_iota(jnp.int32, (1, blk), 1).astype(F32)
    ti = lax.broadcasted_iota(jnp.int32, (blk, blk), 0)
    sj = lax.broadcasted_iota(jnp.int32, (blk, blk), 1)

    def tile(k):
        gated = k > MOBA_TOPK
        if gated:
            gate = _dot_nt_3pass(qn, km_scr[...])
            lane = lax.broadcasted_iota(jnp.int32, gate.shape, 1)
            gm = jnp.where(lane < k, gate, NEG_INF)
            rank = jnp.zeros(gate.shape, jnp.int32)
            for sh in range(1, k):
                other = pltpu.roll(gm, sh, axis=1)
                ahead = (other > gm) | ((other == gm) & (lane >= sh))
                rank = rank + ahead.astype(jnp.int32)
            drop = jnp.where((lane < k) & (rank >= MOBA_TOPK), MASK_NEG, 0.0).astype(BF16)
            q_aug = jnp.concatenate([qs, drop], axis=1)
        scores = []
        for n in range(k):
            keys = ka_scr[n * blk:(n + 1) * blk, :] if gated else ka_scr[n * blk:(n + 1) * blk, :hd]
            bias = slope2 * (key_off + float((n - k) * blk))
            scores.append(_dot_nt(q_aug if gated else qs, keys) + bias)
        s_own = _dot_nt(qs, ka_scr[k * blk:(k + 1) * blk, :hd]) + slope2 * key_off
        scores.append(jnp.where(ti >= sj, s_own, NEG_INF))

        m_el = scores[0]
        for s in scores[1:]:
            m_el = jnp.maximum(m_el, s)
        m = jnp.broadcast_to(jnp.max(m_el, axis=-1, keepdims=True), (blk, blk))
        l_el = jnp.zeros((blk, blk), F32)
        acc = jnp.zeros((blk, hd), F32)
        for n, s in enumerate(scores):
            p = jnp.exp2(s - m)
            l_el = l_el + p
            acc = acc + _dot(p.astype(BF16), vb_scr[n * blk:(n + 1) * blk, :])
        out = acc / jnp.sum(l_el, axis=-1, keepdims=True)
        o_ref[...] = (out * _silu(g_ref[...].astype(F32))).astype(o_ref.dtype)

    for k in range(n_blocks):
        pl.when(c == k)(functools.partial(tile, k))


def _moba_prompt(pqk, pv, pg, q_gain, k_gain, batch, t_rows, out_dtype):
    m, n = pqk.shape
    h = MOBA_HEADS
    hd = MOBA_HD
    assert n == 2 * h * hd and t_rows % MOBA_BLOCK == 0
    nb = t_rows // MOBA_BLOCK
    blk = MOBA_BLOCK
    return pl.pallas_call(
        functools.partial(_moba_prompt_kernel, n_blocks=nb),
        out_shape=(jax.ShapeDtypeStruct((m, h * hd), out_dtype),
                   jax.ShapeDtypeStruct((m, h * hd), F32)),
        grid=(batch, h, nb),
        in_specs=[
            pl.BlockSpec((blk, hd), lambda b, hh, c: (b * nb + c, hh)),
            pl.BlockSpec((t_rows, hd), lambda b, hh, c: (b, h + hh)),
            pl.BlockSpec((t_rows, hd), lambda b, hh, c: (b, hh)),
            pl.BlockSpec((blk, hd), lambda b, hh, c: (b * nb + c, hh)),
            pl.BlockSpec((1, hd), lambda b, hh, c: (0, 0)),
            pl.BlockSpec((1, hd), lambda b, hh, c: (0, 0)),
        ],
        out_specs=(pl.BlockSpec((blk, hd), lambda b, hh, c: (b * nb + c, hh)),
                   pl.BlockSpec((t_rows, hd), lambda b, hh, c: (b, hh))),
        scratch_shapes=[
            pltpu.VMEM((t_rows, 2 * hd), BF16),
            pltpu.VMEM((t_rows, hd), BF16),
            pltpu.VMEM((LANES, hd), F32),
        ],
        compiler_params=_cparams(("parallel", "parallel", "arbitrary")),
        name="moba_prompt",
    )(pqk, pqk, pv, pg, q_gain.reshape(1, hd), k_gain.reshape(1, hd))


KMEAN_PAGES_PER_STEP = 8


def _kmean_kernel(pt_ref, *refs):
    del pt_ref
    page_refs, o_ref = refs[:-1], refs[-1]
    n = pl.program_id(1)
    blocks_per_step = len(page_refs) // 2
    for j in range(blocks_per_step):
        tot = jnp.sum(page_refs[2 * j][...], axis=0) + jnp.sum(page_refs[2 * j + 1][...], axis=0)
        o_ref[n * blocks_per_step + j] = tot * (1.0 / MOBA_BLOCK)


def _moba_kmean(cache, page_table):
    _, page, heads, hd = cache.shape
    db, n_pages = page_table.shape
    pps = math.gcd(n_pages, KMEAN_PAGES_PER_STEP)
    assert MOBA_BLOCK == 2 * page and pps % 2 == 0
    nb = n_pages // 2

    def page_spec(j):
        return pl.BlockSpec((None, page, heads, hd), lambda b, n, pt: (pt[b, pps * n + j], 0, 0, 0))

    return pl.pallas_call(
        _kmean_kernel,
        out_shape=jax.ShapeDtypeStruct((db, nb, heads, hd), F32),
        grid_spec=pltpu.PrefetchScalarGridSpec(
            num_scalar_prefetch=1,
            grid=(db, n_pages // pps),
            in_specs=[page_spec(j) for j in range(pps)],
            out_specs=pl.BlockSpec((None, nb, heads, hd), lambda b, n, pt: (b, 0, 0, 0)),
        ),
        compiler_params=_cparams(("parallel", "arbitrary")),
        name="moba_past_block_means",
    )(page_table, *([cache] * pps))


def _moba_select_kernel(q_ref, k_ref, km_ref, qg_ref, kg_ref, qn_ref, kn_ref, idx_ref, *, n_blocks):
    hd = MOBA_HD
    t = q_ref.shape[0]
    lane = lax.broadcasted_iota(jnp.int32, (t, LANES), 1)
    for h in range(MOBA_HEADS):
        cols = slice(h * hd, (h + 1) * hd)
        qn = _head_norm(q_ref[:, cols], qg_ref[...])
        kn = _head_norm(k_ref[:, cols], kg_ref[...])
        qn_ref[:, cols] = qn
        kn_ref[:, cols] = kn
        km = _pad_rows(km_ref[:, h, :], LANES)
        gate = _dot_nt_3pass(qn, km)
        gm = jnp.where(lane < n_blocks, gate, NEG_INF)
        picks = jnp.zeros((t, LANES), jnp.int32)
        for j in range(MOBA_TOPK):
            best = jnp.max(gm, axis=-1, keepdims=True)
            ix = jnp.min(jnp.where(gm == best, lane, LANES), axis=-1, keepdims=True)
            picks = jnp.where(lane == j, ix, picks)
            gm = jnp.where(lane == ix, NEG_INF, gm)
        idx_ref[h] = picks


def _moba_select(p, kmean, q_gain, k_gain, batch, t_rows):
    m, n = p.shape
    w = MOBA_HEADS * MOBA_HD
    nb = kmean.shape[1]
    assert nb >= MOBA_TOPK and nb <= LANES
    return pl.pallas_call(
        functools.partial(_moba_select_kernel, n_blocks=nb),
        out_shape=(jax.ShapeDtypeStruct((m, w), F32),
                   jax.ShapeDtypeStruct((m, w), F32),
                   jax.ShapeDtypeStruct((batch, MOBA_HEADS, t_rows, LANES), jnp.int32)),
        grid=(batch,),
        in_specs=[
            pl.BlockSpec((t_rows, w), lambda b: (b, 0)),
            pl.BlockSpec((t_rows, w), lambda b: (b, 1)),
            pl.BlockSpec((None, nb, MOBA_HEADS, MOBA_HD), lambda b: (b, 0, 0, 0)),
            pl.BlockSpec((1, MOBA_HD), lambda b: (0, 0)),
            pl.BlockSpec((1, MOBA_HD), lambda b: (0, 0)),
        ],
        out_specs=(pl.BlockSpec((t_rows, w), lambda b: (b, 0)),
                   pl.BlockSpec((t_rows, w), lambda b: (b, 0)),
                   pl.BlockSpec((None, MOBA_HEADS, t_rows, LANES), lambda b: (b, 0, 0, 0))),
        compiler_params=_cparams(("parallel",)),
        name="moba_sample_select",
    )(p, p, kmean, q_gain.reshape(1, MOBA_HD), k_gain.reshape(1, MOBA_HD))


def _moba_sample_attn_kernel(idx_ref, pt_ref, ck_ref, cv_ref, qn_ref, kn_ref, vn_ref, g_ref, o_ref,
                             kbuf, vbuf, sem, *, past_len, t_rows):
    step = pl.program_id(0)
    n_steps = pl.num_programs(0)
    h = step % MOBA_HEADS
    hd = MOBA_HD
    blk = MOBA_BLOCK
    page = blk // 2
    n_sel = t_rows * MOBA_TOPK
    scale = hd ** -0.5
    slope = _alibi_slope(h, MOBA_HEADS)

    def for_each_copy(s, slot, fn):
        sb = s // MOBA_HEADS
        sh = s % MOBA_HEADS

        def body(u, carry):
            blk_id = idx_ref[s * n_sel + u]
            for half in range(2):
                pg = pt_ref[sb, 2 * blk_id + half]
                rows = pl.ds(half * page, page)
                fn(pltpu.make_async_copy(ck_ref.at[pg, :, sh, :], kbuf.at[slot, u, rows, :], sem.at[slot]))
                fn(pltpu.make_async_copy(cv_ref.at[pg, :, sh, :], vbuf.at[slot, u, rows, :], sem.at[slot]))
            return carry

        lax.fori_loop(0, n_sel, body, 0)

    slot = step % 2

    @pl.when(step == 0)
    def _():
        for_each_copy(step, slot, lambda cp: cp.start())

    @pl.when(step + 1 < n_steps)
    def _():
        for_each_copy(step + 1, 1 - slot, lambda cp: cp.start())

    for_each_copy(step, slot, lambda cp: cp.wait())

    offs = lax.broadcasted_iota(jnp.int32, (1, blk), 1)
    key_i = lax.broadcasted_iota(jnp.int32, (t_rows, 1), 0)
    kn = kn_ref[...]
    vn = vn_ref[...]

    for qi in range(t_rows):
        qrow = qn_ref[qi:qi + 1, :]
        q8 = jnp.broadcast_to(qrow, (8, hd)).astype(BF16)
        t_pos = past_len + qi
        s_own = jnp.sum(kn * qrow, axis=-1, keepdims=True) * scale - slope * (qi - key_i).astype(F32)
        s_own = jnp.where(key_i <= qi, s_own, NEG_INF)
        m = jnp.max(s_own, axis=0, keepdims=True)
        scores = []
        for j in range(MOBA_TOPK):
            u = qi * MOBA_TOPK + j
            dist = (t_pos - idx_ref[step * n_sel + u] * blk - offs).astype(F32)
            s = _dot_nt(q8, kbuf[slot, u].astype(BF16))[0:1, :] * scale - slope * dist
            scores.append(s)
            m = jnp.maximum(m, jnp.max(s, axis=-1, keepdims=True))
        p_own = jnp.exp(s_own - m)
        denom = jnp.sum(p_own, axis=0, keepdims=True)
        out = jnp.sum(p_own * vn, axis=0, keepdims=True)
        for j, s in enumerate(scores):
            u = qi * MOBA_TOPK + j
            pj = jnp.exp(s - m)
            denom = denom + jnp.sum(pj, axis=-1, keepdims=True)
            out = out + _dot(jnp.broadcast_to(pj, (8, blk)).astype(BF16), vbuf[slot, u].astype(BF16))[0:1, :]
        gate = g_ref[qi:qi + 1, :]
        o_ref[qi:qi + 1, :] = out / denom * _silu(gate)


def _moba_sample_attn(idx, page_table, cache_k, cache_v, qn, kn, vn, gate, batch, t_rows):
    h = MOBA_HEADS
    hd = MOBA_HD
    _, page, heads, _ = cache_k.shape
    n_pages = page_table.shape[1]
    assert MOBA_BLOCK == 2 * page and heads == h
    n_sel = t_rows * MOBA_TOPK
    row_spec = pl.BlockSpec((t_rows, hd), lambda s, ix, pt: (s // h, s % h))
    return pl.pallas_call(
        functools.partial(_moba_sample_attn_kernel, past_len=n_pages * page, t_rows=t_rows),
        out_shape=jax.ShapeDtypeStruct((batch * t_rows, h * hd), F32),
        grid_spec=pltpu.PrefetchScalarGridSpec(
            num_scalar_prefetch=2,
            grid=(batch * h,),
            in_specs=[pl.BlockSpec(memory_space=pl.ANY), pl.BlockSpec(memory_space=pl.ANY),
                      row_spec, row_spec, row_spec, row_spec],
            out_specs=row_spec,
            scratch_shapes=[pltpu.VMEM((2, n_sel, MOBA_BLOCK, hd), F32),
                            pltpu.VMEM((2, n_sel, MOBA_BLOCK, hd), F32),
                            pltpu.SemaphoreType.DMA((2,))],
        ),
        compiler_params=_cparams(("arbitrary",)),
        name="moba_sample_attention",
    )(idx, page_table, cache_k, cache_v, qn, kn, vn, gate)


def _retention_layer(hp, hs, gain, w_in, w_out, slot, state, bp, tp, bs, ts):
    (pp,), (ps,) = _in_proj(hp, hs, gain, w_in, slot, [(w_in.shape[2], BF16)])
    op, st_p = _retention(pp, None, 0, bp, tp, BF16)
    o_s, st_s = _retention(ps, state, slot, bs, ts, F32)
    hp, hs = _out_proj(op, o_s, w_out, slot, hp, hs)
    return hp, hs, st_p, st_s


def _hgrn_layer(hp, hs, gain, w_in, lb_logits, o_gain, w_out, slot, state, layer, bp, tp, bs, ts):
    w = w_in.shape[2] // 4
    (pq, pf, pig), (sq, sf, sig) = _in_proj(hp, hs, gain, w_in, slot, [(w, BF16), (w, F32), (2 * w, BF16)])
    op, st_p = _hgrn(pq, pf, pig, lb_logits, o_gain, None, 0, layer, bp, tp, BF16)
    o_s, st_s = _hgrn(sq, sf, sig, lb_logits, o_gain, state, slot, layer, bs, ts, F32)
    hp, hs = _out_proj(op, o_s, w_out, slot, hp, hs)
    return hp, hs, st_p, st_s


def _moba_layer(hp, hs, gain, w_in, q_gain, k_gain, w_out, slot, cache_k, cache_v, page_table, bp, tp, bs, ts):
    w = MOBA_HEADS * MOBA_HD
    (pqk, pv, pg), (sqk, sv, sg) = _in_proj(hp, hs, gain, w_in, slot, [(2 * w, F32), (w, F32), (w, BF16)])
    op, kn_p = _moba_prompt(pqk, pv, pg, q_gain, k_gain, bp, tp, BF16)
    kmean = _moba_kmean(cache_k, page_table)
    qn, kn_s, picks = _moba_select(sqk, kmean, q_gain, k_gain, bs, ts)
    idx = picks[..., :MOBA_TOPK].reshape(-1)
    o_s = _moba_sample_attn(idx, page_table, cache_k, cache_v, qn, kn_s, sv, sg, bs, ts)
    hp, hs = _out_proj(op, o_s, w_out, slot, hp, hs)
    return hp, hs, kn_p, pv, kn_s, sv


def kernel(x_prompt, x_sample, state_ret, cache_k, cache_v, state_hgrn, page_table, norm_gain, ret_w_in, ret_w_out, moba_w_in, moba_q_gain, moba_k_gain, moba_w_out, hgrn_w_in, hgrn_lb_logits, hgrn_o_gain, hgrn_w_out):
    bp, tp, d = x_prompt.shape
    bs, ts, _ = x_sample.shape
    depth = norm_gain.shape[0]
    n_pool = cache_k.shape[1]
    hp = x_prompt.reshape(bp * tp, d)
    hs = x_sample.reshape(bs * ts, d)
    ck = cache_k.reshape((-1,) + cache_k.shape[2:])
    cv = cache_v.reshape((-1,) + cache_v.shape[2:])
    kp_l, vp_l, ks_l, vs_l, rp_l, rs_l, gp_l, gs_l = [], [], [], [], [], [], [], []
    for layer in range(depth):
        kind, slot = layer % N_MIXERS, layer // N_MIXERS
        gain = norm_gain[layer]
        if kind == 0:
            hp, hs, st_p, st_s = _retention_layer(hp, hs, gain, ret_w_in, ret_w_out, slot, state_ret,
                                                  bp, tp, bs, ts)
            rp_l.append(st_p)
            rs_l.append(st_s)
        elif kind == 1:
            hp, hs, kp, vp, ks_, vs_ = _moba_layer(hp, hs, gain, moba_w_in, moba_q_gain[slot], moba_k_gain[slot],
                                                   moba_w_out, slot, ck, cv, page_table + slot * n_pool,
                                                   bp, tp, bs, ts)
            shp = (MOBA_HEADS, MOBA_HD)
            kp_l.append(kp.reshape(bp, tp, *shp))
            vp_l.append(vp.reshape(bp, tp, *shp))
            ks_l.append(ks_.reshape(bs, ts, *shp))
            vs_l.append(vs_.reshape(bs, ts, *shp))
        else:
            hp, hs, st_p, st_s = _hgrn_layer(hp, hs, gain, hgrn_w_in, hgrn_lb_logits, hgrn_o_gain[slot],
                                             hgrn_w_out, slot, state_hgrn, layer, bp, tp, bs, ts)
            gp_l.append(st_p)
            gs_l.append(st_s)
    return (hp.reshape(bp, tp, d), hs.reshape(bs, ts, d),
            jnp.stack(kp_l), jnp.stack(vp_l), jnp.stack(ks_l), jnp.stack(vs_l),
            jnp.stack(rp_l), jnp.stack(rs_l), jnp.stack(gp_l), jnp.stack(gs_l))
```

```python
import functools
import math

import numpy as np
import jax
import jax.numpy as jnp
from jax import lax
from jax.experimental import pallas as pl
from jax.experimental.pallas import tpu as pltpu

F32 = jnp.float32
BF16 = jnp.bfloat16
EPS = 1e-6
NEG_INF = float("-inf")
LOG2E = 1.4426950408889634
MASK_NEG = -1e30

N_MIXERS = 3
RET_HEADS = 8
RET_CHUNK = 128
MOBA_HEADS = 16
MOBA_HD = 128
MOBA_BLOCK = 256
MOBA_TOPK = 3
HG_HEADS = 16
HG_DK = 128
HG_CHUNK = 128
HG_SUB = 16
HG_FACTORED_SUB = 64
HG_FACTORED_MAX_LOG2_SPAN = 100.0

VMEM_LIMIT_BYTES = 52 * 1024 * 1024
LANES = 128


def _cparams(sem):
    return pltpu.CompilerParams(dimension_semantics=sem, vmem_limit_bytes=VMEM_LIMIT_BYTES)


def _dot(a, b):
    return jnp.dot(a, b, preferred_element_type=F32)


def _dot_nt(a, b):
    return lax.dot_general(a, b, (((1,), (1,)), ((), ())), preferred_element_type=F32)


def _dot_tn(a, b):
    return lax.dot_general(a, b, (((0,), (0,)), ((), ())), preferred_element_type=F32)


def _split_bf16(x, parts):
    pieces = []
    for _ in range(parts):
        hi = x.astype(BF16)
        pieces.append(hi)
        x = x - hi.astype(F32)
    return pieces


def _dot_nt_3pass(a, b):
    a_hi, a_lo = _split_bf16(a, 2)
    b_hi, b_lo = _split_bf16(b, 2)
    return _dot_nt(a_hi, b_hi) + (_dot_nt(a_hi, b_lo) + _dot_nt(a_lo, b_hi))


def _silu(x):
    return x * jax.nn.sigmoid(x)


def _pad_rows(x, rows):
    if x.shape[0] == rows:
        return x
    return jnp.concatenate([x, jnp.zeros((rows - x.shape[0], x.shape[1]), x.dtype)], axis=0)


PROJ_TM, PROJ_TN = 2048, 256


def _proj_kernel(*refs, n_i, n_groups, bounds, norm, residual):
    it = iter(refs)
    xp_ref, xs_ref = next(it), next(it)
    gain_ref = next(it) if norm else None
    w_first_ref, w_next_ref = next(it), next(it)
    res_p = [next(it) for _ in range(n_groups)] if residual else None
    res_s = [next(it) for _ in range(n_groups)] if residual else None
    outs_p = [next(it) for _ in range(n_groups)]
    outs_s = [next(it) for _ in range(n_groups)]
    wb_even, wb_odd = next(it), next(it)
    xn_ref = next(it) if norm else None
    i = pl.program_id(0)
    j = pl.program_id(1)
    ms_rows = xs_ref.shape[0]

    if norm:
        def normed(x):
            return (x * lax.rsqrt(jnp.mean(x * x, axis=-1, keepdims=True) + EPS) * gain_ref[...]).astype(BF16)

        @pl.when((j == 0) & (i < n_i))
        def _():
            xn_ref[...] = normed(xp_ref[...])

        @pl.when((j == 0) & (i == n_i))
        def _():
            xn_ref[:ms_rows, :] = normed(xs_ref[...])

    @pl.when((i == 0) & (j == 0))
    def _():
        wb_even[...] = w_first_ref[...].astype(BF16)

    def step(w_cur, w_nxt):
        for g, (j0, j1) in enumerate(bounds):
            in_group = (j >= j0) & (j < j1)

            @pl.when(in_group & (i < n_i))
            def _(g=g):
                w_nxt[...] = w_next_ref[...].astype(BF16)
                acc = _dot(xn_ref[...] if norm else xp_ref[...], w_cur[...])
                if residual:
                    acc = acc + res_p[g][...]
                outs_p[g][...] = acc.astype(outs_p[g].dtype)

            @pl.when(in_group & (i == n_i))
            def _(g=g):
                w_nxt[...] = w_next_ref[...].astype(BF16)
                acc = _dot(xn_ref[:ms_rows, :] if norm else xs_ref[...].astype(BF16), w_cur[...])
                if residual:
                    acc = acc + res_s[g][...]
                outs_s[g][...] = acc

    pl.when(j % 2 == 0)(functools.partial(step, wb_even, wb_odd))
    pl.when(j % 2 == 1)(functools.partial(step, wb_odd, wb_even))


def _proj(xp, xs, w, slot, groups, gain=None, residual=None, name="proj"):
    mp, d = xp.shape
    ms = xs.shape[0]
    n = w.shape[2]
    tm, tn = min(PROJ_TM, mp), PROJ_TN
    n_i, n_j = mp // tm, n // tn
    assert mp % tm == 0 and ms <= tm and n % tn == 0 and n_j % 2 == 0
    assert sum(c for c, _ in groups) == n and all(c % tn == 0 for c, _ in groups)
    assert residual is None or len(groups) == 1
    norm = gain is not None
    bounds, j0 = [], 0
    for c, _ in groups:
        bounds.append((j0, j0 + c // tn))
        j0 += c // tn

    def prompt_map(j0, j1):
        def imap(i, j):
            jj = jnp.where(i < n_i, j, n_j - 1)
            return (jnp.minimum(i, n_i - 1), jnp.clip(jj - j0, 0, j1 - j0 - 1))
        return imap

    def sample_map(j0, j1):
        def imap(i, j):
            jj = jnp.where(i < n_i, 0, j)
            return (0, jnp.clip(jj - j0, 0, j1 - j0 - 1))
        return imap

    p_specs = [pl.BlockSpec((tm, tn), prompt_map(*b)) for b in bounds]
    s_specs = [pl.BlockSpec((ms, tn), sample_map(*b)) for b in bounds]
    in_specs = [
        pl.BlockSpec((tm, d), lambda i, j: (jnp.minimum(i, n_i - 1), 0), pipeline_mode=pl.Buffered(1)),
        pl.BlockSpec((ms, d), lambda i, j: (0, 0)),
    ]
    args = [xp, xs]
    if norm:
        in_specs.append(pl.BlockSpec((1, d), lambda i, j: (0, 0)))
        args.append(gain.reshape(1, d))
    in_specs += [
        pl.BlockSpec((None, d, tn), lambda i, j: (slot, 0, 0), pipeline_mode=pl.Buffered(1)),
        pl.BlockSpec((None, d, tn), lambda i, j: (slot, 0, (j + 1) % n_j)),
    ]
    args += [w, w]
    if residual is not None:
        in_specs += p_specs + s_specs
        args += list(residual)
    scratch = [pltpu.VMEM((d, tn), BF16), pltpu.VMEM((d, tn), BF16)]
    if norm:
        scratch.append(pltpu.VMEM((tm, d), BF16))
    outs = pl.pallas_call(
        functools.partial(_proj_kernel, n_i=n_i, n_groups=len(groups), bounds=tuple(bounds),
                          norm=norm, residual=residual is not None),
        out_shape=([jax.ShapeDtypeStruct((mp, c), dt) for c, dt in groups]
                   + [jax.ShapeDtypeStruct((ms, c), F32) for c, _ in groups]),
        grid=(n_i + 1, n_j),
        in_specs=in_specs,
        out_specs=p_specs + s_specs,
        scratch_shapes=scratch,
        compiler_params=_cparams(("arbitrary", "arbitrary")),
        name=name,
    )(*args)
    return outs[:len(groups)], outs[len(groups):]


def _in_proj(xp, xs, gain, w, slot, groups):
    return _proj(xp, xs, w, slot, groups, gain=gain, name="norm_in_proj")


def _out_proj(ap, a_s, w, slot, rp, rs):
    (op,), (o_s,) = _proj(ap, a_s, w, slot, [(w.shape[2], F32)], residual=(rp, rs), name="out_proj_residual")
    return op, o_s


def _ret_tables(c, dk):
    scale = dk ** -0.5
    log_g = np.log1p(-np.exp2(-5.0 - np.arange(RET_HEADS, dtype=np.float64)))
    i = np.arange(c, dtype=np.float64)
    rel = i[:, None] - i[None, :]
    intra = np.where(rel >= 0, np.exp(log_g[:, None, None] * np.maximum(rel, 0.0)), 0.0) * scale
    q_dec = np.exp(log_g[:, None] * (i + 1.0))
    k_dec = np.exp(log_g[:, None] * (c - 1.0 - i)) * scale
    c_dec = np.exp(log_g * c)
    big = RET_CHUNK
    intra_p = np.zeros((RET_HEADS, big, big), np.float32)
    intra_p[:, :c, :c] = intra
    qd = np.zeros((RET_HEADS, big, 1), np.float32)
    qd[:, :c, 0] = q_dec
    kd = np.zeros((RET_HEADS, big, 1), np.float32)
    kd[:, :c, 0] = k_dec
    cd = c_dec.astype(np.float32).reshape(RET_HEADS, 1, 1)
    return jnp.asarray(intra_p), jnp.asarray(qd), jnp.asarray(kd), jnp.asarray(cd)


def _ret_kernel(*refs, t_rows, has_s0):
    if has_s0:
        q_ref, k_ref, v_ref, g_ref, intra_ref, qd_ref, kd_ref, cd_ref, s0_ref, o_ref, s_ref, st = refs
        st[...] = s0_ref[...]
    else:
        q_ref, k_ref, v_ref, g_ref, intra_ref, qd_ref, kd_ref, cd_ref, o_ref, s_ref, st = refs
        st[...] = jnp.zeros_like(st)
    c = RET_CHUNK
    t_pad = max(t_rows, c)
    intra = intra_ref[...]
    qd = qd_ref[...]
    kd = kd_ref[...]
    cd = cd_ref[...]

    def chunk(ci, carry):
        if t_rows >= c:
            rows = pl.ds(pl.multiple_of(ci * c, c), c)
        else:
            rows = pl.ds(0, t_rows)
        qc = _pad_rows(q_ref[rows, :], c).astype(BF16)
        kf = _pad_rows(k_ref[rows, :], c).astype(F32)
        vc = _pad_rows(v_ref[rows, :], c).astype(BF16)
        s = st[...]
        att = _dot_nt(qc, kf.astype(BF16)) * intra
        o = _dot(att.astype(BF16), vc) + qd * _dot(qc, s.astype(BF16))
        st[...] = s * cd + _dot_tn((kf * kd).astype(BF16), vc)
        o = o * lax.rsqrt(jnp.mean(o * o, axis=-1, keepdims=True) + EPS)
        gate = g_ref[rows, :].astype(F32)
        o_ref[rows, :] = (o[: gate.shape[0]] * _silu(gate)).astype(o_ref.dtype)
        return carry

    n_chunks = t_pad // c
    lax.fori_loop(0, n_chunks, chunk, 0, unroll=2 if n_chunks % 2 == 0 else 1)
    s_ref[...] = st[...]


def _retention(p, s0, slot, batch, t_rows, out_dtype):
    m, n = p.shape
    h = RET_HEADS
    dk = n // (6 * h)
    dv = 2 * dk
    c_eff = math.gcd(t_rows, RET_CHUNK)
    assert t_rows % c_eff == 0 and (t_rows >= RET_CHUNK or t_rows == c_eff)
    intra, qd, kd, cd = _ret_tables(c_eff, dk)
    has_s0 = s0 is not None
    in_specs = [
        pl.BlockSpec((t_rows, dk), lambda b, hh: (b, hh)),
        pl.BlockSpec((t_rows, dk), lambda b, hh: (b, h + hh)),
        pl.BlockSpec((t_rows, dv), lambda b, hh: (b, h + hh)),
        pl.BlockSpec((t_rows, dv), lambda b, hh: (b, 2 * h + hh)),
        pl.BlockSpec((None, RET_CHUNK, RET_CHUNK), lambda b, hh: (hh, 0, 0)),
        pl.BlockSpec((None, RET_CHUNK, 1), lambda b, hh: (hh, 0, 0)),
        pl.BlockSpec((None, RET_CHUNK, 1), lambda b, hh: (hh, 0, 0)),
        pl.BlockSpec((None, 1, 1), lambda b, hh: (hh, 0, 0)),
    ]
    args = [p, p, p, p, intra, qd, kd, cd]
    if has_s0:
        in_specs.append(pl.BlockSpec((None, None, None, dk, dv), lambda b, hh: (slot, b, hh, 0, 0)))
        args.append(s0)
    return pl.pallas_call(
        functools.partial(_ret_kernel, t_rows=t_rows, has_s0=has_s0),
        out_shape=(jax.ShapeDtypeStruct((m, h * dv), out_dtype),
                   jax.ShapeDtypeStruct((batch, h, dk, dv), F32)),
        grid=(batch, h),
        in_specs=in_specs,
        out_specs=(pl.BlockSpec((t_rows, dv), lambda b, hh: (b, hh)),
                   pl.BlockSpec((None, None, dk, dv), lambda b, hh: (b, hh, 0, 0))),
        scratch_shapes=[pltpu.VMEM((dk, dv), F32)],
        compiler_params=_cparams(("parallel", "parallel")),
        name="retention",
    )(*args)


def _hgrn_kernel(*refs, layer, t_rows, chunk, has_s0):
    n_in = 7 if has_s0 else 6
    q_ref, f_ref, i_ref, g_ref, lbl_ref, og_ref = refs[:6]
    o_ref, s_ref = refs[n_in:n_in + 2]
    scratch_sets = (refs[n_in + 2:n_in + 5], refs[n_in + 5:n_in + 8])
    c = chunk
    r = HG_SUB
    n_sub = c // r
    n_chunks = max(t_rows, c) // c
    dk = q_ref.shape[1]

    logits = lbl_ref[...]
    e = jnp.exp(logits - jnp.max(logits, axis=0, keepdims=True))
    prob = e / jnp.sum(e, axis=0, keepdims=True)
    lrow = lax.broadcasted_iota(jnp.int32, prob.shape, 0)
    in_lb = (lrow >= 1) & (lrow <= layer)
    lb = jnp.sum(jnp.where(in_lb, prob, 0.0), axis=0, keepdims=True)
    one_m_lb = jnp.sum(jnp.where(in_lb, 0.0, prob), axis=0, keepdims=True)

    ri = lax.broadcasted_iota(jnp.int32, (c, LANES), 0)
    cj = lax.broadcasted_iota(jnp.int32, (c, LANES), 1)
    tri_mask = ri >= cj
    tri = jnp.where(tri_mask, 1.0, 0.0).astype(BF16)
    row_c = lax.broadcasted_iota(jnp.int32, (c, dk), 0)
    row_r = lax.broadcasted_iota(jnp.int32, (r, dk), 0)
    og = og_ref[...]

    def one_chunk(ci, s_t, b_scr, k_scr, v_scr, factored):
        if t_rows >= c:
            rows = pl.ds(pl.multiple_of(ci * c, c), c)
        else:
            rows = pl.ds(0, t_rows)
        z = _pad_rows(f_ref[rows, :].astype(F32), c)
        qraw = _pad_rows(q_ref[rows, :].astype(F32), c)
        v = _pad_rows(i_ref[rows, :].astype(F32), c)
        f = lb + one_m_lb * jax.nn.sigmoid(z)
        logf = jnp.log(f)
        kk = one_m_lb * jax.nn.sigmoid(-z)
        if t_rows < c:
            valid = row_c < t_rows
            logf = jnp.where(valid, logf, 0.0)
            kk = jnp.where(valid, kk, 0.0)
        lf_hi, lf_mid, lf_lo = _split_bf16(_pad_rows(logf, LANES), 3)
        b = (_dot(tri, lf_hi) + (_dot(tri, lf_mid) + _dot(tri, lf_lo))) * LOG2E
        qs = _silu(qraw)
        vb = v.astype(BF16)
        o = _dot_nt((qs * jnp.exp2(b)).astype(BF16), s_t.astype(BF16))

        def sub_block_rows(rq, with_diagonal):
            a_rows = []
            for lo in range(0, c, rq):
                hi = lo + rq if with_diagonal else lo
                if hi == 0:
                    a_rows.append(jnp.zeros((rq, c), F32))
                    continue
                b0 = b[lo - 1:lo, :] if lo else jnp.zeros((1, dk), F32)
                qh = qs[lo:lo + rq, :] * jnp.exp2(b[lo:lo + rq, :] - b0)
                kh = kk[:hi, :] * jnp.exp2(b0 - b[:hi, :])
                a_rows.append(_dot_nt(qh.astype(BF16), _pad_rows(kh, c).astype(BF16)))
            return jnp.concatenate(a_rows, axis=0)

        def intra_factored():
            att = jnp.where(tri_mask, sub_block_rows(HG_FACTORED_SUB, True), 0.0)
            return _dot(att.astype(BF16), vb)

        def intra_pairwise():
            b_scr[...] = b
            k_scr[...] = kk
            v_scr[...] = v
            o_diag = []
            for sb in range(n_sub):
                lo = sb * r
                bi = b[lo:lo + r, :]
                qi = qs[lo:lo + r, :]
                acc = jnp.zeros((r, dk), F32)
                for s in range(min(r, t_rows)):
                    bs = b_scr[lo + s:lo + s + 1, :]
                    ks = k_scr[lo + s:lo + s + 1, :]
                    vs = v_scr[lo + s:lo + s + 1, :]
                    d = bi - bs
                    if s > 0:
                        d = jnp.where(row_r >= s, d, NEG_INF)
                    col = jnp.sum(qi * jnp.exp2(d) * ks, axis=-1, keepdims=True)
                    acc = acc + col * vs
                o_diag.append(acc)
            out = jnp.concatenate(o_diag, axis=0)
            if n_sub > 1:
                out = out + _dot(sub_block_rows(r, False).astype(BF16), vb)
            return out

        o = o + (intra_factored() if factored else intra_pairwise())

        b_end = b[c - 1:c, :]
        kd = kk * jnp.exp2(b_end - b)
        s_new = s_t * jnp.exp2(b_end) + _dot_tn(_pad_rows(vb, LANES), _pad_rows(kd.astype(BF16), LANES))

        o = o * lax.rsqrt(jnp.mean(o * o, axis=-1, keepdims=True) + EPS) * og
        gate = g_ref[rows, :].astype(F32)
        o_ref[rows, :] = (o[: gate.shape[0]] * _silu(gate)).astype(o_ref.dtype)
        return s_new

    s_init = refs[6][...].T if has_s0 else jnp.zeros((dk, dk), F32)
    def run(factored):
        if n_chunks == 1:
            return one_chunk(0, s_init, *scratch_sets[0], factored)

        def pair(i, s_t):
            s_t = one_chunk(2 * i, s_t, *scratch_sets[0], factored)
            return one_chunk(2 * i + 1, s_t, *scratch_sets[1], factored)

        return lax.fori_loop(0, n_chunks // 2, pair, s_init)

    if n_sub > 1:
        factored_ok = jnp.min(lb) >= 2.0 ** (-HG_FACTORED_MAX_LOG2_SPAN / HG_FACTORED_SUB)
        s_fin = lax.cond(factored_ok, functools.partial(run, True), functools.partial(run, False))
    else:
        s_fin = run(False)
    s_ref[...] = s_fin.T


def _hgrn(pq, pf, pig, lb_logits, o_gain, s0, slot, layer, batch, t_rows, out_dtype):
    m, n = pq.shape
    h = HG_HEADS
    dk = HG_DK
    dv = n // h
    depth = lb_logits.shape[0]
    chunk = HG_CHUNK if t_rows >= HG_CHUNK else HG_SUB
    n_chunks = max(t_rows, chunk) // chunk
    assert dv == dk == LANES and (t_rows % chunk == 0 or t_rows < chunk) and (n_chunks == 1 or n_chunks % 2 == 0)
    has_s0 = s0 is not None
    in_specs = [
        pl.BlockSpec((t_rows, dk), lambda b, hh: (b, hh)),
        pl.BlockSpec((t_rows, dk), lambda b, hh: (b, hh)),
        pl.BlockSpec((t_rows, dv), lambda b, hh: (b, hh)),
        pl.BlockSpec((t_rows, dv), lambda b, hh: (b, h + hh)),
        pl.BlockSpec((depth, dk), lambda b, hh: (0, hh)),
        pl.BlockSpec((1, dv), lambda b, hh: (0, 0)),
    ]
    args = [pq, pf, pig, pig, lb_logits, o_gain.reshape(1, dv)]
    if has_s0:
        in_specs.append(pl.BlockSpec((None, None, None, dk, dv), lambda b, hh: (slot, b, hh, 0, 0)))
        args.append(s0)
    cshape = (chunk, dk)
    return pl.pallas_call(
        functools.partial(_hgrn_kernel, layer=layer, t_rows=t_rows, chunk=chunk, has_s0=has_s0),
        out_shape=(jax.ShapeDtypeStruct((m, h * dv), out_dtype),
                   jax.ShapeDtypeStruct((batch, h, dk, dv), F32)),
        grid=(batch, h),
        in_specs=in_specs,
        out_specs=(pl.BlockSpec((t_rows, dv), lambda b, hh: (b, hh)),
                   pl.BlockSpec((None, None, dk, dv), lambda b, hh: (b, hh, 0, 0))),
        scratch_shapes=[pltpu.VMEM(cshape, F32)] * 6,
        compiler_params=_cparams(("parallel", "parallel")),
        name="hgrn2",
    )(*args)


def _head_norm(x, gain):
    return x * lax.rsqrt(jnp.mean(x * x, axis=-1, keepdims=True) + EPS) * gain


def _alibi_slope(h_index, n_heads):
    hv = jnp.full((1, 1), h_index + 1, jnp.int32).astype(F32)
    return jnp.exp2(hv * (-8.0 / n_heads))


def _moba_prompt_kernel(q_ref, k_ref, v_ref, g_ref, qg_ref, kg_ref, o_ref, kn_ref,
                        ka_scr, vb_scr, km_scr, *, n_blocks):
    blk = MOBA_BLOCK
    h = pl.program_id(1)
    c = pl.program_id(2)
    hd = q_ref.shape[1]
    t_all = k_ref.shape[0]

    @pl.when(c == 0)
    def _():
        kn = _head_norm(k_ref[...], kg_ref[...])
        kn_ref[...] = kn
        row_blk = lax.broadcasted_iota(jnp.int32, (t_all, hd), 0) // blk
        lane_k = lax.broadcasted_iota(jnp.int32, (t_all, hd), 1)
        one_hot = jnp.where(lane_k == row_blk, 1.0, 0.0).astype(BF16)
        ka_scr[...] = jnp.concatenate([kn.astype(BF16), one_hot], axis=1)
        vb_scr[...] = v_ref[...].astype(BF16)
        means = [jnp.mean(kn[n * blk:(n + 1) * blk, :], axis=0, keepdims=True) for n in range(n_blocks)]
        km_scr[...] = _pad_rows(jnp.concatenate(means, axis=0), LANES)

    qn = _head_norm(q_ref[...], qg_ref[...])
    qs = (qn * (hd ** -0.5 * LOG2E)).astype(BF16)
    slope2 = _alibi_slope(h, MOBA_HEADS) * LOG2E
    key_off = lax.broadcasted_iota(jnp.int32, (1, blk), 1).astype(F32)
    ti = lax.broadcasted_iota(jnp.int32, (blk, blk), 0)
    sj = lax.broadcasted_iota(jnp.int32, (blk, blk), 1)

    def tile(k):
        gated = k > MOBA_TOPK
        if gated:
            gate = _dot_nt_3pass(qn, km_scr[...])
            lane = lax.broadcasted_iota(jnp.int32, gate.shape, 1)
            gm = jnp.where(lane < k, gate, NEG_INF)
            rank = jnp.zeros(gate.shape, jnp.int32)
            for sh in range(1, k):
                other = pltpu.roll(gm, sh, axis=1)
                ahead = (other > gm) | ((other == gm) & (lane >= sh))
                rank = rank + ahead.astype(jnp.int32)
            drop = jnp.where((lane < k) & (rank >= MOBA_TOPK), MASK_NEG, 0.0).astype(BF16)
            q_aug = jnp.concatenate([qs, drop], axis=1)
        scores = []
        for n in range(k):
            keys = ka_scr[n * blk:(n + 1) * blk, :] if gated else ka_scr[n * blk:(n + 1) * blk, :hd]
            bias = slope2 * (key_off + float((n - k) * blk))
            scores.append(_dot_nt(q_aug if gated else qs, keys) + bias)
        s_own = _dot_nt(qs, ka_scr[k * blk:(k + 1) * blk, :hd]) + slope2 * key_off
        scores.append(jnp.where(ti >= sj, s_own, NEG_INF))

        m_el = scores[0]
        for s in scores[1:]:
            m_el = jnp.maximum(m_el, s)
        m = jnp.broadcast_to(jnp.max(m_el, axis=-1, keepdims=True), (blk, blk))
        l_el = jnp.zeros((blk, blk), F32)
        acc = jnp.zeros((blk, hd), F32)
        for n, s in enumerate(scores):
            p = jnp.exp2(s - m)
            l_el = l_el + p
            acc = acc + _dot(p.astype(BF16), vb_scr[n * blk:(n + 1) * blk, :])
        out = acc / jnp.sum(l_el, axis=-1, keepdims=True)
        o_ref[...] = (out * _silu(g_ref[...].astype(F32))).astype(o_ref.dtype)

    for k in range(n_blocks):
        pl.when(c == k)(functools.partial(tile, k))


def _moba_prompt(pqk, pv, pg, q_gain, k_gain, batch, t_rows, out_dtype):
    m, n = pqk.shape
    h = MOBA_HEADS
    hd = MOBA_HD
    assert n == 2 * h * hd and t_rows % MOBA_BLOCK == 0
    nb = t_rows // MOBA_BLOCK
    blk = MOBA_BLOCK
    return pl.pallas_call(
        functools.partial(_moba_prompt_kernel, n_blocks=nb),
        out_shape=(jax.ShapeDtypeStruct((m, h * hd), out_dtype),
                   jax.ShapeDtypeStruct((m, h * hd), F32)),
        grid=(batch, h, nb),
        in_specs=[
            pl.BlockSpec((blk, hd), lambda b, hh, c: (b * nb + c, hh)),
            pl.BlockSpec((t_rows, hd), lambda b, hh, c: (b, h + hh)),
            pl.BlockSpec((t_rows, hd), lambda b, hh, c: (b, hh)),
            pl.BlockSpec((blk, hd), lambda b, hh, c: (b * nb + c, hh)),
            pl.BlockSpec((1, hd), lambda b, hh, c: (0, 0)),
            pl.BlockSpec((1, hd), lambda b, hh, c: (0, 0)),
        ],
        out_specs=(pl.BlockSpec((blk, hd), lambda b, hh, c: (b * nb + c, hh)),
                   pl.BlockSpec((t_rows, hd), lambda b, hh, c: (b, hh))),
        scratch_shapes=[
            pltpu.VMEM((t_rows, 2 * hd), BF16),
            pltpu.VMEM((t_rows, hd), BF16),
            pltpu.VMEM((LANES, hd), F32),
        ],
        compiler_params=_cparams(("parallel", "parallel", "arbitrary")),
        name="moba_prompt",
    )(pqk, pqk, pv, pg, q_gain.reshape(1, hd), k_gain.reshape(1, hd))


KMEAN_PAGES_PER_STEP = 8


def _kmean_kernel(pt_ref, *refs):
    del pt_ref
    page_refs, o_ref = refs[:-1], refs[-1]
    n = pl.program_id(1)
    blocks_per_step = len(page_refs) // 2
    for j in range(blocks_per_step):
        tot = jnp.sum(page_refs[2 * j][...], axis=0) + jnp.sum(page_refs[2 * j + 1][...], axis=0)
        o_ref[n * blocks_per_step + j] = tot * (1.0 / MOBA_BLOCK)


def _moba_kmean(cache, page_table):
    _, page, heads, hd = cache.shape
    db, n_pages = page_table.shape
    pps = math.gcd(n_pages, KMEAN_PAGES_PER_STEP)
    assert MOBA_BLOCK == 2 * page and pps % 2 == 0
    nb = n_pages // 2

    def page_spec(j):
        return pl.BlockSpec((None, page, heads, hd), lambda b, n, pt: (pt[b, pps * n + j], 0, 0, 0))

    return pl.pallas_call(
        _kmean_kernel,
        out_shape=jax.ShapeDtypeStruct((db, nb, heads, hd), F32),
        grid_spec=pltpu.PrefetchScalarGridSpec(
            num_scalar_prefetch=1,
            grid=(db, n_pages // pps),
            in_specs=[page_spec(j) for j in range(pps)],
            out_specs=pl.BlockSpec((None, nb, heads, hd), lambda b, n, pt: (b, 0, 0, 0)),
        ),
        compiler_params=_cparams(("parallel", "arbitrary")),
        name="moba_past_block_means",
    )(page_table, *([cache] * pps))


def _moba_select_kernel(q_ref, k_ref, km_ref, qg_ref, kg_ref, qn_ref, kn_ref, idx_ref, *, n_blocks):
    hd = MOBA_HD
    t = q_ref.shape[0]
    lane = lax.broadcasted_iota(jnp.int32, (t, LANES), 1)
    for h in range(MOBA_HEADS):
        cols = slice(h * hd, (h + 1) * hd)
        qn = _head_norm(q_ref[:, cols], qg_ref[...])
        kn = _head_norm(k_ref[:, cols], kg_ref[...])
        qn_ref[:, cols] = qn
        kn_ref[:, cols] = kn
        km = _pad_rows(km_ref[:, h, :], LANES)
        gate = _dot_nt_3pass(qn, km)
        gm = jnp.where(lane < n_blocks, gate, NEG_INF)
        picks = jnp.zeros((t, LANES), jnp.int32)
        for j in range(MOBA_TOPK):
            best = jnp.max(gm, axis=-1, keepdims=True)
            ix = jnp.min(jnp.where(gm == best, lane, LANES), axis=-1, keepdims=True)
            picks = jnp.where(lane == j, ix, picks)
            gm = jnp.where(lane == ix, NEG_INF, gm)
        idx_ref[h] = picks


def _moba_select(p, kmean, q_gain, k_gain, batch, t_rows):
    m, n = p.shape
    w = MOBA_HEADS * MOBA_HD
    nb = kmean.shape[1]
    assert nb >= MOBA_TOPK and nb <= LANES
    return pl.pallas_call(
        functools.partial(_moba_select_kernel, n_blocks=nb),
        out_shape=(jax.ShapeDtypeStruct((m, w), F32),
                   jax.ShapeDtypeStruct((m, w), F32),
                   jax.ShapeDtypeStruct((batch, MOBA_HEADS, t_rows, LANES), jnp.int32)),
        grid=(batch,),
        in_specs=[
            pl.BlockSpec((t_rows, w), lambda b: (b, 0)),
            pl.BlockSpec((t_rows, w), lambda b: (b, 1)),
            pl.BlockSpec((None, nb, MOBA_HEADS, MOBA_HD), lambda b: (b, 0, 0, 0)),
            pl.BlockSpec((1, MOBA_HD), lambda b: (0, 0)),
            pl.BlockSpec((1, MOBA_HD), lambda b: (0, 0)),
        ],
        out_specs=(pl.BlockSpec((t_rows, w), lambda b: (b, 0)),
                   pl.BlockSpec((t_rows, w), lambda b: (b, 0)),
                   pl.BlockSpec((None, MOBA_HEADS, t_rows, LANES), lambda b: (b, 0, 0, 0))),
        compiler_params=_cparams(("parallel",)),
        name="moba_sample_select",
    )(p, p, kmean, q_gain.reshape(1, MOBA_HD), k_gain.reshape(1, MOBA_HD))


def _moba_sample_attn_kernel(idx_ref, pt_ref, ck_ref, cv_ref, qn_ref, kn_ref, vn_ref, g_ref, o_ref,
                             kbuf, vbuf, sem, *, past_len, t_rows):
    step = pl.program_id(0)
    n_steps = pl.num_programs(0)
    h = step % MOBA_HEADS
    hd = MOBA_HD
    blk = MOBA_BLOCK
    page = blk // 2
    n_sel = t_rows * MOBA_TOPK
    scale = hd ** -0.5
    slope = _alibi_slope(h, MOBA_HEADS)

    def for_each_copy(s, slot, fn):
        sb = s // MOBA_HEADS
        sh = s % MOBA_HEADS

        def body(u, carry):
            blk_id = idx_ref[s * n_sel + u]
            for half in range(2):
                pg = pt_ref[sb, 2 * blk_id + half]
                rows = pl.ds(half * page, page)
                fn(pltpu.make_async_copy(ck_ref.at[pg, :, sh, :], kbuf.at[slot, u, rows, :], sem.at[slot]))
                fn(pltpu.make_async_copy(cv_ref.at[pg, :, sh, :], vbuf.at[slot, u, rows, :], sem.at[slot]))
            return carry

        lax.fori_loop(0, n_sel, body, 0)

    slot = step % 2

    @pl.when(step == 0)
    def _():
        for_each_copy(step, slot, lambda cp: cp.start())

    @pl.when(step + 1 < n_steps)
    def _():
        for_each_copy(step + 1, 1 - slot, lambda cp: cp.start())

    for_each_copy(step, slot, lambda cp: cp.wait())

    offs = lax.broadcasted_iota(jnp.int32, (1, blk), 1)
    key_i = lax.broadcasted_iota(jnp.int32, (t_rows, 1), 0)
    kn = kn_ref[...]
    vn = vn_ref[...]

    for qi in range(t_rows):
        qrow = qn_ref[qi:qi + 1, :]
        q8 = jnp.broadcast_to(qrow, (8, hd)).astype(BF16)
        t_pos = past_len + qi
        s_own = jnp.sum(kn * qrow, axis=-1, keepdims=True) * scale - slope * (qi - key_i).astype(F32)
        s_own = jnp.where(key_i <= qi, s_own, NEG_INF)
        m = jnp.max(s_own, axis=0, keepdims=True)
        scores = []
        for j in range(MOBA_TOPK):
            u = qi * MOBA_TOPK + j
            dist = (t_pos - idx_ref[step * n_sel + u] * blk - offs).astype(F32)
            s = _dot_nt(q8, kbuf[slot, u].astype(BF16))[0:1, :] * scale - slope * dist
            scores.append(s)
            m = jnp.maximum(m, jnp.max(s, axis=-1, keepdims=True))
        p_own = jnp.exp(s_own - m)
        denom = jnp.sum(p_own, axis=0, keepdims=True)
        out = jnp.sum(p_own * vn, axis=0, keepdims=True)
        for j, s in enumerate(scores):
            u = qi * MOBA_TOPK + j
            pj = jnp.exp(s - m)
            denom = denom + jnp.sum(pj, axis=-1, keepdims=True)
            out = out + _dot(jnp.broadcast_to(pj, (8, blk)).astype(BF16), vbuf[slot, u].astype(BF16))[0:1, :]
        gate = g_ref[qi:qi + 1, :]
        o_ref[qi:qi + 1, :] = out / denom * _silu(gate)


def _moba_sample_attn(idx, page_table, cache_k, cache_v, qn, kn, vn, gate, batch, t_rows):
    h = MOBA_HEADS
    hd = MOBA_HD
    _, page, heads, _ = cache_k.shape
    n_pages = page_table.shape[1]
    assert MOBA_BLOCK == 2 * page and heads == h
    n_sel = t_rows * MOBA_TOPK
    row_spec = pl.BlockSpec((t_rows, hd), lambda s, ix, pt: (s // h, s % h))
    return pl.pallas_call(
        functools.partial(_moba_sample_attn_kernel, past_len=n_pages * page, t_rows=t_rows),
        out_shape=jax.ShapeDtypeStruct((batch * t_rows, h * hd), F32),
        grid_spec=pltpu.PrefetchScalarGridSpec(
            num_scalar_prefetch=2,
            grid=(batch * h,),
            in_specs=[pl.BlockSpec(memory_space=pl.ANY), pl.BlockSpec(memory_space=pl.ANY),
                      row_spec, row_spec, row_spec, row_spec],
            out_specs=row_spec,
            scratch_shapes=[pltpu.VMEM((2, n_sel, MOBA_BLOCK, hd), F32),
                            pltpu.VMEM((2, n_sel, MOBA_BLOCK, hd), F32),
                            pltpu.SemaphoreType.DMA((2,))],
        ),
        compiler_params=_cparams(("arbitrary",)),
        name="moba_sample_attention",
    )(idx, page_table, cache_k, cache_v, qn, kn, vn, gate)


def _retention_layer(hp, hs, gain, w_in, w_out, slot, state, bp, tp, bs, ts):
    (pp,), (ps,) = _in_proj(hp, hs, gain, w_in, slot, [(w_in.shape[2], BF16)])
    op, st_p = _retention(pp, None, 0, bp, tp, BF16)
    o_s, st_s = _retention(ps, state, slot, bs, ts, F32)
    hp, hs = _out_proj(op, o_s, w_out, slot, hp, hs)
    return hp, hs, st_p, st_s


def _hgrn_layer(hp, hs, gain, w_in, lb_logits, o_gain, w_out, slot, state, layer, bp, tp, bs, ts):
    w = w_in.shape[2] // 4
    (pq, pf, pig), (sq, sf, sig) = _in_proj(hp, hs, gain, w_in, slot, [(w, BF16), (w, F32), (2 * w, BF16)])
    op, st_p = _hgrn(pq, pf, pig, lb_logits, o_gain, None, 0, layer, bp, tp, BF16)
    o_s, st_s = _hgrn(sq, sf, sig, lb_logits, o_gain, state, slot, layer, bs, ts, F32)
    hp, hs = _out_proj(op, o_s, w_out, slot, hp, hs)
    return hp, hs, st_p, st_s


def _moba_layer(hp, hs, gain, w_in, q_gain, k_gain, w_out, slot, cache_k, cache_v, page_table, bp, tp, bs, ts):
    w = MOBA_HEADS * MOBA_HD
    (pqk, pv, pg), (sqk, sv, sg) = _in_proj(hp, hs, gain, w_in, slot, [(2 * w, F32), (w, F32), (w, BF16)])
    kmean = _moba_kmean(cache_k, page_table)
    qn, kn_s, picks = _moba_select(sqk, kmean, q_gain, k_gain, bs, ts)
    idx = picks[..., :MOBA_TOPK].reshape(-1)
    o_s = _moba_sample_attn(idx, page_table, cache_k, cache_v, qn, kn_s, sv, sg, bs, ts)
    op, kn_p = _moba_prompt(pqk, pv, pg, q_gain, k_gain, bp, tp, BF16)
    hp, hs = _out_proj(op, o_s, w_out, slot, hp, hs)
    return hp, hs, kn_p, pv, kn_s, sv


def kernel(x_prompt, x_sample, state_ret, cache_k, cache_v, state_hgrn, page_table, norm_gain, ret_w_in, ret_w_out, moba_w_in, moba_q_gain, moba_k_gain, moba_w_out, hgrn_w_in, hgrn_lb_logits, hgrn_o_gain, hgrn_w_out):
    bp, tp, d = x_prompt.shape
    bs, ts, _ = x_sample.shape
    depth = norm_gain.shape[0]
    n_pool = cache_k.shape[1]
    hp = x_prompt.reshape(bp * tp, d)
    hs = x_sample.reshape(bs * ts, d)
    ck = cache_k.reshape((-1,) + cache_k.shape[2:])
    cv = cache_v.reshape((-1,) + cache_v.shape[2:])
    kp_l, vp_l, ks_l, vs_l, rp_l, rs_l, gp_l, gs_l = [], [], [], [], [], [], [], []
    for layer in range(depth):
        kind, slot = layer % N_MIXERS, layer // N_MIXERS
        gain = norm_gain[layer]
        if kind == 0:
            hp, hs, st_p, st_s = _retention_layer(hp, hs, gain, ret_w_in, ret_w_out, slot, state_ret,
                                                  bp, tp, bs, ts)
            rp_l.append(st_p)
            rs_l.append(st_s)
        elif kind == 1:
            hp, hs, kp, vp, ks_, vs_ = _moba_layer(hp, hs, gain, moba_w_in, moba_q_gain[slot], moba_k_gain[slot],
                                                   moba_w_out, slot, ck, cv, page_table + slot * n_pool,
                                                   bp, tp, bs, ts)
            shp = (MOBA_HEADS, MOBA_HD)
            kp_l.append(kp.reshape(bp, tp, *shp))
            vp_l.append(vp.reshape(bp, tp, *shp))
            ks_l.append(ks_.reshape(bs, ts, *shp))
            vs_l.append(vs_.reshape(bs, ts, *shp))
        else:
            hp, hs, st_p, st_s = _hgrn_layer(hp, hs, gain, hgrn_w_in, hgrn_lb_logits, hgrn_o_gain[slot],
                                             hgrn_w_out, slot, state_hgrn, layer, bp, tp, bs, ts)
            gp_l.append(st_p)
            gs_l.append(st_s)
    return (hp.reshape(bp, tp, d), hs.reshape(bs, ts, d),
            jnp.stack(kp_l), jnp.stack(vp_l), jnp.stack(ks_l), jnp.stack(vs_l),
            jnp.stack(rp_l), jnp.stack(rs_l), jnp.stack(gp_l), jnp.stack(gs_l))
```

```python
import functools
import math

import numpy as np
import jax
import jax.numpy as jnp
from jax import lax
from jax.experimental import pallas as pl
from jax.experimental.pallas import tpu as pltpu

F32 = jnp.float32
BF16 = jnp.bfloat16
EPS = 1e-6
NEG_INF = float("-inf")
LOG2E = 1.4426950408889634
MASK_NEG = -1e30

N_MIXERS = 3
RET_HEADS = 8
RET_CHUNK = 128
MOBA_HEADS = 16
MOBA_HD = 128
MOBA_BLOCK = 256
MOBA_TOPK = 3
HG_HEADS = 16
HG_DK = 128
HG_CHUNK = 128
HG_SUB = 16
HG_FACTORED_SUB = 64
HG_FACTORED_MAX_LOG2_SPAN = 100.0

VMEM_LIMIT_BYTES = 52 * 1024 * 1024
LANES = 128


def _cparams(sem):
    return pltpu.CompilerParams(dimension_semantics=sem, vmem_limit_bytes=VMEM_LIMIT_BYTES)


def _dot(a, b):
    return jnp.dot(a, b, preferred_element_type=F32)


def _dot_nt(a, b):
    return lax.dot_general(a, b, (((1,), (1,)), ((), ())), preferred_element_type=F32)


def _dot_tn(a, b):
    return lax.dot_general(a, b, (((0,), (0,)), ((), ())), preferred_element_type=F32)


def _split_bf16(x, parts):
    pieces = []
    for _ in range(parts):
        hi = x.astype(BF16)
        pieces.append(hi)
        x = x - hi.astype(F32)
    return pieces


def _dot_nt_3pass(a, b):
    a_hi, a_lo = _split_bf16(a, 2)
    b_hi, b_lo = _split_bf16(b, 2)
    return _dot_nt(a_hi, b_hi) + (_dot_nt(a_hi, b_lo) + _dot_nt(a_lo, b_hi))


def _silu(x):
    return x * jax.nn.sigmoid(x)


def _pad_rows(x, rows):
    if x.shape[0] == rows:
        return x
    return jnp.concatenate([x, jnp.zeros((rows - x.shape[0], x.shape[1]), x.dtype)], axis=0)


PROJ_TM, PROJ_TN = 2048, 256
PROJ_SUB_ROWS = 512


def _proj_kernel(*refs, n_i, n_groups, bounds, norm, residual):
    it = iter(refs)
    xp_ref, xs_ref = next(it), next(it)
    gain_ref = next(it) if norm else None
    w_first_ref, w_next_ref = next(it), next(it)
    res_p = [next(it) for _ in range(n_groups)] if residual else None
    res_s = [next(it) for _ in range(n_groups)] if residual else None
    outs_p = [next(it) for _ in range(n_groups)]
    outs_s = [next(it) for _ in range(n_groups)]
    wb_even, wb_odd = next(it), next(it)
    xn_ref = next(it) if norm else None
    i = pl.program_id(0)
    j = pl.program_id(1)
    ms_rows = xs_ref.shape[0]

    if norm:
        def normed(x):
            return (x * lax.rsqrt(jnp.mean(x * x, axis=-1, keepdims=True) + EPS) * gain_ref[...]).astype(BF16)

        @pl.when((j == 0) & (i < n_i))
        def _():
            xn_ref[...] = normed(xp_ref[...])

        @pl.when((j == 0) & (i == n_i))
        def _():
            xn_ref[:ms_rows, :] = normed(xs_ref[...])

    @pl.when((i == 0) & (j == 0))
    def _():
        wb_even[...] = w_first_ref[...].astype(BF16)

    def step(w_cur, w_nxt):
        for g, (j0, j1) in enumerate(bounds):
            in_group = (j >= j0) & (j < j1)

            @pl.when(in_group & (i < n_i))
            def _(g=g):
                w_nxt[...] = w_next_ref[...].astype(BF16)
                lhs_ref = xn_ref if norm else xp_ref
                rows_total = lhs_ref.shape[0]
                sub = min(PROJ_SUB_ROWS, rows_total)
                for r0 in range(0, rows_total, sub):
                    acc = _dot(lhs_ref[r0:r0 + sub, :], w_cur[...])
                    if residual:
                        acc = acc + res_p[g][r0:r0 + sub, :]
                    outs_p[g][r0:r0 + sub, :] = acc.astype(outs_p[g].dtype)

            @pl.when(in_group & (i == n_i))
            def _(g=g):
                w_nxt[...] = w_next_ref[...].astype(BF16)
                acc = _dot(xn_ref[:ms_rows, :] if norm else xs_ref[...].astype(BF16), w_cur[...])
                if residual:
                    acc = acc + res_s[g][...]
                outs_s[g][...] = acc

    pl.when(j % 2 == 0)(functools.partial(step, wb_even, wb_odd))
    pl.when(j % 2 == 1)(functools.partial(step, wb_odd, wb_even))


def _proj(xp, xs, w, slot, groups, gain=None, residual=None, name="proj"):
    mp, d = xp.shape
    ms = xs.shape[0]
    n = w.shape[2]
    tm, tn = min(PROJ_TM, mp), PROJ_TN
    n_i, n_j = mp // tm, n // tn
    assert mp % tm == 0 and ms <= tm and n % tn == 0 and n_j % 2 == 0
    assert sum(c for c, _ in groups) == n and all(c % tn == 0 for c, _ in groups)
    assert residual is None or len(groups) == 1
    norm = gain is not None
    bounds, j0 = [], 0
    for c, _ in groups:
        bounds.append((j0, j0 + c // tn))
        j0 += c // tn

    def prompt_map(j0, j1):
        def imap(i, j):
            jj = jnp.where(i < n_i, j, n_j - 1)
            return (jnp.minimum(i, n_i - 1), jnp.clip(jj - j0, 0, j1 - j0 - 1))
        return imap

    def sample_map(j0, j1):
        def imap(i, j):
            jj = jnp.where(i < n_i, 0, j)
            return (0, jnp.clip(jj - j0, 0, j1 - j0 - 1))
        return imap

    p_specs = [pl.BlockSpec((tm, tn), prompt_map(*b)) for b in bounds]
    s_specs = [pl.BlockSpec((ms, tn), sample_map(*b)) for b in bounds]
    in_specs = [
        pl.BlockSpec((tm, d), lambda i, j: (jnp.minimum(i, n_i - 1), 0), pipeline_mode=pl.Buffered(1)),
        pl.BlockSpec((ms, d), lambda i, j: (0, 0)),
    ]
    args = [xp, xs]
    if norm:
        in_specs.append(pl.BlockSpec((1, d), lambda i, j: (0, 0)))
        args.append(gain.reshape(1, d))
    in_specs += [
        pl.BlockSpec((None, d, tn), lambda i, j: (slot, 0, 0), pipeline_mode=pl.Buffered(1)),
        pl.BlockSpec((None, d, tn), lambda i, j: (slot, 0, (j + 1) % n_j)),
    ]
    args += [w, w]
    if residual is not None:
        in_specs += p_specs + s_specs
        args += list(residual)
    scratch = [pltpu.VMEM((d, tn), BF16), pltpu.VMEM((d, tn), BF16)]
    if norm:
        scratch.append(pltpu.VMEM((tm, d), BF16))
    outs = pl.pallas_call(
        functools.partial(_proj_kernel, n_i=n_i, n_groups=len(groups), bounds=tuple(bounds),
                          norm=norm, residual=residual is not None),
        out_shape=([jax.ShapeDtypeStruct((mp, c), dt) for c, dt in groups]
                   + [jax.ShapeDtypeStruct((ms, c), F32) for c, _ in groups]),
        grid=(n_i + 1, n_j),
        in_specs=in_specs,
        out_specs=p_specs + s_specs,
        scratch_shapes=scratch,
        compiler_params=_cparams(("arbitrary", "arbitrary")),
        name=name,
    )(*args)
    return outs[:len(groups)], outs[len(groups):]


def _in_proj(xp, xs, gain, w, slot, groups):
    return _proj(xp, xs, w, slot, groups, gain=gain, name="norm_in_proj")


def _out_proj(ap, a_s, w, slot, rp, rs):
    (op,), (o_s,) = _proj(ap, a_s, w, slot, [(w.shape[2], F32)], residual=(rp, rs), name="out_proj_residual")
    return op, o_s


def _ret_tables(c, dk):
    scale = dk ** -0.5
    log_g = np.log1p(-np.exp2(-5.0 - np.arange(RET_HEADS, dtype=np.float64)))
    i = np.arange(c, dtype=np.float64)
    rel = i[:, None] - i[None, :]
    intra = np.where(rel >= 0, np.exp(log_g[:, None, None] * np.maximum(rel, 0.0)), 0.0) * scale
    q_dec = np.exp(log_g[:, None] * (i + 1.0))
    k_dec = np.exp(log_g[:, None] * (c - 1.0 - i)) * scale
    c_dec = np.exp(log_g * c)
    big = RET_CHUNK
    intra_p = np.zeros((RET_HEADS, big, big), np.float32)
    intra_p[:, :c, :c] = intra
    qd = np.zeros((RET_HEADS, big, 1), np.float32)
    qd[:, :c, 0] = q_dec
    kd = np.zeros((RET_HEADS, big, 1), np.float32)
    kd[:, :c, 0] = k_dec
    cd = c_dec.astype(np.float32).reshape(RET_HEADS, 1, 1)
    return jnp.asarray(intra_p), jnp.asarray(qd), jnp.asarray(kd), jnp.asarray(cd)


def _ret_kernel(*refs, t_rows, has_s0):
    if has_s0:
        q_ref, k_ref, v_ref, g_ref, intra_ref, qd_ref, kd_ref, cd_ref, s0_ref, o_ref, s_ref, st = refs
        st[...] = s0_ref[...]
    else:
        q_ref, k_ref, v_ref, g_ref, intra_ref, qd_ref, kd_ref, cd_ref, o_ref, s_ref, st = refs
        st[...] = jnp.zeros_like(st)
    c = RET_CHUNK
    t_pad = max(t_rows, c)
    intra = intra_ref[...]
    qd = qd_ref[...]
    kd = kd_ref[...]
    cd = cd_ref[...]

    def chunk(ci, carry):
        if t_rows >= c:
            rows = pl.ds(pl.multiple_of(ci * c, c), c)
        else:
            rows = pl.ds(0, t_rows)
        qc = _pad_rows(q_ref[rows, :], c).astype(BF16)
        kf = _pad_rows(k_ref[rows, :], c).astype(F32)
        vc = _pad_rows(v_ref[rows, :], c).astype(BF16)
        s = st[...]
        att = _dot_nt(qc, kf.astype(BF16)) * intra
        o = _dot(att.astype(BF16), vc) + qd * _dot(qc, s.astype(BF16))
        st[...] = s * cd + _dot_tn((kf * kd).astype(BF16), vc)
        o = o * lax.rsqrt(jnp.mean(o * o, axis=-1, keepdims=True) + EPS)
        gate = g_ref[rows, :].astype(F32)
        o_ref[rows, :] = (o[: gate.shape[0]] * _silu(gate)).astype(o_ref.dtype)
        return carry

    n_chunks = t_pad // c
    lax.fori_loop(0, n_chunks, chunk, 0, unroll=2 if n_chunks % 2 == 0 else 1)
    s_ref[...] = st[...]


def _retention(p, s0, slot, batch, t_rows, out_dtype):
    m, n = p.shape
    h = RET_HEADS
    dk = n // (6 * h)
    dv = 2 * dk
    c_eff = math.gcd(t_rows, RET_CHUNK)
    assert t_rows % c_eff == 0 and (t_rows >= RET_CHUNK or t_rows == c_eff)
    intra, qd, kd, cd = _ret_tables(c_eff, dk)
    has_s0 = s0 is not None
    in_specs = [
        pl.BlockSpec((t_rows, dk), lambda b, hh: (b, hh)),
        pl.BlockSpec((t_rows, dk), lambda b, hh: (b, h + hh)),
        pl.BlockSpec((t_rows, dv), lambda b, hh: (b, h + hh)),
        pl.BlockSpec((t_rows, dv), lambda b, hh: (b, 2 * h + hh)),
        pl.BlockSpec((None, RET_CHUNK, RET_CHUNK), lambda b, hh: (hh, 0, 0)),
        pl.BlockSpec((None, RET_CHUNK, 1), lambda b, hh: (hh, 0, 0)),
        pl.BlockSpec((None, RET_CHUNK, 1), lambda b, hh: (hh, 0, 0)),
        pl.BlockSpec((None, 1, 1), lambda b, hh: (hh, 0, 0)),
    ]
    args = [p, p, p, p, intra, qd, kd, cd]
    if has_s0:
        in_specs.append(pl.BlockSpec((None, None, None, dk, dv), lambda b, hh: (slot, b, hh, 0, 0)))
        args.append(s0)
    return pl.pallas_call(
        functools.partial(_ret_kernel, t_rows=t_rows, has_s0=has_s0),
        out_shape=(jax.ShapeDtypeStruct((m, h * dv), out_dtype),
                   jax.ShapeDtypeStruct((batch, h, dk, dv), F32)),
        grid=(batch, h),
        in_specs=in_specs,
        out_specs=(pl.BlockSpec((t_rows, dv), lambda b, hh: (b, hh)),
                   pl.BlockSpec((None, None, dk, dv), lambda b, hh: (b, hh, 0, 0))),
        scratch_shapes=[pltpu.VMEM((dk, dv), F32)],
        compiler_params=_cparams(("parallel", "parallel")),
        name="retention",
    )(*args)


def _hgrn_kernel(*refs, layer, t_rows, chunk, has_s0):
    n_in = 7 if has_s0 else 6
    q_ref, f_ref, i_ref, g_ref, lbl_ref, og_ref = refs[:6]
    o_ref, s_ref = refs[n_in:n_in + 2]
    scratch_sets = (refs[n_in + 2:n_in + 5], refs[n_in + 5:n_in + 8])
    c = chunk
    r = HG_SUB
    n_sub = c // r
    n_chunks = max(t_rows, c) // c
    dk = q_ref.shape[1]

    logits = lbl_ref[...]
    e = jnp.exp(logits - jnp.max(logits, axis=0, keepdims=True))
    prob = e / jnp.sum(e, axis=0, keepdims=True)
    lrow = lax.broadcasted_iota(jnp.int32, prob.shape, 0)
    in_lb = (lrow >= 1) & (lrow <= layer)
    lb = jnp.sum(jnp.where(in_lb, prob, 0.0), axis=0, keepdims=True)
    one_m_lb = jnp.sum(jnp.where(in_lb, 0.0, prob), axis=0, keepdims=True)

    ri = lax.broadcasted_iota(jnp.int32, (c, LANES), 0)
    cj = lax.broadcasted_iota(jnp.int32, (c, LANES), 1)
    tri_mask = ri >= cj
    tri = jnp.where(tri_mask, 1.0, 0.0).astype(BF16)
    row_c = lax.broadcasted_iota(jnp.int32, (c, dk), 0)
    row_r = lax.broadcasted_iota(jnp.int32, (r, dk), 0)
    og = og_ref[...]

    def one_chunk(ci, s_t, b_scr, k_scr, v_scr, factored):
        if t_rows >= c:
            rows = pl.ds(pl.multiple_of(ci * c, c), c)
        else:
            rows = pl.ds(0, t_rows)
        z = _pad_rows(f_ref[rows, :].astype(F32), c)
        qraw = _pad_rows(q_ref[rows, :].astype(F32), c)
        v = _pad_rows(i_ref[rows, :].astype(F32), c)
        f = lb + one_m_lb * jax.nn.sigmoid(z)
        logf = jnp.log(f)
        kk = one_m_lb * jax.nn.sigmoid(-z)
        if t_rows < c:
            valid = row_c < t_rows
            logf = jnp.where(valid, logf, 0.0)
            kk = jnp.where(valid, kk, 0.0)
        lf_hi, lf_mid, lf_lo = _split_bf16(_pad_rows(logf, LANES), 3)
        b = (_dot(tri, lf_hi) + (_dot(tri, lf_mid) + _dot(tri, lf_lo))) * LOG2E
        qs = _silu(qraw)
        vb = v.astype(BF16)
        o = _dot_nt((qs * jnp.exp2(b)).astype(BF16), s_t.astype(BF16))

        def sub_block_rows(rq, with_diagonal):
            a_rows = []
            for lo in range(0, c, rq):
                hi = lo + rq if with_diagonal else lo
                if hi == 0:
                    a_rows.append(jnp.zeros((rq, c), F32))
                    continue
                b0 = b[lo - 1:lo, :] if lo else jnp.zeros((1, dk), F32)
                qh = qs[lo:lo + rq, :] * jnp.exp2(b[lo:lo + rq, :] - b0)
                kh = kk[:hi, :] * jnp.exp2(b0 - b[:hi, :])
                a_rows.append(_dot_nt(qh.astype(BF16), _pad_rows(kh, c).astype(BF16)))
            return jnp.concatenate(a_rows, axis=0)

        def intra_factored():
            att = jnp.where(tri_mask, sub_block_rows(HG_FACTORED_SUB, True), 0.0)
            return _dot(att.astype(BF16), vb)

        def intra_pairwise():
            b_scr[...] = b
            k_scr[...] = kk
            v_scr[...] = v
            o_diag = []
            for sb in range(n_sub):
                lo = sb * r
                bi = b[lo:lo + r, :]
                qi = qs[lo:lo + r, :]
                acc = jnp.zeros((r, dk), F32)
                for s in range(min(r, t_rows)):
                    bs = b_scr[lo + s:lo + s + 1, :]
                    ks = k_scr[lo + s:lo + s + 1, :]
                    vs = v_scr[lo + s:lo + s + 1, :]
                    d = bi - bs
                    if s > 0:
                        d = jnp.where(row_r >= s, d, NEG_INF)
                    col = jnp.sum(qi * jnp.exp2(d) * ks, axis=-1, keepdims=True)
                    acc = acc + col * vs
                o_diag.append(acc)
            out = jnp.concatenate(o_diag, axis=0)
            if n_sub > 1:
                out = out + _dot(sub_block_rows(r, False).astype(BF16), vb)
            return out

        o = o + (intra_factored() if factored else intra_pairwise())

        b_end = b[c - 1:c, :]
        kd = kk * jnp.exp2(b_end - b)
        s_new = s_t * jnp.exp2(b_end) + _dot_tn(_pad_rows(vb, LANES), _pad_rows(kd.astype(BF16), LANES))

        o = o * lax.rsqrt(jnp.mean(o * o, axis=-1, keepdims=True) + EPS) * og
        gate = g_ref[rows, :].astype(F32)
        o_ref[rows, :] = (o[: gate.shape[0]] * _silu(gate)).astype(o_ref.dtype)
        return s_new

    s_init = refs[6][...].T if has_s0 else jnp.zeros((dk, dk), F32)
    def run(factored):
        if n_chunks == 1:
            return one_chunk(0, s_init, *scratch_sets[0], factored)

        def pair(i, s_t):
            s_t = one_chunk(2 * i, s_t, *scratch_sets[0], factored)
            return one_chunk(2 * i + 1, s_t, *scratch_sets[1], factored)

        return lax.fori_loop(0, n_chunks // 2, pair, s_init)

    if n_sub > 1:
        factored_ok = jnp.min(lb) >= 2.0 ** (-HG_FACTORED_MAX_LOG2_SPAN / HG_FACTORED_SUB)
        s_fin = lax.cond(factored_ok, functools.partial(run, True), functools.partial(run, False))
    else:
        s_fin = run(False)
    s_ref[...] = s_fin.T


def _hgrn(pq, pf, pig, lb_logits, o_gain, s0, slot, layer, batch, t_rows, out_dtype):
    m, n = pq.shape
    h = HG_HEADS
    dk = HG_DK
    dv = n // h
    depth = lb_logits.shape[0]
    chunk = HG_CHUNK if t_rows >= HG_CHUNK else HG_SUB
    n_chunks = max(t_rows, chunk) // chunk
    assert dv == dk == LANES and (t_rows % chunk == 0 or t_rows < chunk) and (n_chunks == 1 or n_chunks % 2 == 0)
    has_s0 = s0 is not None
    in_specs = [
        pl.BlockSpec((t_rows, dk), lambda b, hh: (b, hh)),
        pl.BlockSpec((t_rows, dk), lambda b, hh: (b, hh)),
        pl.BlockSpec((t_rows, dv), lambda b, hh: (b, hh)),
        pl.BlockSpec((t_rows, dv), lambda b, hh: (b, h + hh)),
        pl.BlockSpec((depth, dk), lambda b, hh: (0, hh)),
        pl.BlockSpec((1, dv), lambda b, hh: (0, 0)),
    ]
    args = [pq, pf, pig, pig, lb_logits, o_gain.reshape(1, dv)]
    if has_s0:
        in_specs.append(pl.BlockSpec((None, None, None, dk, dv), lambda b, hh: (slot, b, hh, 0, 0)))
        args.append(s0)
    cshape = (chunk, dk)
    return pl.pallas_call(
        functools.partial(_hgrn_kernel, layer=layer, t_rows=t_rows, chunk=chunk, has_s0=has_s0),
        out_shape=(jax.ShapeDtypeStruct((m, h * dv), out_dtype),
                   jax.ShapeDtypeStruct((batch, h, dk, dv), F32)),
        grid=(batch, h),
        in_specs=in_specs,
        out_specs=(pl.BlockSpec((t_rows, dv), lambda b, hh: (b, hh)),
                   pl.BlockSpec((None, None, dk, dv), lambda b, hh: (b, hh, 0, 0))),
        scratch_shapes=[pltpu.VMEM(cshape, F32)] * 6,
        compiler_params=_cparams(("parallel", "parallel")),
        name="hgrn2",
    )(*args)


def _head_norm(x, gain):
    return x * lax.rsqrt(jnp.mean(x * x, axis=-1, keepdims=True) + EPS) * gain


def _alibi_slope(h_index, n_heads):
    hv = jnp.full((1, 1), h_index + 1, jnp.int32).astype(F32)
    return jnp.exp2(hv * (-8.0 / n_heads))


def _moba_prompt_kernel(q_ref, k_ref, v_ref, g_ref, qg_ref, kg_ref, o_ref, kn_ref,
                        ka_scr, vb_scr, km_scr, *, n_blocks):
    blk = MOBA_BLOCK
    h = pl.program_id(1)
    c = pl.program_id(2)
    hd = q_ref.shape[1]
    t_all = k_ref.shape[0]

    @pl.when(c == 0)
    def _():
        kn = _head_norm(k_ref[...], kg_ref[...])
        kn_ref[...] = kn
        row_blk = lax.broadcasted_iota(jnp.int32, (t_all, hd), 0) // blk
        lane_k = lax.broadcasted_iota(jnp.int32, (t_all, hd), 1)
        one_hot = jnp.where(lane_k == row_blk, 1.0, 0.0).astype(BF16)
        ka_scr[...] = jnp.concatenate([kn.astype(BF16), one_hot], axis=1)
        vb_scr[...] = v_ref[...].astype(BF16)
        means = [jnp.mean(kn[n * blk:(n + 1) * blk, :], axis=0, keepdims=True) for n in range(n_blocks)]
        km_scr[...] = _pad_rows(jnp.concatenate(means, axis=0), LANES)

    qn = _head_norm(q_ref[...], qg_ref[...])
    qs = (qn * (hd ** -0.5 * LOG2E)).astype(BF16)
    slope2 = _alibi_slope(h, MOBA_HEADS) * LOG2E
    key_off = lax.broadcasted_iota(jnp.int32, (1, blk), 1).astype(F32)
    ti = lax.broadcasted_iota(jnp.int32, (blk, blk), 0)
    sj = lax.broadcasted_iota(jnp.int32, (blk, blk), 1)

    def tile(k):
        gated = k > MOBA_TOPK
        if gated:
            gate = _dot_nt_3pass(qn, km_scr[...])
            lane = lax.broadcasted_iota(jnp.int32, gate.shape, 1)
            gm = jnp.where(lane < k, gate, NEG_INF)
            rank = jnp.zeros(gate.shape, jnp.int32)
            for sh in range(1, k):
                other = pltpu.roll(gm, sh, axis=1)
                ahead = (other > gm) | ((other == gm) & (lane >= sh))
                rank = rank + ahead.astype(jnp.int32)
            drop = jnp.where((lane < k) & (rank >= MOBA_TOPK), MASK_NEG, 0.0).astype(BF16)
            q_aug = jnp.concatenate([qs, drop], axis=1)
        scores = []
        for n in range(k):
            keys = ka_scr[n * blk:(n + 1) * blk, :] if gated else ka_scr[n * blk:(n + 1) * blk, :hd]
            bias = slope2 * (key_off + float((n - k) * blk))
            scores.append(_dot_nt(q_aug if gated else qs, keys) + bias)
        s_own = _dot_nt(qs, ka_scr[k * blk:(k + 1) * blk, :hd]) + slope2 * key_off
        scores.append(jnp.where(ti >= sj, s_own, NEG_INF))

        m_el = scores[0]
        for s in scores[1:]:
            m_el = jnp.maximum(m_el, s)
        m = jnp.broadcast_to(jnp.max(m_el, axis=-1, keepdims=True), (blk, blk))
        l_el = jnp.zeros((blk, blk), F32)
        acc = jnp.zeros((blk, hd), F32)
        for n, s in enumerate(scores):
            p = jnp.exp2(s - m)
            l_el = l_el + p
            acc = acc + _dot(p.astype(BF16), vb_scr[n * blk:(n + 1) * blk, :])
        out = acc / jnp.sum(l_el, axis=-1, keepdims=True)
        o_ref[...] = (out * _silu(g_ref[...].astype(F32))).astype(o_ref.dtype)

    for k in range(n_blocks):
        pl.when(c == k)(functools.partial(tile, k))


def _moba_prompt(pqk, pv, pg, q_gain, k_gain, batch, t_rows, out_dtype):
    m, n = pqk.shape
    h = MOBA_HEADS
    hd = MOBA_HD
    assert n == 2 * h * hd and t_rows % MOBA_BLOCK == 0
    nb = t_rows // MOBA_BLOCK
    blk = MOBA_BLOCK
    return pl.pallas_call(
        functools.partial(_moba_prompt_kernel, n_blocks=nb),
        out_shape=(jax.ShapeDtypeStruct((m, h * hd), out_dtype),
                   jax.ShapeDtypeStruct((m, h * hd), F32)),
        grid=(batch, h, nb),
        in_specs=[
            pl.BlockSpec((blk, hd), lambda b, hh, c: (b * nb + c, hh)),
            pl.BlockSpec((t_rows, hd), lambda b, hh, c: (b, h + hh)),
            pl.BlockSpec((t_rows, hd), lambda b, hh, c: (b, hh)),
            pl.BlockSpec((blk, hd), lambda b, hh, c: (b * nb + c, hh)),
            pl.BlockSpec((1, hd), lambda b, hh, c: (0, 0)),
            pl.BlockSpec((1, hd), lambda b, hh, c: (0, 0)),
        ],
        out_specs=(pl.BlockSpec((blk, hd), lambda b, hh, c: (b * nb + c, hh)),
                   pl.BlockSpec((t_rows, hd), lambda b, hh, c: (b, hh))),
        scratch_shapes=[
            pltpu.VMEM((t_rows, 2 * hd), BF16),
            pltpu.VMEM((t_rows, hd), BF16),
            pltpu.VMEM((LANES, hd), F32),
        ],
        compiler_params=_cparams(("parallel", "parallel", "arbitrary")),
        name="moba_prompt",
    )(pqk, pqk, pv, pg, q_gain.reshape(1, hd), k_gain.reshape(1, hd))


KMEAN_PAGES_PER_STEP = 8


def _kmean_kernel(pt_ref, *refs):
    del pt_ref
    page_refs, o_ref = refs[:-1], refs[-1]
    n = pl.program_id(1)
    blocks_per_step = len(page_refs) // 2
    for j in range(blocks_per_step):
        tot = jnp.sum(page_refs[2 * j][...], axis=0) + jnp.sum(page_refs[2 * j + 1][...], axis=0)
        o_ref[n * blocks_per_step + j] = tot * (1.0 / MOBA_BLOCK)


def _moba_kmean(cache, page_table):
    _, page, heads, hd = cache.shape
    db, n_pages = page_table.shape
    pps = math.gcd(n_pages, KMEAN_PAGES_PER_STEP)
    assert MOBA_BLOCK == 2 * page and pps % 2 == 0
    nb = n_pages // 2

    def page_spec(j):
        return pl.BlockSpec((None, page, heads, hd), lambda b, n, pt: (pt[b, pps * n + j], 0, 0, 0))

    return pl.pallas_call(
        _kmean_kernel,
        out_shape=jax.ShapeDtypeStruct((db, nb, heads, hd), F32),
        grid_spec=pltpu.PrefetchScalarGridSpec(
            num_scalar_prefetch=1,
            grid=(db, n_pages // pps),
            in_specs=[page_spec(j) for j in range(pps)],
            out_specs=pl.BlockSpec((None, nb, heads, hd), lambda b, n, pt: (b, 0, 0, 0)),
        ),
        compiler_params=_cparams(("parallel", "arbitrary")),
        name="moba_past_block_means",
    )(page_table, *([cache] * pps))


def _moba_select_kernel(q_ref, k_ref, km_ref, qg_ref, kg_ref, qn_ref, kn_ref, idx_ref, *, n_blocks):
    hd = MOBA_HD
    t = q_ref.shape[0]
    lane = lax.broadcasted_iota(jnp.int32, (t, LANES), 1)
    for h in range(MOBA_HEADS):
        cols = slice(h * hd, (h + 1) * hd)
        qn = _head_norm(q_ref[:, cols], qg_ref[...])
        kn = _head_norm(k_ref[:, cols], kg_ref[...])
        qn_ref[:, cols] = qn
        kn_ref[:, cols] = kn
        km = _pad_rows(km_ref[:, h, :], LANES)
        gate = _dot_nt_3pass(qn, km)
        gm = jnp.where(lane < n_blocks, gate, NEG_INF)
        picks = jnp.zeros((t, LANES), jnp.int32)
        for j in range(MOBA_TOPK):
            best = jnp.max(gm, axis=-1, keepdims=True)
            ix = jnp.min(jnp.where(gm == best, lane, LANES), axis=-1, keepdims=True)
            picks = jnp.where(lane == j, ix, picks)
            gm = jnp.where(lane == ix, NEG_INF, gm)
        idx_ref[h] = picks


def _moba_select(p, kmean, q_gain, k_gain, batch, t_rows):
    m, n = p.shape
    w = MOBA_HEADS * MOBA_HD
    nb = kmean.shape[1]
    assert nb >= MOBA_TOPK and nb <= LANES
    return pl.pallas_call(
        functools.partial(_moba_select_kernel, n_blocks=nb),
        out_shape=(jax.ShapeDtypeStruct((m, w), F32),
                   jax.ShapeDtypeStruct((m, w), F32),
                   jax.ShapeDtypeStruct((batch, MOBA_HEADS, t_rows, LANES), jnp.int32)),
        grid=(batch,),
        in_specs=[
            pl.BlockSpec((t_rows, w), lambda b: (b, 0)),
            pl.BlockSpec((t_rows, w), lambda b: (b, 1)),
            pl.BlockSpec((None, nb, MOBA_HEADS, MOBA_HD), lambda b: (b, 0, 0, 0)),
            pl.BlockSpec((1, MOBA_HD), lambda b: (0, 0)),
            pl.BlockSpec((1, MOBA_HD), lambda b: (0, 0)),
        ],
        out_specs=(pl.BlockSpec((t_rows, w), lambda b: (b, 0)),
                   pl.BlockSpec((t_rows, w), lambda b: (b, 0)),
                   pl.BlockSpec((None, MOBA_HEADS, t_rows, LANES), lambda b: (b, 0, 0, 0))),
        compiler_params=_cparams(("parallel",)),
        name="moba_sample_select",
    )(p, p, kmean, q_gain.reshape(1, MOBA_HD), k_gain.reshape(1, MOBA_HD))


def _moba_sample_attn_kernel(idx_ref, pt_ref, ck_ref, cv_ref, qn_ref, kn_ref, vn_ref, g_ref, o_ref,
                             kbuf, vbuf, sem, *, past_len, t_rows):
    step = pl.program_id(0)
    n_steps = pl.num_programs(0)
    h = step % MOBA_HEADS
    hd = MOBA_HD
    blk = MOBA_BLOCK
    page = blk // 2
    n_sel = t_rows * MOBA_TOPK
    scale = hd ** -0.5
    slope = _alibi_slope(h, MOBA_HEADS)

    def for_each_copy(s, slot, fn):
        sb = s // MOBA_HEADS
        sh = s % MOBA_HEADS

        def body(u, carry):
            blk_id = idx_ref[s * n_sel + u]
            for half in range(2):
                pg = pt_ref[sb, 2 * blk_id + half]
                rows = pl.ds(half * page, page)
                fn(pltpu.make_async_copy(ck_ref.at[pg, :, sh, :], kbuf.at[slot, u, rows, :], sem.at[slot]))
                fn(pltpu.make_async_copy(cv_ref.at[pg, :, sh, :], vbuf.at[slot, u, rows, :], sem.at[slot]))
            return carry

        lax.fori_loop(0, n_sel, body, 0)

    slot = step % 2

    @pl.when(step == 0)
    def _():
        for_each_copy(step, slot, lambda cp: cp.start())

    @pl.when(step + 1 < n_steps)
    def _():
        for_each_copy(step + 1, 1 - slot, lambda cp: cp.start())

    for_each_copy(step, slot, lambda cp: cp.wait())

    offs = lax.broadcasted_iota(jnp.int32, (1, blk), 1)
    key_i = lax.broadcasted_iota(jnp.int32, (t_rows, 1), 0)
    kn = kn_ref[...]
    vn = vn_ref[...]

    for qi in range(t_rows):
        qrow = qn_ref[qi:qi + 1, :]
        q8 = jnp.broadcast_to(qrow, (8, hd)).astype(BF16)
        t_pos = past_len + qi
        s_own = jnp.sum(kn * qrow, axis=-1, keepdims=True) * scale - slope * (qi - key_i).astype(F32)
        s_own = jnp.where(key_i <= qi, s_own, NEG_INF)
        m = jnp.max(s_own, axis=0, keepdims=True)
        scores = []
        for j in range(MOBA_TOPK):
            u = qi * MOBA_TOPK + j
            dist = (t_pos - idx_ref[step * n_sel + u] * blk - offs).astype(F32)
            s = _dot_nt(q8, kbuf[slot, u].astype(BF16))[0:1, :] * scale - slope * dist
            scores.append(s)
            m = jnp.maximum(m, jnp.max(s, axis=-1, keepdims=True))
        p_own = jnp.exp(s_own - m)
        denom = jnp.sum(p_own, axis=0, keepdims=True)
        out = jnp.sum(p_own * vn, axis=0, keepdims=True)
        for j, s in enumerate(scores):
            u = qi * MOBA_TOPK + j
            pj = jnp.exp(s - m)
            denom = denom + jnp.sum(pj, axis=-1, keepdims=True)
            out = out + _dot(jnp.broadcast_to(pj, (8, blk)).astype(BF16), vbuf[slot, u].astype(BF16))[0:1, :]
        gate = g_ref[qi:qi + 1, :]
        o_ref[qi:qi + 1, :] = out / denom * _silu(gate)


def _moba_sample_attn(idx, page_table, cache_k, cache_v, qn, kn, vn, gate, batch, t_rows):
    h = MOBA_HEADS
    hd = MOBA_HD
    _, page, heads, _ = cache_k.shape
    n_pages = page_table.shape[1]
    assert MOBA_BLOCK == 2 * page and heads == h
    n_sel = t_rows * MOBA_TOPK
    row_spec = pl.BlockSpec((t_rows, hd), lambda s, ix, pt: (s // h, s % h))
    return pl.pallas_call(
        functools.partial(_moba_sample_attn_kernel, past_len=n_pages * page, t_rows=t_rows),
        out_shape=jax.ShapeDtypeStruct((batch * t_rows, h * hd), F32),
        grid_spec=pltpu.PrefetchScalarGridSpec(
            num_scalar_prefetch=2,
            grid=(batch * h,),
            in_specs=[pl.BlockSpec(memory_space=pl.ANY), pl.BlockSpec(memory_space=pl.ANY),
                      row_spec, row_spec, row_spec, row_spec],
            out_specs=row_spec,
            scratch_shapes=[pltpu.VMEM((2, n_sel, MOBA_BLOCK, hd), F32),
                            pltpu.VMEM((2, n_sel, MOBA_BLOCK, hd), F32),
                            pltpu.SemaphoreType.DMA((2,))],
        ),
        compiler_params=_cparams(("arbitrary",)),
        name="moba_sample_attention",
    )(idx, page_table, cache_k, cache_v, qn, kn, vn, gate)


def _retention_layer(hp, hs, gain, w_in, w_out, slot, state, bp, tp, bs, ts):
    (pp,), (ps,) = _in_proj(hp, hs, gain, w_in, slot, [(w_in.shape[2], BF16)])
    op, st_p = _retention(pp, None, 0, bp, tp, BF16)
    o_s, st_s = _retention(ps, state, slot, bs, ts, F32)
    hp, hs = _out_proj(op, o_s, w_out, slot, hp, hs)
    return hp, hs, st_p, st_s


def _hgrn_layer(hp, hs, gain, w_in, lb_logits, o_gain, w_out, slot, state, layer, bp, tp, bs, ts):
    w = w_in.shape[2] // 4
    (pq, pf, pig), (sq, sf, sig) = _in_proj(hp, hs, gain, w_in, slot, [(w, BF16), (w, F32), (2 * w, BF16)])
    op, st_p = _hgrn(pq, pf, pig, lb_logits, o_gain, None, 0, layer, bp, tp, BF16)
    o_s, st_s = _hgrn(sq, sf, sig, lb_logits, o_gain, state, slot, layer, bs, ts, F32)
    hp, hs = _out_proj(op, o_s, w_out, slot, hp, hs)
    return hp, hs, st_p, st_s


def _moba_layer(hp, hs, gain, w_in, q_gain, k_gain, w_out, slot, cache_k, cache_v, page_table, bp, tp, bs, ts):
    w = MOBA_HEADS * MOBA_HD
    (pqk, pv, pg), (sqk, sv, sg) = _in_proj(hp, hs, gain, w_in, slot, [(2 * w, F32), (w, F32), (w, BF16)])
    kmean = _moba_kmean(cache_k, page_table)
    qn, kn_s, picks = _moba_select(sqk, kmean, q_gain, k_gain, bs, ts)
    idx = picks[..., :MOBA_TOPK].reshape(-1)
    o_s = _moba_sample_attn(idx, page_table, cache_k, cache_v, qn, kn_s, sv, sg, bs, ts)
    op, kn_p = _moba_prompt(pqk, pv, pg, q_gain, k_gain, bp, tp, BF16)
    hp, hs = _out_proj(op, o_s, w_out, slot, hp, hs)
    return hp, hs, kn_p, pv, kn_s, sv


def kernel(x_prompt, x_sample, state_ret, cache_k, cache_v, state_hgrn, page_table, norm_gain, ret_w_in, ret_w_out, moba_w_in, moba_q_gain, moba_k_gain, moba_w_out, hgrn_w_in, hgrn_lb_logits, hgrn_o_gain, hgrn_w_out):
    bp, tp, d = x_prompt.shape
    bs, ts, _ = x_sample.shape
    depth = norm_gain.shape[0]
    n_pool = cache_k.shape[1]
    hp = x_prompt.reshape(bp * tp, d)
    hs = x_sample.reshape(bs * ts, d)
    ck = cache_k.reshape((-1,) + cache_k.shape[2:])
    cv = cache_v.reshape((-1,) + cache_v.shape[2:])
    kp_l, vp_l, ks_l, vs_l, rp_l, rs_l, gp_l, gs_l = [], [], [], [], [], [], [], []
    for layer in range(depth):
        kind, slot = layer % N_MIXERS, layer // N_MIXERS
        gain = norm_gain[layer]
        if kind == 0:
            hp, hs, st_p, st_s = _retention_layer(hp, hs, gain, ret_w_in, ret_w_out, slot, state_ret,
                                                  bp, tp, bs, ts)
            rp_l.append(st_p)
            rs_l.append(st_s)
        elif kind == 1:
            hp, hs, kp, vp, ks_, vs_ = _moba_layer(hp, hs, gain, moba_w_in, moba_q_gain[slot], moba_k_gain[slot],
                                                   moba_w_out, slot, ck, cv, page_table + slot * n_pool,
                                                   bp, tp, bs, ts)
            kp, vp, hp = lax.optimization_barrier((kp, vp, hp))
            shp = (MOBA_HEADS, MOBA_HD)
            kp_l.append(kp.reshape(bp, tp, *shp))
            vp_l.append(vp.reshape(bp, tp, *shp))
            ks_l.append(ks_.reshape(bs, ts, *shp))
            vs_l.append(vs_.reshape(bs, ts, *shp))
        else:
            hp, hs, st_p, st_s = _hgrn_layer(hp, hs, gain, hgrn_w_in, hgrn_lb_logits, hgrn_o_gain[slot],
                                             hgrn_w_out, slot, state_hgrn, layer, bp, tp, bs, ts)
            gp_l.append(st_p)
            gs_l.append(st_s)
    return (hp.reshape(bp, tp, d), hs.reshape(bs, ts, d),
            jnp.stack(kp_l), jnp.stack(vp_l), jnp.stack(ks_l), jnp.stack(vs_l),
            jnp.stack(rp_l), jnp.stack(rs_l), jnp.stack(gp_l), jnp.stack(gs_l))
```

```python
import functools
import math

import numpy as np
import jax
import jax.numpy as jnp
from jax import lax
from jax.experimental import pallas as pl
from jax.experimental.pallas import tpu as pltpu

F32 = jnp.float32
BF16 = jnp.bfloat16
EPS = 1e-6
NEG_INF = float("-inf")
LOG2E = 1.4426950408889634
MASK_NEG = -1e30

N_MIXERS = 3
RET_HEADS = 8
RET_CHUNK = 128
MOBA_HEADS = 16
MOBA_HD = 128
MOBA_BLOCK = 256
MOBA_TOPK = 3
HG_HEADS = 16
HG_DK = 128
HG_CHUNK = 128
HG_SUB = 16
HG_FACTORED_SUB = 64
HG_FACTORED_MAX_LOG2_SPAN = 100.0

VMEM_LIMIT_BYTES = 52 * 1024 * 1024
LANES = 128


def _cparams(sem):
    return pltpu.CompilerParams(dimension_semantics=sem, vmem_limit_bytes=VMEM_LIMIT_BYTES)


def _dot(a, b):
    return jnp.dot(a, b, preferred_element_type=F32)


def _dot_nt(a, b):
    return lax.dot_general(a, b, (((1,), (1,)), ((), ())), preferred_element_type=F32)


def _dot_tn(a, b):
    return lax.dot_general(a, b, (((0,), (0,)), ((), ())), preferred_element_type=F32)


def _split_bf16(x, parts):
    pieces = []
    for _ in range(parts):
        hi = x.astype(BF16)
        pieces.append(hi)
        x = x - hi.astype(F32)
    return pieces


def _dot_nt_3pass(a, b):
    a_hi, a_lo = _split_bf16(a, 2)
    b_hi, b_lo = _split_bf16(b, 2)
    return _dot_nt(a_hi, b_hi) + (_dot_nt(a_hi, b_lo) + _dot_nt(a_lo, b_hi))


def _silu(x):
    return x * jax.nn.sigmoid(x)


def _pad_rows(x, rows):
    if x.shape[0] == rows:
        return x
    return jnp.concatenate([x, jnp.zeros((rows - x.shape[0], x.shape[1]), x.dtype)], axis=0)


PROJ_TM, PROJ_TN = 2048, 256
PROJ_SUB_ROWS = 512


def _proj_kernel(*refs, n_i, n_groups, bounds, norm, residual):
    it = iter(refs)
    xp_ref, xs_ref = next(it), next(it)
    gain_ref = next(it) if norm else None
    w_first_ref, w_next_ref = next(it), next(it)
    res_p = [next(it) for _ in range(n_groups)] if residual else None
    res_s = [next(it) for _ in range(n_groups)] if residual else None
    outs_p = [next(it) for _ in range(n_groups)]
    outs_s = [next(it) for _ in range(n_groups)]
    wb_even, wb_odd = next(it), next(it)
    xn_ref = next(it) if norm else None
    i = pl.program_id(0)
    j = pl.program_id(1)
    ms_rows = xs_ref.shape[0]

    if norm:
        def normed(x):
            return (x * lax.rsqrt(jnp.mean(x * x, axis=-1, keepdims=True) + EPS) * gain_ref[...]).astype(BF16)

        @pl.when((j == 0) & (i < n_i))
        def _():
            xn_ref[...] = normed(xp_ref[...])

        @pl.when((j == 0) & (i == n_i))
        def _():
            xn_ref[:ms_rows, :] = normed(xs_ref[...])

    @pl.when((i == 0) & (j == 0))
    def _():
        wb_even[...] = w_first_ref[...].astype(BF16)

    def step(w_cur, w_nxt):
        for g, (j0, j1) in enumerate(bounds):
            in_group = (j >= j0) & (j < j1)

            @pl.when(in_group & (i < n_i))
            def _(g=g):
                w_nxt[...] = w_next_ref[...].astype(BF16)
                lhs_ref = xn_ref if norm else xp_ref
                rows_total = lhs_ref.shape[0]
                sub = min(PROJ_SUB_ROWS, rows_total)
                for r0 in range(0, rows_total, sub):
                    acc = _dot(lhs_ref[r0:r0 + sub, :], w_cur[...])
                    if residual:
                        acc = acc + res_p[g][r0:r0 + sub, :]
                    outs_p[g][r0:r0 + sub, :] = acc.astype(outs_p[g].dtype)

            @pl.when(in_group & (i == n_i))
            def _(g=g):
                w_nxt[...] = w_next_ref[...].astype(BF16)
                acc = _dot(xn_ref[:ms_rows, :] if norm else xs_ref[...].astype(BF16), w_cur[...])
                if residual:
                    acc = acc + res_s[g][...]
                outs_s[g][...] = acc

    pl.when(j % 2 == 0)(functools.partial(step, wb_even, wb_odd))
    pl.when(j % 2 == 1)(functools.partial(step, wb_odd, wb_even))


def _proj(xp, xs, w, slot, groups, gain=None, residual=None, name="proj"):
    mp, d = xp.shape
    ms = xs.shape[0]
    n = w.shape[2]
    tm, tn = min(PROJ_TM, mp), PROJ_TN
    n_i, n_j = mp // tm, n // tn
    assert mp % tm == 0 and ms <= tm and n % tn == 0 and n_j % 2 == 0
    assert sum(c for c, _ in groups) == n and all(c % tn == 0 for c, _ in groups)
    assert residual is None or len(groups) == 1
    norm = gain is not None
    bounds, j0 = [], 0
    for c, _ in groups:
        bounds.append((j0, j0 + c // tn))
        j0 += c // tn

    def prompt_map(j0, j1):
        def imap(i, j):
            jj = jnp.where(i < n_i, j, n_j - 1)
            return (jnp.minimum(i, n_i - 1), jnp.clip(jj - j0, 0, j1 - j0 - 1))
        return imap

    def sample_map(j0, j1):
        def imap(i, j):
            jj = jnp.where(i < n_i, 0, j)
            return (0, jnp.clip(jj - j0, 0, j1 - j0 - 1))
        return imap

    p_specs = [pl.BlockSpec((tm, tn), prompt_map(*b)) for b in bounds]
    s_specs = [pl.BlockSpec((ms, tn), sample_map(*b)) for b in bounds]
    in_specs = [
        pl.BlockSpec((tm, d), lambda i, j: (jnp.minimum(i, n_i - 1), 0), pipeline_mode=pl.Buffered(1)),
        pl.BlockSpec((ms, d), lambda i, j: (0, 0)),
    ]
    args = [xp, xs]
    if norm:
        in_specs.append(pl.BlockSpec((1, d), lambda i, j: (0, 0)))
        args.append(gain.reshape(1, d))
    in_specs += [
        pl.BlockSpec((None, d, tn), lambda i, j: (slot, 0, 0), pipeline_mode=pl.Buffered(1)),
        pl.BlockSpec((None, d, tn), lambda i, j: (slot, 0, (j + 1) % n_j)),
    ]
    args += [w, w]
    if residual is not None:
        in_specs += p_specs + s_specs
        args += list(residual)
    scratch = [pltpu.VMEM((d, tn), BF16), pltpu.VMEM((d, tn), BF16)]
    if norm:
        scratch.append(pltpu.VMEM((tm, d), BF16))
    outs = pl.pallas_call(
        functools.partial(_proj_kernel, n_i=n_i, n_groups=len(groups), bounds=tuple(bounds),
                          norm=norm, residual=residual is not None),
        out_shape=([jax.ShapeDtypeStruct((mp, c), dt) for c, dt in groups]
                   + [jax.ShapeDtypeStruct((ms, c), F32) for c, _ in groups]),
        grid=(n_i + 1, n_j),
        in_specs=in_specs,
        out_specs=p_specs + s_specs,
        scratch_shapes=scratch,
        compiler_params=_cparams(("arbitrary", "arbitrary")),
        name=name,
    )(*args)
    return outs[:len(groups)], outs[len(groups):]


def _in_proj(xp, xs, gain, w, slot, groups):
    return _proj(xp, xs, w, slot, groups, gain=gain, name="norm_in_proj")


def _out_proj(ap, a_s, w, slot, rp, rs):
    (op,), (o_s,) = _proj(ap, a_s, w, slot, [(w.shape[2], F32)], residual=(rp, rs), name="out_proj_residual")
    return op, o_s


def _ret_tables(c, dk):
    scale = dk ** -0.5
    log_g = np.log1p(-np.exp2(-5.0 - np.arange(RET_HEADS, dtype=np.float64)))
    i = np.arange(c, dtype=np.float64)
    rel = i[:, None] - i[None, :]
    intra = np.where(rel >= 0, np.exp(log_g[:, None, None] * np.maximum(rel, 0.0)), 0.0) * scale
    q_dec = np.exp(log_g[:, None] * (i + 1.0))
    k_dec = np.exp(log_g[:, None] * (c - 1.0 - i)) * scale
    c_dec = np.exp(log_g * c)
    big = RET_CHUNK
    intra_p = np.zeros((RET_HEADS, big, big), np.float32)
    intra_p[:, :c, :c] = intra
    qd = np.zeros((RET_HEADS, big, 1), np.float32)
    qd[:, :c, 0] = q_dec
    kd = np.zeros((RET_HEADS, big, 1), np.float32)
    kd[:, :c, 0] = k_dec
    cd = c_dec.astype(np.float32).reshape(RET_HEADS, 1, 1)
    return jnp.asarray(intra_p), jnp.asarray(qd), jnp.asarray(kd), jnp.asarray(cd)


def _ret_kernel(*refs, t_rows, has_s0):
    if has_s0:
        q_ref, k_ref, v_ref, g_ref, intra_ref, qd_ref, kd_ref, cd_ref, s0_ref, o_ref, s_ref, st = refs
        st[...] = s0_ref[...]
    else:
        q_ref, k_ref, v_ref, g_ref, intra_ref, qd_ref, kd_ref, cd_ref, o_ref, s_ref, st = refs
        st[...] = jnp.zeros_like(st)
    c = RET_CHUNK
    t_pad = max(t_rows, c)
    intra = intra_ref[...]
    qd = qd_ref[...]
    kd = kd_ref[...]
    cd = cd_ref[...]

    def chunk(ci, carry):
        if t_rows >= c:
            rows = pl.ds(pl.multiple_of(ci * c, c), c)
        else:
            rows = pl.ds(0, t_rows)
        qc = _pad_rows(q_ref[rows, :], c).astype(BF16)
        kf = _pad_rows(k_ref[rows, :], c).astype(F32)
        vc = _pad_rows(v_ref[rows, :], c).astype(BF16)
        s = st[...]
        att = _dot_nt(qc, kf.astype(BF16)) * intra
        o = _dot(att.astype(BF16), vc) + qd * _dot(qc, s.astype(BF16))
        st[...] = s * cd + _dot_tn((kf * kd).astype(BF16), vc)
        o = o * lax.rsqrt(jnp.mean(o * o, axis=-1, keepdims=True) + EPS)
        gate = g_ref[rows, :].astype(F32)
        o_ref[rows, :] = (o[: gate.shape[0]] * _silu(gate)).astype(o_ref.dtype)
        return carry

    n_chunks = t_pad // c
    lax.fori_loop(0, n_chunks, chunk, 0, unroll=2 if n_chunks % 2 == 0 else 1)
    s_ref[...] = st[...]


def _retention(p, s0, slot, batch, t_rows, out_dtype):
    m, n = p.shape
    h = RET_HEADS
    dk = n // (6 * h)
    dv = 2 * dk
    c_eff = math.gcd(t_rows, RET_CHUNK)
    assert t_rows % c_eff == 0 and (t_rows >= RET_CHUNK or t_rows == c_eff)
    intra, qd, kd, cd = _ret_tables(c_eff, dk)
    has_s0 = s0 is not None
    in_specs = [
        pl.BlockSpec((t_rows, dk), lambda b, hh: (b, hh)),
        pl.BlockSpec((t_rows, dk), lambda b, hh: (b, h + hh)),
        pl.BlockSpec((t_rows, dv), lambda b, hh: (b, h + hh)),
        pl.BlockSpec((t_rows, dv), lambda b, hh: (b, 2 * h + hh)),
        pl.BlockSpec((None, RET_CHUNK, RET_CHUNK), lambda b, hh: (hh, 0, 0)),
        pl.BlockSpec((None, RET_CHUNK, 1), lambda b, hh: (hh, 0, 0)),
        pl.BlockSpec((None, RET_CHUNK, 1), lambda b, hh: (hh, 0, 0)),
        pl.BlockSpec((None, 1, 1), lambda b, hh: (hh, 0, 0)),
    ]
    args = [p, p, p, p, intra, qd, kd, cd]
    if has_s0:
        in_specs.append(pl.BlockSpec((None, None, None, dk, dv), lambda b, hh: (slot, b, hh, 0, 0)))
        args.append(s0)
    return pl.pallas_call(
        functools.partial(_ret_kernel, t_rows=t_rows, has_s0=has_s0),
        out_shape=(jax.ShapeDtypeStruct((m, h * dv), out_dtype),
                   jax.ShapeDtypeStruct((batch, h, dk, dv), F32)),
        grid=(batch, h),
        in_specs=in_specs,
        out_specs=(pl.BlockSpec((t_rows, dv), lambda b, hh: (b, hh)),
                   pl.BlockSpec((None, None, dk, dv), lambda b, hh: (b, hh, 0, 0))),
        scratch_shapes=[pltpu.VMEM((dk, dv), F32)],
        compiler_params=_cparams(("parallel", "parallel")),
        name="retention",
    )(*args)


def _hgrn_kernel(*refs, layer, t_rows, chunk, hps, has_s0):
    n_in = 7 if has_s0 else 6
    q_ref, f_ref, i_ref, g_ref, lbl_ref, og_ref = refs[:6]
    o_ref, s_ref = refs[n_in:n_in + 2]
    scratch = refs[n_in + 2:]
    scratch_sets = [scratch[3 * n:3 * n + 3] for n in range(len(scratch) // 3)]
    c = chunk
    r = HG_SUB
    n_sub = c // r
    n_chunks = max(t_rows, c) // c
    dk = q_ref.shape[1] // hps

    logits = lbl_ref[...]
    e = jnp.exp(logits - jnp.max(logits, axis=0, keepdims=True))
    prob = e / jnp.sum(e, axis=0, keepdims=True)
    lrow = lax.broadcasted_iota(jnp.int32, prob.shape, 0)
    in_lb = (lrow >= 1) & (lrow <= layer)
    lb_all = jnp.sum(jnp.where(in_lb, prob, 0.0), axis=0, keepdims=True)
    one_m_lb_all = jnp.sum(jnp.where(in_lb, 0.0, prob), axis=0, keepdims=True)

    ri = lax.broadcasted_iota(jnp.int32, (c, LANES), 0)
    cj = lax.broadcasted_iota(jnp.int32, (c, LANES), 1)
    tri_mask = ri >= cj
    tri = jnp.where(tri_mask, 1.0, 0.0).astype(BF16)
    row_c = lax.broadcasted_iota(jnp.int32, (c, dk), 0)
    row_r = lax.broadcasted_iota(jnp.int32, (r, dk), 0)
    og = og_ref[...]

    def one_chunk(ci, hj, s_t, b_scr, k_scr, v_scr, factored):
        if t_rows >= c:
            rows = pl.ds(pl.multiple_of(ci * c, c), c)
        else:
            rows = pl.ds(0, t_rows)
        cols = slice(hj * dk, (hj + 1) * dk)
        lb = lb_all[:, cols]
        one_m_lb = one_m_lb_all[:, cols]
        z = _pad_rows(f_ref[rows, cols].astype(F32), c)
        qraw = _pad_rows(q_ref[rows, cols].astype(F32), c)
        v = _pad_rows(i_ref[rows, cols].astype(F32), c)
        f = lb + one_m_lb * jax.nn.sigmoid(z)
        logf = jnp.log(f)
        kk = one_m_lb * jax.nn.sigmoid(-z)
        if t_rows < c:
            valid = row_c < t_rows
            logf = jnp.where(valid, logf, 0.0)
            kk = jnp.where(valid, kk, 0.0)
        lf_hi, lf_mid, lf_lo = _split_bf16(_pad_rows(logf, LANES), 3)
        b = (_dot(tri, lf_hi) + (_dot(tri, lf_mid) + _dot(tri, lf_lo))) * LOG2E
        qs = _silu(qraw)
        vb = v.astype(BF16)
        o = _dot_nt((qs * jnp.exp2(b)).astype(BF16), s_t.astype(BF16))

        def sub_block_rows(rq, with_diagonal):
            a_rows = []
            for lo in range(0, c, rq):
                hi = lo + rq if with_diagonal else lo
                if hi == 0:
                    a_rows.append(jnp.zeros((rq, c), F32))
                    continue
                b0 = b[lo - 1:lo, :] if lo else jnp.zeros((1, dk), F32)
                qh = qs[lo:lo + rq, :] * jnp.exp2(b[lo:lo + rq, :] - b0)
                kh = kk[:hi, :] * jnp.exp2(b0 - b[:hi, :])
                a_rows.append(_dot_nt(qh.astype(BF16), _pad_rows(kh, c).astype(BF16)))
            return jnp.concatenate(a_rows, axis=0)

        def intra_factored():
            att = jnp.where(tri_mask, sub_block_rows(HG_FACTORED_SUB, True), 0.0)
            return _dot(att.astype(BF16), vb)

        def intra_pairwise():
            b_scr[...] = b
            k_scr[...] = kk
            v_scr[...] = v
            o_diag = []
            for sb in range(n_sub):
                lo = sb * r
                bi = b[lo:lo + r, :]
                qi = qs[lo:lo + r, :]
                acc = jnp.zeros((r, dk), F32)
                for s in range(min(r, t_rows)):
                    bs = b_scr[lo + s:lo + s + 1, :]
                    ks = k_scr[lo + s:lo + s + 1, :]
                    vs = v_scr[lo + s:lo + s + 1, :]
                    d = bi - bs
                    if s > 0:
                        d = jnp.where(row_r >= s, d, NEG_INF)
                    col = jnp.sum(qi * jnp.exp2(d) * ks, axis=-1, keepdims=True)
                    acc = acc + col * vs
                o_diag.append(acc)
            out = jnp.concatenate(o_diag, axis=0)
            if n_sub > 1:
                out = out + _dot(sub_block_rows(r, False).astype(BF16), vb)
            return out

        o = o + (intra_factored() if factored else intra_pairwise())

        b_end = b[c - 1:c, :]
        kd = kk * jnp.exp2(b_end - b)
        s_new = s_t * jnp.exp2(b_end) + _dot_tn(_pad_rows(vb, LANES), _pad_rows(kd.astype(BF16), LANES))

        o = o * lax.rsqrt(jnp.mean(o * o, axis=-1, keepdims=True) + EPS) * og
        gate = g_ref[rows, cols].astype(F32)
        o_ref[rows, cols] = (o[: gate.shape[0]] * _silu(gate)).astype(o_ref.dtype)
        return s_new

    s_init = tuple(refs[6][hj].T if has_s0 else jnp.zeros((dk, dk), F32) for hj in range(hps))

    def run(factored):
        if n_chunks == 1:
            return tuple(one_chunk(0, hj, s_init[hj], *scratch_sets[hj], factored) for hj in range(hps))

        def pair(i, states):
            states = tuple(one_chunk(2 * i, hj, states[hj], *scratch_sets[2 * hj], factored)
                           for hj in range(hps))
            return tuple(one_chunk(2 * i + 1, hj, states[hj], *scratch_sets[2 * hj + 1], factored)
                         for hj in range(hps))

        return lax.fori_loop(0, n_chunks // 2, pair, s_init)

    if n_sub > 1:
        factored_ok = jnp.min(lb_all) >= 2.0 ** (-HG_FACTORED_MAX_LOG2_SPAN / HG_FACTORED_SUB)
        s_fin = lax.cond(factored_ok, functools.partial(run, True), functools.partial(run, False))
    else:
        s_fin = run(False)
    for hj in range(hps):
        s_ref[hj] = s_fin[hj].T


def _hgrn(pq, pf, pig, lb_logits, o_gain, s0, slot, layer, batch, t_rows, out_dtype):
    m, n = pq.shape
    h = HG_HEADS
    dk = HG_DK
    dv = n // h
    depth = lb_logits.shape[0]
    chunk = HG_CHUNK if t_rows >= HG_CHUNK else HG_SUB
    n_chunks = max(t_rows, chunk) // chunk
    assert dv == dk == LANES and (t_rows % chunk == 0 or t_rows < chunk) and (n_chunks == 1 or n_chunks % 2 == 0)
    has_s0 = s0 is not None
    hps = 2 if n_chunks > 1 else h
    n_hg = h // hps
    wk = hps * dk
    in_specs = [
        pl.BlockSpec((t_rows, wk), lambda b, hg: (b, hg)),
        pl.BlockSpec((t_rows, wk), lambda b, hg: (b, hg)),
        pl.BlockSpec((t_rows, wk), lambda b, hg: (b, hg)),
        pl.BlockSpec((t_rows, wk), lambda b, hg: (b, n_hg + hg)),
        pl.BlockSpec((depth, wk), lambda b, hg: (0, hg)),
        pl.BlockSpec((1, dv), lambda b, hg: (0, 0)),
    ]
    args = [pq, pf, pig, pig, lb_logits, o_gain.reshape(1, dv)]
    if has_s0:
        in_specs.append(pl.BlockSpec((None, None, hps, dk, dv), lambda b, hg: (slot, b, hg, 0, 0)))
        args.append(s0)
    n_sets = hps if n_chunks == 1 else 2 * hps
    return pl.pallas_call(
        functools.partial(_hgrn_kernel, layer=layer, t_rows=t_rows, chunk=chunk, hps=hps, has_s0=has_s0),
        out_shape=(jax.ShapeDtypeStruct((m, h * dv), out_dtype),
                   jax.ShapeDtypeStruct((batch, h, dk, dv), F32)),
        grid=(batch, n_hg),
        in_specs=in_specs,
        out_specs=(pl.BlockSpec((t_rows, wk), lambda b, hg: (b, hg)),
                   pl.BlockSpec((None, hps, dk, dv), lambda b, hg: (b, hg, 0, 0))),
        scratch_shapes=[pltpu.VMEM((chunk, dk), F32)] * (3 * n_sets),
        compiler_params=_cparams(("parallel", "parallel")),
        name="hgrn2",
    )(*args)


def _head_norm(x, gain):
    return x * lax.rsqrt(jnp.mean(x * x, axis=-1, keepdims=True) + EPS) * gain


def _alibi_slope(h_index, n_heads):
    hv = jnp.full((1, 1), h_index + 1, jnp.int32).astype(F32)
    return jnp.exp2(hv * (-8.0 / n_heads))


def _moba_prompt_kernel(q_ref, k_ref, v_ref, g_ref, qg_ref, kg_ref, o_ref, kn_ref,
                        ka_scr, vb_scr, km_scr, *, n_blocks):
    blk = MOBA_BLOCK
    h = pl.program_id(1)
    hd = q_ref.shape[1]
    t_all = k_ref.shape[0]

    kn = _head_norm(k_ref[...], kg_ref[...])
    kn_ref[...] = kn
    row_blk = lax.broadcasted_iota(jnp.int32, (t_all, hd), 0) // blk
    lane_k = lax.broadcasted_iota(jnp.int32, (t_all, hd), 1)
    one_hot = jnp.where(lane_k == row_blk, 1.0, 0.0).astype(BF16)
    ka_scr[...] = jnp.concatenate([kn.astype(BF16), one_hot], axis=1)
    vb_scr[...] = v_ref[...].astype(BF16)
    means = [jnp.mean(kn[n * blk:(n + 1) * blk, :], axis=0, keepdims=True) for n in range(n_blocks)]
    km_scr[...] = _pad_rows(jnp.concatenate(means, axis=0), LANES)

    slope2 = _alibi_slope(h, MOBA_HEADS) * LOG2E
    key_off = lax.broadcasted_iota(jnp.int32, (1, blk), 1).astype(F32)
    ti = lax.broadcasted_iota(jnp.int32, (blk, blk), 0)
    sj = lax.broadcasted_iota(jnp.int32, (blk, blk), 1)

    def tile(k):
        gated = k > MOBA_TOPK
        rows = slice(k * blk, (k + 1) * blk)
        qn = _head_norm(q_ref[rows, :], qg_ref[...])
        qs = (qn * (hd ** -0.5 * LOG2E)).astype(BF16)
        if gated:
            gate = _dot_nt_3pass(qn, km_scr[...])
            lane = lax.broadcasted_iota(jnp.int32, gate.shape, 1)
            gm = jnp.where(lane < k, gate, NEG_INF)
            rank = jnp.zeros(gate.shape, jnp.int32)
            for sh in range(1, k):
                other = pltpu.roll(gm, sh, axis=1)
                ahead = (other > gm) | ((other == gm) & (lane >= sh))
                rank = rank + ahead.astype(jnp.int32)
            drop = jnp.where((lane < k) & (rank >= MOBA_TOPK), MASK_NEG, 0.0).astype(BF16)
            q_aug = jnp.concatenate([qs, drop], axis=1)
        scores = []
        for n in range(k):
            keys = ka_scr[n * blk:(n + 1) * blk, :] if gated else ka_scr[n * blk:(n + 1) * blk, :hd]
            bias = slope2 * (key_off + float((n - k) * blk))
            scores.append(_dot_nt(q_aug if gated else qs, keys) + bias)
        s_own = _dot_nt(qs, ka_scr[k * blk:(k + 1) * blk, :hd]) + slope2 * key_off
        scores.append(jnp.where(ti >= sj, s_own, NEG_INF))

        m_el = scores[0]
        for s in scores[1:]:
            m_el = jnp.maximum(m_el, s)
        m = jnp.broadcast_to(jnp.max(m_el, axis=-1, keepdims=True), (blk, blk))
        l_el = jnp.zeros((blk, blk), F32)
        acc = jnp.zeros((blk, hd), F32)
        for n, s in enumerate(scores):
            p = jnp.exp2(s - m)
            l_el = l_el + p
            acc = acc + _dot(p.astype(BF16), vb_scr[n * blk:(n + 1) * blk, :])
        out = acc / jnp.sum(l_el, axis=-1, keepdims=True)
        o_ref[rows, :] = (out * _silu(g_ref[rows, :].astype(F32))).astype(o_ref.dtype)

    for k in range(n_blocks):
        tile(k)


def _moba_prompt(pqk, pv, pg, q_gain, k_gain, batch, t_rows, out_dtype):
    m, n = pqk.shape
    h = MOBA_HEADS
    hd = MOBA_HD
    assert n == 2 * h * hd and t_rows % MOBA_BLOCK == 0
    nb = t_rows // MOBA_BLOCK
    blk = MOBA_BLOCK
    return pl.pallas_call(
        functools.partial(_moba_prompt_kernel, n_blocks=nb),
        out_shape=(jax.ShapeDtypeStruct((m, h * hd), out_dtype),
                   jax.ShapeDtypeStruct((m, h * hd), F32)),
        grid=(batch, h),
        in_specs=[
            pl.BlockSpec((t_rows, hd), lambda b, hh: (b, hh)),
            pl.BlockSpec((t_rows, hd), lambda b, hh: (b, h + hh)),
            pl.BlockSpec((t_rows, hd), lambda b, hh: (b, hh)),
            pl.BlockSpec((t_rows, hd), lambda b, hh: (b, hh)),
            pl.BlockSpec((1, hd), lambda b, hh: (0, 0)),
            pl.BlockSpec((1, hd), lambda b, hh: (0, 0)),
        ],
        out_specs=(pl.BlockSpec((t_rows, hd), lambda b, hh: (b, hh)),
                   pl.BlockSpec((t_rows, hd), lambda b, hh: (b, hh))),
        scratch_shapes=[
            pltpu.VMEM((t_rows, 2 * hd), BF16),
            pltpu.VMEM((t_rows, hd), BF16),
            pltpu.VMEM((LANES, hd), F32),
        ],
        compiler_params=_cparams(("parallel", "parallel")),
        name="moba_prompt",
    )(pqk, pqk, pv, pg, q_gain.reshape(1, hd), k_gain.reshape(1, hd))


KMEAN_PAGES_PER_STEP = 8


def _kmean_kernel(pt_ref, *refs):
    del pt_ref
    page_refs, o_ref = refs[:-1], refs[-1]
    n = pl.program_id(1)
    blocks_per_step = len(page_refs) // 2
    for j in range(blocks_per_step):
        tot = jnp.sum(page_refs[2 * j][...], axis=0) + jnp.sum(page_refs[2 * j + 1][...], axis=0)
        o_ref[n * blocks_per_step + j] = tot * (1.0 / MOBA_BLOCK)


def _moba_kmean(cache, page_table):
    _, page, heads, hd = cache.shape
    db, n_pages = page_table.shape
    pps = math.gcd(n_pages, KMEAN_PAGES_PER_STEP)
    assert MOBA_BLOCK == 2 * page and pps % 2 == 0
    nb = n_pages // 2

    def page_spec(j):
        return pl.BlockSpec((None, page, heads, hd), lambda b, n, pt: (pt[b, pps * n + j], 0, 0, 0))

    return pl.pallas_call(
        _kmean_kernel,
        out_shape=jax.ShapeDtypeStruct((db, nb, heads, hd), F32),
        grid_spec=pltpu.PrefetchScalarGridSpec(
            num_scalar_prefetch=1,
            grid=(db, n_pages // pps),
            in_specs=[page_spec(j) for j in range(pps)],
            out_specs=pl.BlockSpec((None, nb, heads, hd), lambda b, n, pt: (b, 0, 0, 0)),
        ),
        compiler_params=_cparams(("parallel", "arbitrary")),
        name="moba_past_block_means",
    )(page_table, *([cache] * pps))


def _moba_select_kernel(q_ref, k_ref, km_ref, qg_ref, kg_ref, qn_ref, kn_ref, idx_ref, *, n_blocks):
    hd = MOBA_HD
    t = q_ref.shape[0]
    lane = lax.broadcasted_iota(jnp.int32, (t, LANES), 1)
    for h in range(MOBA_HEADS):
        cols = slice(h * hd, (h + 1) * hd)
        qn = _head_norm(q_ref[:, cols], qg_ref[...])
        kn = _head_norm(k_ref[:, cols], kg_ref[...])
        qn_ref[:, cols] = qn
        kn_ref[:, cols] = kn
        km = _pad_rows(km_ref[:, h, :], LANES)
        gate = _dot_nt_3pass(qn, km)
        gm = jnp.where(lane < n_blocks, gate, NEG_INF)
        picks = jnp.zeros((t, LANES), jnp.int32)
        for j in range(MOBA_TOPK):
            best = jnp.max(gm, axis=-1, keepdims=True)
            ix = jnp.min(jnp.where(gm == best, lane, LANES), axis=-1, keepdims=True)
            picks = jnp.where(lane == j, ix, picks)
            gm = jnp.where(lane == ix, NEG_INF, gm)
        idx_ref[h] = picks


def _moba_select(p, kmean, q_gain, k_gain, batch, t_rows):
    m, n = p.shape
    w = MOBA_HEADS * MOBA_HD
    nb = kmean.shape[1]
    assert nb >= MOBA_TOPK and nb <= LANES
    return pl.pallas_call(
        functools.partial(_moba_select_kernel, n_blocks=nb),
        out_shape=(jax.ShapeDtypeStruct((m, w), F32),
                   jax.ShapeDtypeStruct((m, w), F32),
                   jax.ShapeDtypeStruct((batch, MOBA_HEADS, t_rows, LANES), jnp.int32)),
        grid=(batch,),
        in_specs=[
            pl.BlockSpec((t_rows, w), lambda b: (b, 0)),
            pl.BlockSpec((t_rows, w), lambda b: (b, 1)),
            pl.BlockSpec((None, nb, MOBA_HEADS, MOBA_HD), lambda b: (b, 0, 0, 0)),
            pl.BlockSpec((1, MOBA_HD), lambda b: (0, 0)),
            pl.BlockSpec((1, MOBA_HD), lambda b: (0, 0)),
        ],
        out_specs=(pl.BlockSpec((t_rows, w), lambda b: (b, 0)),
                   pl.BlockSpec((t_rows, w), lambda b: (b, 0)),
                   pl.BlockSpec((None, MOBA_HEADS, t_rows, LANES), lambda b: (b, 0, 0, 0))),
        compiler_params=_cparams(("parallel",)),
        name="moba_sample_select",
    )(p, p, kmean, q_gain.reshape(1, MOBA_HD), k_gain.reshape(1, MOBA_HD))


def _moba_sample_attn_kernel(idx_ref, pt_ref, ck_ref, cv_ref, qn_ref, kn_ref, vn_ref, g_ref, o_ref,
                             kbuf, vbuf, sem, *, past_len, t_rows):
    step = pl.program_id(0)
    n_steps = pl.num_programs(0)
    h = step % MOBA_HEADS
    hd = MOBA_HD
    blk = MOBA_BLOCK
    page = blk // 2
    n_sel = t_rows * MOBA_TOPK
    scale = hd ** -0.5
    slope = _alibi_slope(h, MOBA_HEADS)

    def for_each_copy(s, slot, fn):
        sb = s // MOBA_HEADS
        sh = s % MOBA_HEADS

        def body(u, carry):
            blk_id = idx_ref[s * n_sel + u]
            for half in range(2):
                pg = pt_ref[sb, 2 * blk_id + half]
                rows = pl.ds(half * page, page)
                fn(pltpu.make_async_copy(ck_ref.at[pg, :, sh, :], kbuf.at[slot, u, rows, :], sem.at[slot]))
                fn(pltpu.make_async_copy(cv_ref.at[pg, :, sh, :], vbuf.at[slot, u, rows, :], sem.at[slot]))
            return carry

        lax.fori_loop(0, n_sel, body, 0)

    slot = step % 2

    @pl.when(step == 0)
    def _():
        for_each_copy(step, slot, lambda cp: cp.start())

    @pl.when(step + 1 < n_steps)
    def _():
        for_each_copy(step + 1, 1 - slot, lambda cp: cp.start())

    for_each_copy(step, slot, lambda cp: cp.wait())

    offs = lax.broadcasted_iota(jnp.int32, (1, blk), 1)
    key_i = lax.broadcasted_iota(jnp.int32, (t_rows, 1), 0)
    kn = kn_ref[...]
    vn = vn_ref[...]

    for qi in range(t_rows):
        qrow = qn_ref[qi:qi + 1, :]
        q8 = jnp.broadcast_to(qrow, (8, hd)).astype(BF16)
        t_pos = past_len + qi
        s_own = jnp.sum(kn * qrow, axis=-1, keepdims=True) * scale - slope * (qi - key_i).astype(F32)
        s_own = jnp.where(key_i <= qi, s_own, NEG_INF)
        m = jnp.max(s_own, axis=0, keepdims=True)
        scores = []
        for j in range(MOBA_TOPK):
            u = qi * MOBA_TOPK + j
            dist = (t_pos - idx_ref[step * n_sel + u] * blk - offs).astype(F32)
            s = _dot_nt(q8, kbuf[slot, u].astype(BF16))[0:1, :] * scale - slope * dist
            scores.append(s)
            m = jnp.maximum(m, jnp.max(s, axis=-1, keepdims=True))
        p_own = jnp.exp(s_own - m)
        denom = jnp.sum(p_own, axis=0, keepdims=True)
        out = jnp.sum(p_own * vn, axis=0, keepdims=True)
        for j, s in enumerate(scores):
            u = qi * MOBA_TOPK + j
            pj = jnp.exp(s - m)
            denom = denom + jnp.sum(pj, axis=-1, keepdims=True)
            out = out + _dot(jnp.broadcast_to(pj, (8, blk)).astype(BF16), vbuf[slot, u].astype(BF16))[0:1, :]
        gate = g_ref[qi:qi + 1, :]
        o_ref[qi:qi + 1, :] = out / denom * _silu(gate)


def _moba_sample_attn(idx, page_table, cache_k, cache_v, qn, kn, vn, gate, batch, t_rows):
    h = MOBA_HEADS
    hd = MOBA_HD
    _, page, heads, _ = cache_k.shape
    n_pages = page_table.shape[1]
    assert MOBA_BLOCK == 2 * page and heads == h
    n_sel = t_rows * MOBA_TOPK
    row_spec = pl.BlockSpec((t_rows, hd), lambda s, ix, pt: (s // h, s % h))
    return pl.pallas_call(
        functools.partial(_moba_sample_attn_kernel, past_len=n_pages * page, t_rows=t_rows),
        out_shape=jax.ShapeDtypeStruct((batch * t_rows, h * hd), F32),
        grid_spec=pltpu.PrefetchScalarGridSpec(
            num_scalar_prefetch=2,
            grid=(batch * h,),
            in_specs=[pl.BlockSpec(memory_space=pl.ANY), pl.BlockSpec(memory_space=pl.ANY),
                      row_spec, row_spec, row_spec, row_spec],
            out_specs=row_spec,
            scratch_shapes=[pltpu.VMEM((2, n_sel, MOBA_BLOCK, hd), F32),
                            pltpu.VMEM((2, n_sel, MOBA_BLOCK, hd), F32),
                            pltpu.SemaphoreType.DMA((2,))],
        ),
        compiler_params=_cparams(("arbitrary",)),
        name="moba_sample_attention",
    )(idx, page_table, cache_k, cache_v, qn, kn, vn, gate)


def _retention_layer(hp, hs, gain, w_in, w_out, slot, state, bp, tp, bs, ts):
    (pp,), (ps,) = _in_proj(hp, hs, gain, w_in, slot, [(w_in.shape[2], BF16)])
    op, st_p = _retention(pp, None, 0, bp, tp, BF16)
    o_s, st_s = _retention(ps, state, slot, bs, ts, F32)
    hp, hs = _out_proj(op, o_s, w_out, slot, hp, hs)
    return hp, hs, st_p, st_s


def _hgrn_layer(hp, hs, gain, w_in, lb_logits, o_gain, w_out, slot, state, layer, bp, tp, bs, ts):
    w = w_in.shape[2] // 4
    (pq, pf, pig), (sq, sf, sig) = _in_proj(hp, hs, gain, w_in, slot, [(w, BF16), (w, F32), (2 * w, BF16)])
    op, st_p = _hgrn(pq, pf, pig, lb_logits, o_gain, None, 0, layer, bp, tp, BF16)
    o_s, st_s = _hgrn(sq, sf, sig, lb_logits, o_gain, state, slot, layer, bs, ts, F32)
    hp, hs = _out_proj(op, o_s, w_out, slot, hp, hs)
    return hp, hs, st_p, st_s


def _moba_layer(hp, hs, gain, w_in, q_gain, k_gain, w_out, slot, cache_k, cache_v, page_table, bp, tp, bs, ts):
    w = MOBA_HEADS * MOBA_HD
    (pqk, pv, pg), (sqk, sv, sg) = _in_proj(hp, hs, gain, w_in, slot, [(2 * w, F32), (w, F32), (w, BF16)])
    kmean = _moba_kmean(cache_k, page_table)
    qn, kn_s, picks = _moba_select(sqk, kmean, q_gain, k_gain, bs, ts)
    idx = picks[..., :MOBA_TOPK].reshape(-1)
    o_s = _moba_sample_attn(idx, page_table, cache_k, cache_v, qn, kn_s, sv, sg, bs, ts)
    op, kn_p = _moba_prompt(pqk, pv, pg, q_gain, k_gain, bp, tp, BF16)
    hp, hs = _out_proj(op, o_s, w_out, slot, hp, hs)
    return hp, hs, kn_p, pv, kn_s, sv


def kernel(x_prompt, x_sample, state_ret, cache_k, cache_v, state_hgrn, page_table, norm_gain, ret_w_in, ret_w_out, moba_w_in, moba_q_gain, moba_k_gain, moba_w_out, hgrn_w_in, hgrn_lb_logits, hgrn_o_gain, hgrn_w_out):
    bp, tp, d = x_prompt.shape
    bs, ts, _ = x_sample.shape
    depth = norm_gain.shape[0]
    n_pool = cache_k.shape[1]
    hp = x_prompt.reshape(bp * tp, d)
    hs = x_sample.reshape(bs * ts, d)
    ck = cache_k.reshape((-1,) + cache_k.shape[2:])
    cv = cache_v.reshape((-1,) + cache_v.shape[2:])
    kp_l, vp_l, ks_l, vs_l, rp_l, rs_l, gp_l, gs_l = [], [], [], [], [], [], [], []
    for layer in range(depth):
        kind, slot = layer % N_MIXERS, layer // N_MIXERS
        gain = norm_gain[layer]
        if kind == 0:
            hp, hs, st_p, st_s = _retention_layer(hp, hs, gain, ret_w_in, ret_w_out, slot, state_ret,
                                                  bp, tp, bs, ts)
            rp_l.append(st_p)
            rs_l.append(st_s)
        elif kind == 1:
            hp, hs, kp, vp, ks_, vs_ = _moba_layer(hp, hs, gain, moba_w_in, moba_q_gain[slot], moba_k_gain[slot],
                                                   moba_w_out, slot, ck, cv, page_table + slot * n_pool,
                                                   bp, tp, bs, ts)
            kp, vp, hp = lax.optimization_barrier((kp, vp, hp))
            shp = (MOBA_HEADS, MOBA_HD)
            kp_l.append(kp.reshape(bp, tp, *shp))
            vp_l.append(vp.reshape(bp, tp, *shp))
            ks_l.append(ks_.reshape(bs, ts, *shp))
            vs_l.append(vs_.reshape(bs, ts, *shp))
        else:
            hp, hs, st_p, st_s = _hgrn_layer(hp, hs, gain, hgrn_w_in, hgrn_lb_logits, hgrn_o_gain[slot],
                                             hgrn_w_out, slot, state_hgrn, layer, bp, tp, bs, ts)
            gp_l.append(st_p)
            gs_l.append(st_s)
    return (hp.reshape(bp, tp, d), hs.reshape(bs, ts, d),
            jnp.stack(kp_l), jnp.stack(vp_l), jnp.stack(ks_l), jnp.stack(vs_l),
            jnp.stack(rp_l), jnp.stack(rs_l), jnp.stack(gp_l), jnp.stack(gs_l))
```

```python
import functools
import math

import numpy as np
import jax
import jax.numpy as jnp
from jax import lax
from jax.experimental import pallas as pl
from jax.experimental.pallas import tpu as pltpu

F32 = jnp.float32
BF16 = jnp.bfloat16
EPS = 1e-6
NEG_INF = float("-inf")
LOG2E = 1.4426950408889634
MASK_NEG = -1e30

N_MIXERS = 3
RET_HEADS = 8
RET_CHUNK = 128
MOBA_HEADS = 16
MOBA_HD = 128
MOBA_BLOCK = 256
MOBA_TOPK = 3
HG_HEADS = 16
HG_DK = 128
HG_CHUNK = 128
HG_SUB = 16
HG_FACTORED_SUB = 64
HG_FACTORED_MAX_LOG2_SPAN = 100.0

VMEM_LIMIT_BYTES = 52 * 1024 * 1024
LANES = 128


def _cparams(sem):
    return pltpu.CompilerParams(dimension_semantics=sem, vmem_limit_bytes=VMEM_LIMIT_BYTES)


def _dot(a, b):
    return jnp.dot(a, b, preferred_element_type=F32)


def _dot_nt(a, b):
    return lax.dot_general(a, b, (((1,), (1,)), ((), ())), preferred_element_type=F32)


def _dot_tn(a, b):
    return lax.dot_general(a, b, (((0,), (0,)), ((), ())), preferred_element_type=F32)


def _split_bf16(x, parts):
    pieces = []
    for _ in range(parts):
        hi = x.astype(BF16)
        pieces.append(hi)
        x = x - hi.astype(F32)
    return pieces


def _dot_nt_3pass(a, b):
    a_hi, a_lo = _split_bf16(a, 2)
    b_hi, b_lo = _split_bf16(b, 2)
    return _dot_nt(a_hi, b_hi) + (_dot_nt(a_hi, b_lo) + _dot_nt(a_lo, b_hi))


def _silu(x):
    return x * jax.nn.sigmoid(x)


def _pad_rows(x, rows):
    if x.shape[0] == rows:
        return x
    return jnp.concatenate([x, jnp.zeros((rows - x.shape[0], x.shape[1]), x.dtype)], axis=0)


PROJ_TM, PROJ_TN = 1024, 512
PROJ_SUB_ROWS = 512


def _proj_kernel(*refs, n_i, n_groups, bounds, norm, residual):
    it = iter(refs)
    xp_ref, xs_ref = next(it), next(it)
    gain_ref = next(it) if norm else None
    w_first_ref, w_next_ref = next(it), next(it)
    res_p = [next(it) for _ in range(n_groups)] if residual else None
    res_s = [next(it) for _ in range(n_groups)] if residual else None
    outs_p = [next(it) for _ in range(n_groups)]
    outs_s = [next(it) for _ in range(n_groups)]
    wb_even, wb_odd = next(it), next(it)
    xn_ref = next(it) if norm else None
    i = pl.program_id(0)
    j = pl.program_id(1)
    ms_rows = xs_ref.shape[0]

    if norm:
        def normed(x):
            return (x * lax.rsqrt(jnp.mean(x * x, axis=-1, keepdims=True) + EPS) * gain_ref[...]).astype(BF16)

        @pl.when((j == 0) & (i < n_i))
        def _():
            xn_ref[...] = normed(xp_ref[...])

        @pl.when((j == 0) & (i == n_i))
        def _():
            xn_ref[:ms_rows, :] = normed(xs_ref[...])

    @pl.when((i == 0) & (j == 0))
    def _():
        wb_even[...] = w_first_ref[...].astype(BF16)

    def step(w_cur, w_nxt):
        for g, (j0, j1) in enumerate(bounds):
            in_group = (j >= j0) & (j < j1)

            @pl.when(in_group & (i < n_i))
            def _(g=g):
                w_nxt[...] = w_next_ref[...].astype(BF16)
                lhs_ref = xn_ref if norm else xp_ref
                rows_total = lhs_ref.shape[0]
                sub = min(PROJ_SUB_ROWS, rows_total)
                for r0 in range(0, rows_total, sub):
                    acc = _dot(lhs_ref[r0:r0 + sub, :], w_cur[...])
                    if residual:
                        acc = acc + res_p[g][r0:r0 + sub, :]
                    outs_p[g][r0:r0 + sub, :] = acc.astype(outs_p[g].dtype)

            @pl.when(in_group & (i == n_i))
            def _(g=g):
                w_nxt[...] = w_next_ref[...].astype(BF16)
                acc = _dot(xn_ref[:ms_rows, :] if norm else xs_ref[...].astype(BF16), w_cur[...])
                if residual:
                    acc = acc + res_s[g][...]
                outs_s[g][...] = acc

    pl.when(j % 2 == 0)(functools.partial(step, wb_even, wb_odd))
    pl.when(j % 2 == 1)(functools.partial(step, wb_odd, wb_even))


def _proj(xp, xs, w, slot, groups, gain=None, residual=None, name="proj"):
    mp, d = xp.shape
    ms = xs.shape[0]
    n = w.shape[2]
    tm, tn = min(PROJ_TM, mp), PROJ_TN
    n_i, n_j = mp // tm, n // tn
    assert mp % tm == 0 and ms <= tm and n % tn == 0 and n_j % 2 == 0
    assert sum(c for c, _ in groups) == n and all(c % tn == 0 for c, _ in groups)
    assert residual is None or len(groups) == 1
    norm = gain is not None
    bounds, j0 = [], 0
    for c, _ in groups:
        bounds.append((j0, j0 + c // tn))
        j0 += c // tn

    def prompt_map(j0, j1):
        def imap(i, j):
            jj = jnp.where(i < n_i, j, n_j - 1)
            return (jnp.minimum(i, n_i - 1), jnp.clip(jj - j0, 0, j1 - j0 - 1))
        return imap

    def sample_map(j0, j1):
        def imap(i, j):
            jj = jnp.where(i < n_i, 0, j)
            return (0, jnp.clip(jj - j0, 0, j1 - j0 - 1))
        return imap

    p_specs = [pl.BlockSpec((tm, tn), prompt_map(*b)) for b in bounds]
    s_specs = [pl.BlockSpec((ms, tn), sample_map(*b)) for b in bounds]
    in_specs = [
        pl.BlockSpec((tm, d), lambda i, j: (jnp.minimum(i, n_i - 1), 0), pipeline_mode=pl.Buffered(1)),
        pl.BlockSpec((ms, d), lambda i, j: (0, 0)),
    ]
    args = [xp, xs]
    if norm:
        in_specs.append(pl.BlockSpec((1, d), lambda i, j: (0, 0)))
        args.append(gain.reshape(1, d))
    in_specs += [
        pl.BlockSpec((None, d, tn), lambda i, j: (slot, 0, 0), pipeline_mode=pl.Buffered(1)),
        pl.BlockSpec((None, d, tn), lambda i, j: (slot, 0, (j + 1) % n_j)),
    ]
    args += [w, w]
    if residual is not None:
        in_specs += p_specs + s_specs
        args += list(residual)
    scratch = [pltpu.VMEM((d, tn), BF16), pltpu.VMEM((d, tn), BF16)]
    if norm:
        scratch.append(pltpu.VMEM((tm, d), BF16))
    outs = pl.pallas_call(
        functools.partial(_proj_kernel, n_i=n_i, n_groups=len(groups), bounds=tuple(bounds),
                          norm=norm, residual=residual is not None),
        out_shape=([jax.ShapeDtypeStruct((mp, c), dt) for c, dt in groups]
                   + [jax.ShapeDtypeStruct((ms, c), F32) for c, _ in groups]),
        grid=(n_i + 1, n_j),
        in_specs=in_specs,
        out_specs=p_specs + s_specs,
        scratch_shapes=scratch,
        compiler_params=_cparams(("arbitrary", "arbitrary")),
        name=name,
    )(*args)
    return outs[:len(groups)], outs[len(groups):]


def _in_proj(xp, xs, gain, w, slot, groups):
    return _proj(xp, xs, w, slot, groups, gain=gain, name="norm_in_proj")


def _out_proj(ap, a_s, w, slot, rp, rs):
    (op,), (o_s,) = _proj(ap, a_s, w, slot, [(w.shape[2], F32)], residual=(rp, rs), name="out_proj_residual")
    return op, o_s


def _ret_tables(c, dk):
    scale = dk ** -0.5
    log_g = np.log1p(-np.exp2(-5.0 - np.arange(RET_HEADS, dtype=np.float64)))
    i = np.arange(c, dtype=np.float64)
    rel = i[:, None] - i[None, :]
    intra = np.where(rel >= 0, np.exp(log_g[:, None, None] * np.maximum(rel, 0.0)), 0.0) * scale
    q_dec = np.exp(log_g[:, None] * (i + 1.0))
    k_dec = np.exp(log_g[:, None] * (c - 1.0 - i)) * scale
    c_dec = np.exp(log_g * c)
    big = RET_CHUNK
    intra_p = np.zeros((RET_HEADS, big, big), np.float32)
    intra_p[:, :c, :c] = intra
    qd = np.zeros((RET_HEADS, big, 1), np.float32)
    qd[:, :c, 0] = q_dec
    kd = np.zeros((RET_HEADS, big, 1), np.float32)
    kd[:, :c, 0] = k_dec
    cd = c_dec.astype(np.float32).reshape(RET_HEADS, 1, 1)
    return jnp.asarray(intra_p), jnp.asarray(qd), jnp.asarray(kd), jnp.asarray(cd)


def _ret_kernel(*refs, t_rows, hps, has_s0):
    if has_s0:
        q_ref, k_ref, v_ref, g_ref, intra_ref, qd_ref, kd_ref, cd_ref, s0_ref, o_ref, s_ref, st = refs
        st[...] = s0_ref[...]
    else:
        q_ref, k_ref, v_ref, g_ref, intra_ref, qd_ref, kd_ref, cd_ref, o_ref, s_ref, st = refs
        st[...] = jnp.zeros_like(st)
    c = RET_CHUNK
    t_pad = max(t_rows, c)
    dk = q_ref.shape[1] // hps
    dv = v_ref.shape[1] // hps

    def chunk(ci, carry):
        if t_rows >= c:
            rows = pl.ds(pl.multiple_of(ci * c, c), c)
        else:
            rows = pl.ds(0, t_rows)
        for hj in range(hps):
            kcols = slice(hj * dk, (hj + 1) * dk)
            vcols = slice(hj * dv, (hj + 1) * dv)
            qc = _pad_rows(q_ref[rows, kcols], c).astype(BF16)
            kf = _pad_rows(k_ref[rows, kcols], c).astype(F32)
            vc = _pad_rows(v_ref[rows, vcols], c).astype(BF16)
            s = st[hj]
            att = _dot_nt(qc, kf.astype(BF16)) * intra_ref[hj]
            o = _dot(att.astype(BF16), vc) + qd_ref[hj] * _dot(qc, s.astype(BF16))
            st[hj] = s * cd_ref[hj] + _dot_tn((kf * kd_ref[hj]).astype(BF16), vc)
            o = o * lax.rsqrt(jnp.mean(o * o, axis=-1, keepdims=True) + EPS)
            gate = g_ref[rows, vcols].astype(F32)
            o_ref[rows, vcols] = (o[: gate.shape[0]] * _silu(gate)).astype(o_ref.dtype)
        return carry

    n_chunks = t_pad // c
    lax.fori_loop(0, n_chunks, chunk, 0, unroll=2 if n_chunks % 2 == 0 else 1)
    s_ref[...] = st[...]


def _retention(p, s0, slot, batch, t_rows, out_dtype):
    m, n = p.shape
    h = RET_HEADS
    dk = n // (6 * h)
    dv = 2 * dk
    c_eff = math.gcd(t_rows, RET_CHUNK)
    assert t_rows % c_eff == 0 and (t_rows >= RET_CHUNK or t_rows == c_eff)
    intra, qd, kd, cd = _ret_tables(c_eff, dk)
    has_s0 = s0 is not None
    hps = 2 if t_rows > RET_CHUNK else h
    n_hg = h // hps
    in_specs = [
        pl.BlockSpec((t_rows, hps * dk), lambda b, hg: (b, hg)),
        pl.BlockSpec((t_rows, hps * dk), lambda b, hg: (b, n_hg + hg)),
        pl.BlockSpec((t_rows, hps * dv), lambda b, hg: (b, n_hg + hg)),
        pl.BlockSpec((t_rows, hps * dv), lambda b, hg: (b, 2 * n_hg + hg)),
        pl.BlockSpec((hps, RET_CHUNK, RET_CHUNK), lambda b, hg: (hg, 0, 0)),
        pl.BlockSpec((hps, RET_CHUNK, 1), lambda b, hg: (hg, 0, 0)),
        pl.BlockSpec((hps, RET_CHUNK, 1), lambda b, hg: (hg, 0, 0)),
        pl.BlockSpec((hps, 1, 1), lambda b, hg: (hg, 0, 0)),
    ]
    args = [p, p, p, p, intra, qd, kd, cd]
    if has_s0:
        in_specs.append(pl.BlockSpec((None, None, hps, dk, dv), lambda b, hg: (slot, b, hg, 0, 0)))
        args.append(s0)
    return pl.pallas_call(
        functools.partial(_ret_kernel, t_rows=t_rows, hps=hps, has_s0=has_s0),
        out_shape=(jax.ShapeDtypeStruct((m, h * dv), out_dtype),
                   jax.ShapeDtypeStruct((batch, h, dk, dv), F32)),
        grid=(batch, n_hg),
        in_specs=in_specs,
        out_specs=(pl.BlockSpec((t_rows, hps * dv), lambda b, hg: (b, hg)),
                   pl.BlockSpec((None, hps, dk, dv), lambda b, hg: (b, hg, 0, 0))),
        scratch_shapes=[pltpu.VMEM((hps, dk, dv), F32)],
        compiler_params=_cparams(("parallel", "parallel")),
        name="retention",
    )(*args)


def _hgrn_kernel(*refs, layer, t_rows, chunk, hps, has_s0):
    n_in = 7 if has_s0 else 6
    q_ref, f_ref, i_ref, g_ref, lbl_ref, og_ref = refs[:6]
    o_ref, s_ref = refs[n_in:n_in + 2]
    scratch = refs[n_in + 2:]
    scratch_sets = [scratch[3 * n:3 * n + 3] for n in range(len(scratch) // 3)]
    c = chunk
    r = HG_SUB
    n_sub = c // r
    n_chunks = max(t_rows, c) // c
    dk = q_ref.shape[1] // hps

    logits = lbl_ref[...]
    e = jnp.exp(logits - jnp.max(logits, axis=0, keepdims=True))
    prob = e / jnp.sum(e, axis=0, keepdims=True)
    lrow = lax.broadcasted_iota(jnp.int32, prob.shape, 0)
    in_lb = (lrow >= 1) & (lrow <= layer)
    lb_all = jnp.sum(jnp.where(in_lb, prob, 0.0), axis=0, keepdims=True)
    one_m_lb_all = jnp.sum(jnp.where(in_lb, 0.0, prob), axis=0, keepdims=True)

    ri = lax.broadcasted_iota(jnp.int32, (c, LANES), 0)
    cj = lax.broadcasted_iota(jnp.int32, (c, LANES), 1)
    tri_mask = ri >= cj
    tri = jnp.where(tri_mask, 1.0, 0.0).astype(BF16)
    row_c = lax.broadcasted_iota(jnp.int32, (c, dk), 0)
    row_r = lax.broadcasted_iota(jnp.int32, (r, dk), 0)
    og = og_ref[...]

    def one_chunk(ci, hj, s_t, b_scr, k_scr, v_scr, factored):
        if t_rows >= c:
            rows = pl.ds(pl.multiple_of(ci * c, c), c)
        else:
            rows = pl.ds(0, t_rows)
        cols = slice(hj * dk, (hj + 1) * dk)
        lb = lb_all[:, cols]
        one_m_lb = one_m_lb_all[:, cols]
        z = _pad_rows(f_ref[rows, cols].astype(F32), c)
        qraw = _pad_rows(q_ref[rows, cols].astype(F32), c)
        v = _pad_rows(i_ref[rows, cols].astype(F32), c)
        f = lb + one_m_lb * jax.nn.sigmoid(z)
        logf = jnp.log(f)
        kk = one_m_lb * jax.nn.sigmoid(-z)
        if t_rows < c:
            valid = row_c < t_rows
            logf = jnp.where(valid, logf, 0.0)
            kk = jnp.where(valid, kk, 0.0)
        lf_hi, lf_mid, lf_lo = _split_bf16(_pad_rows(logf, LANES), 3)
        b = (_dot(tri, lf_hi) + (_dot(tri, lf_mid) + _dot(tri, lf_lo))) * LOG2E
        qs = _silu(qraw)
        vb = v.astype(BF16)
        o = _dot_nt((qs * jnp.exp2(b)).astype(BF16), s_t.astype(BF16))

        def sub_block_rows(rq, with_diagonal):
            a_rows = []
            for lo in range(0, c, rq):
                hi = lo + rq if with_diagonal else lo
                if hi == 0:
                    a_rows.append(jnp.zeros((rq, c), F32))
                    continue
                b0 = b[lo - 1:lo, :] if lo else jnp.zeros((1, dk), F32)
                qh = qs[lo:lo + rq, :] * jnp.exp2(b[lo:lo + rq, :] - b0)
                kh = kk[:hi, :] * jnp.exp2(b0 - b[:hi, :])
                a_rows.append(_dot_nt(qh.astype(BF16), _pad_rows(kh, c).astype(BF16)))
            return jnp.concatenate(a_rows, axis=0)

        def intra_factored():
            att = jnp.where(tri_mask, sub_block_rows(HG_FACTORED_SUB, True), 0.0)
            return _dot(att.astype(BF16), vb)

        def intra_pairwise():
            b_scr[...] = b
            k_scr[...] = kk
            v_scr[...] = v
            o_diag = []
            for sb in range(n_sub):
                lo = sb * r
                bi = b[lo:lo + r, :]
                qi = qs[lo:lo + r, :]
                acc = jnp.zeros((r, dk), F32)
                for s in range(min(r, t_rows)):
                    bs = b_scr[lo + s:lo + s + 1, :]
                    ks = k_scr[lo + s:lo + s + 1, :]
                    vs = v_scr[lo + s:lo + s + 1, :]
                    d = bi - bs
                    if s > 0:
                        d = jnp.where(row_r >= s, d, NEG_INF)
                    col = jnp.sum(qi * jnp.exp2(d) * ks, axis=-1, keepdims=True)
                    acc = acc + col * vs
                o_diag.append(acc)
            out = jnp.concatenate(o_diag, axis=0)
            if n_sub > 1:
                out = out + _dot(sub_block_rows(r, False).astype(BF16), vb)
            return out

        o = o + (intra_factored() if factored else intra_pairwise())

        b_end = b[c - 1:c, :]
        kd = kk * jnp.exp2(b_end - b)
        s_new = s_t * jnp.exp2(b_end) + _dot_tn(_pad_rows(vb, LANES), _pad_rows(kd.astype(BF16), LANES))

        o = o * lax.rsqrt(jnp.mean(o * o, axis=-1, keepdims=True) + EPS) * og
        gate = g_ref[rows, cols].astype(F32)
        o_ref[rows, cols] = (o[: gate.shape[0]] * _silu(gate)).astype(o_ref.dtype)
        return s_new

    s_init = tuple(refs[6][hj].T if has_s0 else jnp.zeros((dk, dk), F32) for hj in range(hps))

    def run(factored):
        if n_chunks == 1:
            return tuple(one_chunk(0, hj, s_init[hj], *scratch_sets[hj], factored) for hj in range(hps))

        def pair(i, states):
            states = tuple(one_chunk(2 * i, hj, states[hj], *scratch_sets[2 * hj], factored)
                           for hj in range(hps))
            return tuple(one_chunk(2 * i + 1, hj, states[hj], *scratch_sets[2 * hj + 1], factored)
                         for hj in range(hps))

        return lax.fori_loop(0, n_chunks // 2, pair, s_init)

    if n_sub > 1:
        factored_ok = jnp.min(lb_all) >= 2.0 ** (-HG_FACTORED_MAX_LOG2_SPAN / HG_FACTORED_SUB)
        s_fin = lax.cond(factored_ok, functools.partial(run, True), functools.partial(run, False))
    else:
        s_fin = run(False)
    for hj in range(hps):
        s_ref[hj] = s_fin[hj].T


def _hgrn(pq, pf, pig, lb_logits, o_gain, s0, slot, layer, batch, t_rows, out_dtype):
    m, n = pq.shape
    h = HG_HEADS
    dk = HG_DK
    dv = n // h
    depth = lb_logits.shape[0]
    chunk = HG_CHUNK if t_rows >= HG_CHUNK else HG_SUB
    n_chunks = max(t_rows, chunk) // chunk
    assert dv == dk == LANES and (t_rows % chunk == 0 or t_rows < chunk) and (n_chunks == 1 or n_chunks % 2 == 0)
    has_s0 = s0 is not None
    hps = 2 if n_chunks > 1 else h
    n_hg = h // hps
    wk = hps * dk
    in_specs = [
        pl.BlockSpec((t_rows, wk), lambda b, hg: (b, hg)),
        pl.BlockSpec((t_rows, wk), lambda b, hg: (b, hg)),
        pl.BlockSpec((t_rows, wk), lambda b, hg: (b, hg)),
        pl.BlockSpec((t_rows, wk), lambda b, hg: (b, n_hg + hg)),
        pl.BlockSpec((depth, wk), lambda b, hg: (0, hg)),
        pl.BlockSpec((1, dv), lambda b, hg: (0, 0)),
    ]
    args = [pq, pf, pig, pig, lb_logits, o_gain.reshape(1, dv)]
    if has_s0:
        in_specs.append(pl.BlockSpec((None, None, hps, dk, dv), lambda b, hg: (slot, b, hg, 0, 0)))
        args.append(s0)
    n_sets = hps if n_chunks == 1 else 2 * hps
    return pl.pallas_call(
        functools.partial(_hgrn_kernel, layer=layer, t_rows=t_rows, chunk=chunk, hps=hps, has_s0=has_s0),
        out_shape=(jax.ShapeDtypeStruct((m, h * dv), out_dtype),
                   jax.ShapeDtypeStruct((batch, h, dk, dv), F32)),
        grid=(batch, n_hg),
        in_specs=in_specs,
        out_specs=(pl.BlockSpec((t_rows, wk), lambda b, hg: (b, hg)),
                   pl.BlockSpec((None, hps, dk, dv), lambda b, hg: (b, hg, 0, 0))),
        scratch_shapes=[pltpu.VMEM((chunk, dk), F32)] * (3 * n_sets),
        compiler_params=_cparams(("parallel", "parallel")),
        name="hgrn2",
    )(*args)


def _head_norm(x, gain):
    return x * lax.rsqrt(jnp.mean(x * x, axis=-1, keepdims=True) + EPS) * gain


def _alibi_slope(h_index, n_heads):
    hv = jnp.full((1, 1), h_index + 1, jnp.int32).astype(F32)
    return jnp.exp2(hv * (-8.0 / n_heads))


def _moba_prompt_kernel(q_ref, k_ref, v_ref, g_ref, qg_ref, kg_ref, o_ref, kn_ref,
                        ka_scr, vb_scr, km_scr, *, n_blocks):
    blk = MOBA_BLOCK
    h = pl.program_id(1)
    hd = q_ref.shape[1]
    t_all = k_ref.shape[0]

    kn = _head_norm(k_ref[...], kg_ref[...])
    kn_ref[...] = kn
    row_blk = lax.broadcasted_iota(jnp.int32, (t_all, hd), 0) // blk
    lane_k = lax.broadcasted_iota(jnp.int32, (t_all, hd), 1)
    one_hot = jnp.where(lane_k == row_blk, 1.0, 0.0).astype(BF16)
    ka_scr[...] = jnp.concatenate([kn.astype(BF16), one_hot], axis=1)
    vb_scr[...] = v_ref[...].astype(BF16)
    means = [jnp.mean(kn[n * blk:(n + 1) * blk, :], axis=0, keepdims=True) for n in range(n_blocks)]
    km_scr[...] = _pad_rows(jnp.concatenate(means, axis=0), LANES)

    slope2 = _alibi_slope(h, MOBA_HEADS) * LOG2E
    key_off = lax.broadcasted_iota(jnp.int32, (1, blk), 1).astype(F32)
    ti = lax.broadcasted_iota(jnp.int32, (blk, blk), 0)
    sj = lax.broadcasted_iota(jnp.int32, (blk, blk), 1)

    def tile(k):
        gated = k > MOBA_TOPK
        rows = slice(k * blk, (k + 1) * blk)
        qn = _head_norm(q_ref[rows, :], qg_ref[...])
        qs = (qn * (hd ** -0.5 * LOG2E)).astype(BF16)
        if gated:
            gate = _dot_nt_3pass(qn, km_scr[...])
            lane = lax.broadcasted_iota(jnp.int32, gate.shape, 1)
            gm = jnp.where(lane < k, gate, NEG_INF)
            rank = jnp.zeros(gate.shape, jnp.int32)
            for sh in range(1, k):
                other = pltpu.roll(gm, sh, axis=1)
                ahead = (other > gm) | ((other == gm) & (lane >= sh))
                rank = rank + ahead.astype(jnp.int32)
            drop = jnp.where((lane < k) & (rank >= MOBA_TOPK), MASK_NEG, 0.0).astype(BF16)
            q_aug = jnp.concatenate([qs, drop], axis=1)
        scores = []
        for n in range(k):
            keys = ka_scr[n * blk:(n + 1) * blk, :] if gated else ka_scr[n * blk:(n + 1) * blk, :hd]
            bias = slope2 * (key_off + float((n - k) * blk))
            scores.append(_dot_nt(q_aug if gated else qs, keys) + bias)
        s_own = _dot_nt(qs, ka_scr[k * blk:(k + 1) * blk, :hd]) + slope2 * key_off
        scores.append(jnp.where(ti >= sj, s_own, NEG_INF))

        m_el = scores[0]
        for s in scores[1:]:
            m_el = jnp.maximum(m_el, s)
        m = jnp.broadcast_to(jnp.max(m_el, axis=-1, keepdims=True), (blk, blk))
        l_el = jnp.zeros((blk, blk), F32)
        acc = jnp.zeros((blk, hd), F32)
        for n, s in enumerate(scores):
            p = jnp.exp2(s - m)
            l_el = l_el + p
            acc = acc + _dot(p.astype(BF16), vb_scr[n * blk:(n + 1) * blk, :])
        out = acc / jnp.sum(l_el, axis=-1, keepdims=True)
        o_ref[rows, :] = (out * _silu(g_ref[rows, :].astype(F32))).astype(o_ref.dtype)

    for k in range(n_blocks):
        tile(k)


def _moba_prompt(pqk, pv, pg, q_gain, k_gain, batch, t_rows, out_dtype):
    m, n = pqk.shape
    h = MOBA_HEADS
    hd = MOBA_HD
    assert n == 2 * h * hd and t_rows % MOBA_BLOCK == 0
    nb = t_rows // MOBA_BLOCK
    blk = MOBA_BLOCK
    return pl.pallas_call(
        functools.partial(_moba_prompt_kernel, n_blocks=nb),
        out_shape=(jax.ShapeDtypeStruct((m, h * hd), out_dtype),
                   jax.ShapeDtypeStruct((m, h * hd), F32)),
        grid=(batch, h),
        in_specs=[
            pl.BlockSpec((t_rows, hd), lambda b, hh: (b, hh)),
            pl.BlockSpec((t_rows, hd), lambda b, hh: (b, h + hh)),
            pl.BlockSpec((t_rows, hd), lambda b, hh: (b, hh)),
            pl.BlockSpec((t_rows, hd), lambda b, hh: (b, hh)),
            pl.BlockSpec((1, hd), lambda b, hh: (0, 0)),
            pl.BlockSpec((1, hd), lambda b, hh: (0, 0)),
        ],
        out_specs=(pl.BlockSpec((t_rows, hd), lambda b, hh: (b, hh)),
                   pl.BlockSpec((t_rows, hd), lambda b, hh: (b, hh))),
        scratch_shapes=[
            pltpu.VMEM((t_rows, 2 * hd), BF16),
            pltpu.VMEM((t_rows, hd), BF16),
            pltpu.VMEM((LANES, hd), F32),
        ],
        compiler_params=_cparams(("parallel", "parallel")),
        name="moba_prompt",
    )(pqk, pqk, pv, pg, q_gain.reshape(1, hd), k_gain.reshape(1, hd))


KMEAN_PAGES_PER_STEP = 8


def _kmean_kernel(pt_ref, *refs):
    del pt_ref
    page_refs, o_ref = refs[:-1], refs[-1]
    n = pl.program_id(1)
    blocks_per_step = len(page_refs) // 2
    for j in range(blocks_per_step):
        tot = jnp.sum(page_refs[2 * j][...], axis=0) + jnp.sum(page_refs[2 * j + 1][...], axis=0)
        o_ref[n * blocks_per_step + j] = tot * (1.0 / MOBA_BLOCK)


def _moba_kmean(cache, page_table):
    _, page, heads, hd = cache.shape
    db, n_pages = page_table.shape
    pps = math.gcd(n_pages, KMEAN_PAGES_PER_STEP)
    assert MOBA_BLOCK == 2 * page and pps % 2 == 0
    nb = n_pages // 2

    def page_spec(j):
        return pl.BlockSpec((None, page, heads, hd), lambda b, n, pt: (pt[b, pps * n + j], 0, 0, 0))

    return pl.pallas_call(
        _kmean_kernel,
        out_shape=jax.ShapeDtypeStruct((db, nb, heads, hd), F32),
        grid_spec=pltpu.PrefetchScalarGridSpec(
            num_scalar_prefetch=1,
            grid=(db, n_pages // pps),
            in_specs=[page_spec(j) for j in range(pps)],
            out_specs=pl.BlockSpec((None, nb, heads, hd), lambda b, n, pt: (b, 0, 0, 0)),
        ),
        compiler_params=_cparams(("parallel", "arbitrary")),
        name="moba_past_block_means",
    )(page_table, *([cache] * pps))


def _moba_select_kernel(q_ref, k_ref, km_ref, qg_ref, kg_ref, qn_ref, kn_ref, idx_ref, *, n_blocks):
    hd = MOBA_HD
    t = q_ref.shape[0]
    lane = lax.broadcasted_iota(jnp.int32, (t, LANES), 1)
    for h in range(MOBA_HEADS):
        cols = slice(h * hd, (h + 1) * hd)
        qn = _head_norm(q_ref[:, cols], qg_ref[...])
        kn = _head_norm(k_ref[:, cols], kg_ref[...])
        qn_ref[:, cols] = qn
        kn_ref[:, cols] = kn
        km = _pad_rows(km_ref[:, h, :], LANES)
        gate = _dot_nt_3pass(qn, km)
        gm = jnp.where(lane < n_blocks, gate, NEG_INF)
        picks = jnp.zeros((t, LANES), jnp.int32)
        for j in range(MOBA_TOPK):
            best = jnp.max(gm, axis=-1, keepdims=True)
            ix = jnp.min(jnp.where(gm == best, lane, LANES), axis=-1, keepdims=True)
            picks = jnp.where(lane == j, ix, picks)
            gm = jnp.where(lane == ix, NEG_INF, gm)
        idx_ref[h] = picks


def _moba_select(p, kmean, q_gain, k_gain, batch, t_rows):
    m, n = p.shape
    w = MOBA_HEADS * MOBA_HD
    nb = kmean.shape[1]
    assert nb >= MOBA_TOPK and nb <= LANES
    return pl.pallas_call(
        functools.partial(_moba_select_kernel, n_blocks=nb),
        out_shape=(jax.ShapeDtypeStruct((m, w), F32),
                   jax.ShapeDtypeStruct((m, w), F32),
                   jax.ShapeDtypeStruct((batch, MOBA_HEADS, t_rows, LANES), jnp.int32)),
        grid=(batch,),
        in_specs=[
            pl.BlockSpec((t_rows, w), lambda b: (b, 0)),
            pl.BlockSpec((t_rows, w), lambda b: (b, 1)),
            pl.BlockSpec((None, nb, MOBA_HEADS, MOBA_HD), lambda b: (b, 0, 0, 0)),
            pl.BlockSpec((1, MOBA_HD), lambda b: (0, 0)),
            pl.BlockSpec((1, MOBA_HD), lambda b: (0, 0)),
        ],
        out_specs=(pl.BlockSpec((t_rows, w), lambda b: (b, 0)),
                   pl.BlockSpec((t_rows, w), lambda b: (b, 0)),
                   pl.BlockSpec((None, MOBA_HEADS, t_rows, LANES), lambda b: (b, 0, 0, 0))),
        compiler_params=_cparams(("parallel",)),
        name="moba_sample_select",
    )(p, p, kmean, q_gain.reshape(1, MOBA_HD), k_gain.reshape(1, MOBA_HD))


def _moba_sample_attn_kernel(idx_ref, pt_ref, ck_ref, cv_ref, qn_ref, kn_ref, vn_ref, g_ref, o_ref,
                             kbuf, vbuf, sem, *, past_len, t_rows):
    step = pl.program_id(0)
    n_steps = pl.num_programs(0)
    h = step % MOBA_HEADS
    hd = MOBA_HD
    blk = MOBA_BLOCK
    page = blk // 2
    n_sel = t_rows * MOBA_TOPK
    scale = hd ** -0.5
    slope = _alibi_slope(h, MOBA_HEADS)

    def for_each_copy(s, slot, fn):
        sb = s // MOBA_HEADS
        sh = s % MOBA_HEADS

        def body(u, carry):
            blk_id = idx_ref[s * n_sel + u]
            for half in range(2):
                pg = pt_ref[sb, 2 * blk_id + half]
                rows = pl.ds(half * page, page)
                fn(pltpu.make_async_copy(ck_ref.at[pg, :, sh, :], kbuf.at[slot, u, rows, :], sem.at[slot]))
                fn(pltpu.make_async_copy(cv_ref.at[pg, :, sh, :], vbuf.at[slot, u, rows, :], sem.at[slot]))
            return carry

        lax.fori_loop(0, n_sel, body, 0)

    slot = step % 2

    @pl.when(step == 0)
    def _():
        for_each_copy(step, slot, lambda cp: cp.start())

    @pl.when(step + 1 < n_steps)
    def _():
        for_each_copy(step + 1, 1 - slot, lambda cp: cp.start())

    for_each_copy(step, slot, lambda cp: cp.wait())

    offs = lax.broadcasted_iota(jnp.int32, (1, blk), 1)
    key_i = lax.broadcasted_iota(jnp.int32, (t_rows, 1), 0)
    kn = kn_ref[...]
    vn = vn_ref[...]

    for qi in range(t_rows):
        qrow = qn_ref[qi:qi + 1, :]
        q8 = jnp.broadcast_to(qrow, (8, hd)).astype(BF16)
        t_pos = past_len + qi
        s_own = jnp.sum(kn * qrow, axis=-1, keepdims=True) * scale - slope * (qi - key_i).astype(F32)
        s_own = jnp.where(key_i <= qi, s_own, NEG_INF)
        m = jnp.max(s_own, axis=0, keepdims=True)
        scores = []
        for j in range(MOBA_TOPK):
            u = qi * MOBA_TOPK + j
            dist = (t_pos - idx_ref[step * n_sel + u] * blk - offs).astype(F32)
            s = _dot_nt(q8, kbuf[slot, u].astype(BF16))[0:1, :] * scale - slope * dist
            scores.append(s)
            m = jnp.maximum(m, jnp.max(s, axis=-1, keepdims=True))
        p_own = jnp.exp(s_own - m)
        denom = jnp.sum(p_own, axis=0, keepdims=True)
        out = jnp.sum(p_own * vn, axis=0, keepdims=True)
        for j, s in enumerate(scores):
            u = qi * MOBA_TOPK + j
            pj = jnp.exp(s - m)
            denom = denom + jnp.sum(pj, axis=-1, keepdims=True)
            out = out + _dot(jnp.broadcast_to(pj, (8, blk)).astype(BF16), vbuf[slot, u].astype(BF16))[0:1, :]
        gate = g_ref[qi:qi + 1, :]
        o_ref[qi:qi + 1, :] = out / denom * _silu(gate)


def _moba_sample_attn(idx, page_table, cache_k, cache_v, qn, kn, vn, gate, batch, t_rows):
    h = MOBA_HEADS
    hd = MOBA_HD
    _, page, heads, _ = cache_k.shape
    n_pages = page_table.shape[1]
    assert MOBA_BLOCK == 2 * page and heads == h
    n_sel = t_rows * MOBA_TOPK
    row_spec = pl.BlockSpec((t_rows, hd), lambda s, ix, pt: (s // h, s % h))
    return pl.pallas_call(
        functools.partial(_moba_sample_attn_kernel, past_len=n_pages * page, t_rows=t_rows),
        out_shape=jax.ShapeDtypeStruct((batch * t_rows, h * hd), F32),
        grid_spec=pltpu.PrefetchScalarGridSpec(
            num_scalar_prefetch=2,
            grid=(batch * h,),
            in_specs=[pl.BlockSpec(memory_space=pl.ANY), pl.BlockSpec(memory_space=pl.ANY),
                      row_spec, row_spec, row_spec, row_spec],
            out_specs=row_spec,
            scratch_shapes=[pltpu.VMEM((2, n_sel, MOBA_BLOCK, hd), F32),
                            pltpu.VMEM((2, n_sel, MOBA_BLOCK, hd), F32),
                            pltpu.SemaphoreType.DMA((2,))],
        ),
        compiler_params=_cparams(("arbitrary",)),
        name="moba_sample_attention",
    )(idx, page_table, cache_k, cache_v, qn, kn, vn, gate)


def _retention_layer(hp, hs, gain, w_in, w_out, slot, state, bp, tp, bs, ts):
    (pp,), (ps,) = _in_proj(hp, hs, gain, w_in, slot, [(w_in.shape[2], BF16)])
    op, st_p = _retention(pp, None, 0, bp, tp, BF16)
    o_s, st_s = _retention(ps, state, slot, bs, ts, F32)
    hp, hs = _out_proj(op, o_s, w_out, slot, hp, hs)
    return hp, hs, st_p, st_s


def _hgrn_layer(hp, hs, gain, w_in, lb_logits, o_gain, w_out, slot, state, layer, bp, tp, bs, ts):
    w = w_in.shape[2] // 4
    (pq, pf, pig), (sq, sf, sig) = _in_proj(hp, hs, gain, w_in, slot, [(w, BF16), (w, F32), (2 * w, BF16)])
    op, st_p = _hgrn(pq, pf, pig, lb_logits, o_gain, None, 0, layer, bp, tp, BF16)
    o_s, st_s = _hgrn(sq, sf, sig, lb_logits, o_gain, state, slot, layer, bs, ts, F32)
    hp, hs = _out_proj(op, o_s, w_out, slot, hp, hs)
    return hp, hs, st_p, st_s


def _moba_layer(hp, hs, gain, w_in, q_gain, k_gain, w_out, slot, cache_k, cache_v, page_table, bp, tp, bs, ts):
    w = MOBA_HEADS * MOBA_HD
    (pqk, pv, pg), (sqk, sv, sg) = _in_proj(hp, hs, gain, w_in, slot, [(2 * w, F32), (w, F32), (w, BF16)])
    kmean = _moba_kmean(cache_k, page_table)
    qn, kn_s, picks = _moba_select(sqk, kmean, q_gain, k_gain, bs, ts)
    idx = picks[..., :MOBA_TOPK].reshape(-1)
    o_s = _moba_sample_attn(idx, page_table, cache_k, cache_v, qn, kn_s, sv, sg, bs, ts)
    op, kn_p = _moba_prompt(pqk, pv, pg, q_gain, k_gain, bp, tp, BF16)
    hp, hs = _out_proj(op, o_s, w_out, slot, hp, hs)
    return hp, hs, kn_p, pv, kn_s, sv


def kernel(x_prompt, x_sample, state_ret, cache_k, cache_v, state_hgrn, page_table, norm_gain, ret_w_in, ret_w_out, moba_w_in, moba_q_gain, moba_k_gain, moba_w_out, hgrn_w_in, hgrn_lb_logits, hgrn_o_gain, hgrn_w_out):
    bp, tp, d = x_prompt.shape
    bs, ts, _ = x_sample.shape
    depth = norm_gain.shape[0]
    n_pool = cache_k.shape[1]
    hp = x_prompt.reshape(bp * tp, d)
    hs = x_sample.reshape(bs * ts, d)
    ck = cache_k.reshape((-1,) + cache_k.shape[2:])
    cv = cache_v.reshape((-1,) + cache_v.shape[2:])
    kp_l, vp_l, ks_l, vs_l, rp_l, rs_l, gp_l, gs_l = [], [], [], [], [], [], [], []
    for layer in range(depth):
        kind, slot = layer % N_MIXERS, layer // N_MIXERS
        gain = norm_gain[layer]
        if kind == 0:
            hp, hs, st_p, st_s = _retention_layer(hp, hs, gain, ret_w_in, ret_w_out, slot, state_ret,
                                                  bp, tp, bs, ts)
            rp_l.append(st_p)
            rs_l.append(st_s)
        elif kind == 1:
            hp, hs, kp, vp, ks_, vs_ = _moba_layer(hp, hs, gain, moba_w_in, moba_q_gain[slot], moba_k_gain[slot],
                                                   moba_w_out, slot, ck, cv, page_table + slot * n_pool,
                                                   bp, tp, bs, ts)
            kp, vp, hp = lax.optimization_barrier((kp, vp, hp))
            shp = (MOBA_HEADS, MOBA_HD)
            kp_l.append(kp.reshape(bp, tp, *shp))
            vp_l.append(vp.reshape(bp, tp, *shp))
            ks_l.append(ks_.reshape(bs, ts, *shp))
            vs_l.append(vs_.reshape(bs, ts, *shp))
        else:
            hp, hs, st_p, st_s = _hgrn_layer(hp, hs, gain, hgrn_w_in, hgrn_lb_logits, hgrn_o_gain[slot],
                                             hgrn_w_out, slot, state_hgrn, layer, bp, tp, bs, ts)
            gp_l.append(st_p)
            gs_l.append(st_s)
    return (hp.reshape(bp, tp, d), hs.reshape(bs, ts, d),
            jnp.stack(kp_l), jnp.stack(vp_l), jnp.stack(ks_l), jnp.stack(vs_l),
            jnp.stack(rp_l), jnp.stack(rs_l), jnp.stack(gp_l), jnp.stack(gs_l))
```

```python
import functools
import math

import numpy as np
import jax
import jax.numpy as jnp
from jax import lax
from jax.experimental import pallas as pl
from jax.experimental.pallas import tpu as pltpu

F32 = jnp.float32
BF16 = jnp.bfloat16
EPS = 1e-6
NEG_INF = float("-inf")
LOG2E = 1.4426950408889634
MASK_NEG = -1e30

N_MIXERS = 3
RET_HEADS = 8
RET_CHUNK = 128
MOBA_HEADS = 16
MOBA_HD = 128
MOBA_BLOCK = 256
MOBA_TOPK = 3
HG_HEADS = 16
HG_DK = 128
HG_CHUNK = 128
HG_SUB = 16
HG_FACTORED_SUB = 64
HG_FACTORED_MAX_LOG2_SPAN = 100.0

VMEM_LIMIT_BYTES = 52 * 1024 * 1024
LANES = 128


def _cparams(sem):
    return pltpu.CompilerParams(dimension_semantics=sem, vmem_limit_bytes=VMEM_LIMIT_BYTES)


def _dot(a, b):
    return jnp.dot(a, b, preferred_element_type=F32)


def _dot_nt(a, b):
    return lax.dot_general(a, b, (((1,), (1,)), ((), ())), preferred_element_type=F32)


def _dot_tn(a, b):
    return lax.dot_general(a, b, (((0,), (0,)), ((), ())), preferred_element_type=F32)


def _split_bf16(x, parts):
    pieces = []
    for _ in range(parts):
        hi = x.astype(BF16)
        pieces.append(hi)
        x = x - hi.astype(F32)
    return pieces


def _dot_nt_3pass(a, b):
    a_hi, a_lo = _split_bf16(a, 2)
    b_hi, b_lo = _split_bf16(b, 2)
    return _dot_nt(a_hi, b_hi) + (_dot_nt(a_hi, b_lo) + _dot_nt(a_lo, b_hi))


def _silu(x):
    return x * jax.nn.sigmoid(x)


def _pad_rows(x, rows):
    if x.shape[0] == rows:
        return x
    return jnp.concatenate([x, jnp.zeros((rows - x.shape[0], x.shape[1]), x.dtype)], axis=0)


PROJ_TM, PROJ_TN = 1024, 512
PROJ_SUB_ROWS = 512


def _proj_kernel(*refs, n_i, n_groups, bounds, norm, residual):
    it = iter(refs)
    xp_ref, xs_ref = next(it), next(it)
    gain_ref = next(it) if norm else None
    w_ref = next(it)
    res_p = [next(it) for _ in range(n_groups)] if residual else None
    res_s = [next(it) for _ in range(n_groups)] if residual else None
    outs_p = [next(it) for _ in range(n_groups)]
    outs_s = [next(it) for _ in range(n_groups)]
    xn_ref = next(it) if norm else None
    i = pl.program_id(0)
    j = pl.program_id(1)
    ms_rows = xs_ref.shape[0]

    if norm:
        def normed(x):
            return (x * lax.rsqrt(jnp.mean(x * x, axis=-1, keepdims=True) + EPS) * gain_ref[...]).astype(BF16)

        @pl.when((j == 0) & (i < n_i))
        def _():
            xn_ref[...] = normed(xp_ref[...])

        @pl.when((j == 0) & (i == n_i))
        def _():
            xn_ref[:ms_rows, :] = normed(xs_ref[...])

    for g, (j0, j1) in enumerate(bounds):
        in_group = (j >= j0) & (j < j1)

        @pl.when(in_group & (i < n_i))
        def _(g=g):
            lhs_ref = xn_ref if norm else xp_ref
            rows_total = lhs_ref.shape[0]
            sub = min(PROJ_SUB_ROWS, rows_total)
            for r0 in range(0, rows_total, sub):
                acc = _dot(lhs_ref[r0:r0 + sub, :], w_ref[...])
                if residual:
                    acc = acc + res_p[g][r0:r0 + sub, :]
                outs_p[g][r0:r0 + sub, :] = acc.astype(outs_p[g].dtype)

        @pl.when(in_group & (i == n_i))
        def _(g=g):
            acc = _dot(xn_ref[:ms_rows, :] if norm else xs_ref[...].astype(BF16), w_ref[...])
            if residual:
                acc = acc + res_s[g][...]
            outs_s[g][...] = acc


def _proj(xp, xs, w, slot, groups, gain=None, residual=None, name="proj"):
    mp, d = xp.shape
    ms = xs.shape[0]
    n = w.shape[2]
    tm, tn = min(PROJ_TM, mp), PROJ_TN
    n_i, n_j = mp // tm, n // tn
    assert mp % tm == 0 and ms <= tm and n % tn == 0 and w.dtype == BF16
    assert sum(c for c, _ in groups) == n and all(c % tn == 0 for c, _ in groups)
    assert residual is None or len(groups) == 1
    norm = gain is not None
    bounds, j0 = [], 0
    for c, _ in groups:
        bounds.append((j0, j0 + c // tn))
        j0 += c // tn

    def prompt_map(j0, j1):
        def imap(i, j):
            jj = jnp.where(i < n_i, j, n_j - 1)
            return (jnp.minimum(i, n_i - 1), jnp.clip(jj - j0, 0, j1 - j0 - 1))
        return imap

    def sample_map(j0, j1):
        def imap(i, j):
            jj = jnp.where(i < n_i, 0, j)
            return (0, jnp.clip(jj - j0, 0, j1 - j0 - 1))
        return imap

    p_specs = [pl.BlockSpec((tm, tn), prompt_map(*b)) for b in bounds]
    s_specs = [pl.BlockSpec((ms, tn), sample_map(*b)) for b in bounds]
    in_specs = [
        pl.BlockSpec((tm, d), lambda i, j: (jnp.minimum(i, n_i - 1), 0), pipeline_mode=pl.Buffered(1)),
        pl.BlockSpec((ms, d), lambda i, j: (0, 0)),
    ]
    args = [xp, xs]
    if norm:
        in_specs.append(pl.BlockSpec((1, d), lambda i, j: (0, 0)))
        args.append(gain.reshape(1, d))
    in_specs.append(pl.BlockSpec((None, d, tn), lambda i, j: (slot, 0, j)))
    args.append(w)
    if residual is not None:
        in_specs += p_specs + s_specs
        args += list(residual)
    scratch = [pltpu.VMEM((tm, d), BF16)] if norm else []
    outs = pl.pallas_call(
        functools.partial(_proj_kernel, n_i=n_i, n_groups=len(groups), bounds=tuple(bounds),
                          norm=norm, residual=residual is not None),
        out_shape=([jax.ShapeDtypeStruct((mp, c), dt) for c, dt in groups]
                   + [jax.ShapeDtypeStruct((ms, c), F32) for c, _ in groups]),
        grid=(n_i + 1, n_j),
        in_specs=in_specs,
        out_specs=p_specs + s_specs,
        scratch_shapes=scratch,
        compiler_params=_cparams(("arbitrary", "arbitrary")),
        name=name,
    )(*args)
    return outs[:len(groups)], outs[len(groups):]


def _in_proj(xp, xs, gain, w, slot, groups):
    return _proj(xp, xs, w, slot, groups, gain=gain, name="norm_in_proj")


def _out_proj(ap, a_s, w, slot, rp, rs):
    (op,), (o_s,) = _proj(ap, a_s, w, slot, [(w.shape[2], F32)], residual=(rp, rs), name="out_proj_residual")
    return op, o_s


def _ret_tables(c, dk):
    scale = dk ** -0.5
    log_g = np.log1p(-np.exp2(-5.0 - np.arange(RET_HEADS, dtype=np.float64)))
    i = np.arange(c, dtype=np.float64)
    rel = i[:, None] - i[None, :]
    intra = np.where(rel >= 0, np.exp(log_g[:, None, None] * np.maximum(rel, 0.0)), 0.0) * scale
    q_dec = np.exp(log_g[:, None] * (i + 1.0))
    k_dec = np.exp(log_g[:, None] * (c - 1.0 - i)) * scale
    c_dec = np.exp(log_g * c)
    big = RET_CHUNK
    intra_p = np.zeros((RET_HEADS, big, big), np.float32)
    intra_p[:, :c, :c] = intra
    qd = np.zeros((RET_HEADS, big, 1), np.float32)
    qd[:, :c, 0] = q_dec
    kd = np.zeros((RET_HEADS, big, 1), np.float32)
    kd[:, :c, 0] = k_dec
    cd = c_dec.astype(np.float32).reshape(RET_HEADS, 1, 1)
    return jnp.asarray(intra_p), jnp.asarray(qd), jnp.asarray(kd), jnp.asarray(cd)


def _ret_kernel(*refs, t_rows, hps, has_s0):
    if has_s0:
        q_ref, k_ref, v_ref, g_ref, intra_ref, qd_ref, kd_ref, cd_ref, s0_ref, o_ref, s_ref, st = refs
        st[...] = s0_ref[...]
    else:
        q_ref, k_ref, v_ref, g_ref, intra_ref, qd_ref, kd_ref, cd_ref, o_ref, s_ref, st = refs
        st[...] = jnp.zeros_like(st)
    c = RET_CHUNK
    t_pad = max(t_rows, c)
    dk = q_ref.shape[1] // hps
    dv = v_ref.shape[1] // hps

    def chunk(ci, carry):
        if t_rows >= c:
            rows = pl.ds(pl.multiple_of(ci * c, c), c)
        else:
            rows = pl.ds(0, t_rows)
        for hj in range(hps):
            kcols = slice(hj * dk, (hj + 1) * dk)
            vcols = slice(hj * dv, (hj + 1) * dv)
            qc = _pad_rows(q_ref[rows, kcols], c).astype(BF16)
            kf = _pad_rows(k_ref[rows, kcols], c).astype(F32)
            vc = _pad_rows(v_ref[rows, vcols], c).astype(BF16)
            s = st[hj]
            att = _dot_nt(qc, kf.astype(BF16)) * intra_ref[hj]
            o = _dot(att.astype(BF16), vc) + qd_ref[hj] * _dot(qc, s.astype(BF16))
            st[hj] = s * cd_ref[hj] + _dot_tn((kf * kd_ref[hj]).astype(BF16), vc)
            o = o * lax.rsqrt(jnp.mean(o * o, axis=-1, keepdims=True) + EPS)
            gate = g_ref[rows, vcols].astype(F32)
            o_ref[rows, vcols] = (o[: gate.shape[0]] * _silu(gate)).astype(o_ref.dtype)
        return carry

    n_chunks = t_pad // c
    lax.fori_loop(0, n_chunks, chunk, 0, unroll=2 if n_chunks % 2 == 0 else 1)
    s_ref[...] = st[...]


def _retention(p, s0, slot, batch, t_rows, out_dtype):
    m, n = p.shape
    h = RET_HEADS
    dk = n // (6 * h)
    dv = 2 * dk
    c_eff = math.gcd(t_rows, RET_CHUNK)
    assert t_rows % c_eff == 0 and (t_rows >= RET_CHUNK or t_rows == c_eff)
    intra, qd, kd, cd = _ret_tables(c_eff, dk)
    has_s0 = s0 is not None
    hps = 2 if t_rows > RET_CHUNK else h
    n_hg = h // hps
    in_specs = [
        pl.BlockSpec((t_rows, hps * dk), lambda b, hg: (b, hg)),
        pl.BlockSpec((t_rows, hps * dk), lambda b, hg: (b, n_hg + hg)),
        pl.BlockSpec((t_rows, hps * dv), lambda b, hg: (b, n_hg + hg)),
        pl.BlockSpec((t_rows, hps * dv), lambda b, hg: (b, 2 * n_hg + hg)),
        pl.BlockSpec((hps, RET_CHUNK, RET_CHUNK), lambda b, hg: (hg, 0, 0)),
        pl.BlockSpec((hps, RET_CHUNK, 1), lambda b, hg: (hg, 0, 0)),
        pl.BlockSpec((hps, RET_CHUNK, 1), lambda b, hg: (hg, 0, 0)),
        pl.BlockSpec((hps, 1, 1), lambda b, hg: (hg, 0, 0)),
    ]
    args = [p, p, p, p, intra, qd, kd, cd]
    if has_s0:
        in_specs.append(pl.BlockSpec((None, None, hps, dk, dv), lambda b, hg: (slot, b, hg, 0, 0)))
        args.append(s0)
    return pl.pallas_call(
        functools.partial(_ret_kernel, t_rows=t_rows, hps=hps, has_s0=has_s0),
        out_shape=(jax.ShapeDtypeStruct((m, h * dv), out_dtype),
                   jax.ShapeDtypeStruct((batch, h, dk, dv), F32)),
        grid=(batch, n_hg),
        in_specs=in_specs,
        out_specs=(pl.BlockSpec((t_rows, hps * dv), lambda b, hg: (b, hg)),
                   pl.BlockSpec((None, hps, dk, dv), lambda b, hg: (b, hg, 0, 0))),
        scratch_shapes=[pltpu.VMEM((hps, dk, dv), F32)],
        compiler_params=_cparams(("parallel", "parallel")),
        name="retention",
    )(*args)


def _hgrn_kernel(*refs, layer, t_rows, chunk, hps, has_s0):
    n_in = 7 if has_s0 else 6
    q_ref, f_ref, i_ref, g_ref, lbl_ref, og_ref = refs[:6]
    o_ref, s_ref = refs[n_in:n_in + 2]
    scratch = refs[n_in + 2:]
    scratch_sets = [scratch[3 * n:3 * n + 3] for n in range(len(scratch) // 3)]
    c = chunk
    r = HG_SUB
    n_sub = c // r
    n_chunks = max(t_rows, c) // c
    dk = q_ref.shape[1] // hps

    logits = lbl_ref[...]
    e = jnp.exp(logits - jnp.max(logits, axis=0, keepdims=True))
    prob = e / jnp.sum(e, axis=0, keepdims=True)
    lrow = lax.broadcasted_iota(jnp.int32, prob.shape, 0)
    in_lb = (lrow >= 1) & (lrow <= layer)
    lb_all = jnp.sum(jnp.where(in_lb, prob, 0.0), axis=0, keepdims=True)
    one_m_lb_all = jnp.sum(jnp.where(in_lb, 0.0, prob), axis=0, keepdims=True)

    ri = lax.broadcasted_iota(jnp.int32, (c, LANES), 0)
    cj = lax.broadcasted_iota(jnp.int32, (c, LANES), 1)
    tri_mask = ri >= cj
    tri = jnp.where(tri_mask, 1.0, 0.0).astype(BF16)
    row_c = lax.broadcasted_iota(jnp.int32, (c, dk), 0)
    row_r = lax.broadcasted_iota(jnp.int32, (r, dk), 0)
    og = og_ref[...]

    def one_chunk(ci, hj, s_t, b_scr, k_scr, v_scr, factored):
        if t_rows >= c:
            rows = pl.ds(pl.multiple_of(ci * c, c), c)
        else:
            rows = pl.ds(0, t_rows)
        cols = slice(hj * dk, (hj + 1) * dk)
        lb = lb_all[:, cols]
        one_m_lb = one_m_lb_all[:, cols]
        z = _pad_rows(f_ref[rows, cols].astype(F32), c)
        qraw = _pad_rows(q_ref[rows, cols].astype(F32), c)
        v = _pad_rows(i_ref[rows, cols].astype(F32), c)
        f = lb + one_m_lb * jax.nn.sigmoid(z)
        logf = jnp.log(f)
        kk = one_m_lb * jax.nn.sigmoid(-z)
        if t_rows < c:
            valid = row_c < t_rows
            logf = jnp.where(valid, logf, 0.0)
            kk = jnp.where(valid, kk, 0.0)
        lf_hi, lf_mid, lf_lo = _split_bf16(_pad_rows(logf, LANES), 3)
        b = (_dot(tri, lf_hi) + (_dot(tri, lf_mid) + _dot(tri, lf_lo))) * LOG2E
        qs = _silu(qraw)
        vb = v.astype(BF16)
        o = _dot_nt((qs * jnp.exp2(b)).astype(BF16), s_t.astype(BF16))

        def sub_block_rows(rq, with_diagonal):
            a_rows = []
            for lo in range(0, c, rq):
                hi = lo + rq if with_diagonal else lo
                if hi == 0:
                    a_rows.append(jnp.zeros((rq, c), F32))
                    continue
                b0 = b[lo - 1:lo, :] if lo else jnp.zeros((1, dk), F32)
                qh = qs[lo:lo + rq, :] * jnp.exp2(b[lo:lo + rq, :] - b0)
                kh = kk[:hi, :] * jnp.exp2(b0 - b[:hi, :])
                a_rows.append(_dot_nt(qh.astype(BF16), _pad_rows(kh, c).astype(BF16)))
            return jnp.concatenate(a_rows, axis=0)

        def intra_factored():
            att = jnp.where(tri_mask, sub_block_rows(HG_FACTORED_SUB, True), 0.0)
            return _dot(att.astype(BF16), vb)

        def intra_pairwise():
            b_scr[...] = b
            k_scr[...] = kk
            v_scr[...] = v
            o_diag = []
            for sb in range(n_sub):
                lo = sb * r
                bi = b[lo:lo + r, :]
                qi = qs[lo:lo + r, :]
                acc = jnp.zeros((r, dk), F32)
                for s in range(min(r, t_rows)):
                    bs = b_scr[lo + s:lo + s + 1, :]
                    ks = k_scr[lo + s:lo + s + 1, :]
                    vs = v_scr[lo + s:lo + s + 1, :]
                    d = bi - bs
                    if s > 0:
                        d = jnp.where(row_r >= s, d, NEG_INF)
                    col = jnp.sum(qi * jnp.exp2(d) * ks, axis=-1, keepdims=True)
                    acc = acc + col * vs
                o_diag.append(acc)
            out = jnp.concatenate(o_diag, axis=0)
            if n_sub > 1:
                out = out + _dot(sub_block_rows(r, False).astype(BF16), vb)
            return out

        o = o + (intra_factored() if factored else intra_pairwise())

        b_end = b[c - 1:c, :]
        kd = kk * jnp.exp2(b_end - b)
        s_new = s_t * jnp.exp2(b_end) + _dot_tn(_pad_rows(vb, LANES), _pad_rows(kd.astype(BF16), LANES))

        o = o * lax.rsqrt(jnp.mean(o * o, axis=-1, keepdims=True) + EPS) * og
        gate = g_ref[rows, cols].astype(F32)
        o_ref[rows, cols] = (o[: gate.shape[0]] * _silu(gate)).astype(o_ref.dtype)
        return s_new

    s_init = tuple(refs[6][hj].T if has_s0 else jnp.zeros((dk, dk), F32) for hj in range(hps))

    def run(factored):
        if n_chunks == 1:
            return tuple(one_chunk(0, hj, s_init[hj], *scratch_sets[hj], factored) for hj in range(hps))

        def pair(i, states):
            states = tuple(one_chunk(2 * i, hj, states[hj], *scratch_sets[2 * hj], factored)
                           for hj in range(hps))
            return tuple(one_chunk(2 * i + 1, hj, states[hj], *scratch_sets[2 * hj + 1], factored)
                         for hj in range(hps))

        return lax.fori_loop(0, n_chunks // 2, pair, s_init)

    if n_sub > 1:
        factored_ok = jnp.min(lb_all) >= 2.0 ** (-HG_FACTORED_MAX_LOG2_SPAN / HG_FACTORED_SUB)
        s_fin = lax.cond(factored_ok, functools.partial(run, True), functools.partial(run, False))
    else:
        s_fin = run(False)
    for hj in range(hps):
        s_ref[hj] = s_fin[hj].T


def _hgrn(pq, pf, pig, lb_logits, o_gain, s0, slot, layer, batch, t_rows, out_dtype):
    m, n = pq.shape
    h = HG_HEADS
    dk = HG_DK
    dv = n // h
    depth = lb_logits.shape[0]
    chunk = HG_CHUNK if t_rows >= HG_CHUNK else HG_SUB
    n_chunks = max(t_rows, chunk) // chunk
    assert dv == dk == LANES and (t_rows % chunk == 0 or t_rows < chunk) and (n_chunks == 1 or n_chunks % 2 == 0)
    has_s0 = s0 is not None
    hps = 2 if n_chunks > 1 else h
    n_hg = h // hps
    wk = hps * dk
    in_specs = [
        pl.BlockSpec((t_rows, wk), lambda b, hg: (b, hg)),
        pl.BlockSpec((t_rows, wk), lambda b, hg: (b, hg)),
        pl.BlockSpec((t_rows, wk), lambda b, hg: (b, hg)),
        pl.BlockSpec((t_rows, wk), lambda b, hg: (b, n_hg + hg)),
        pl.BlockSpec((depth, wk), lambda b, hg: (0, hg)),
        pl.BlockSpec((1, dv), lambda b, hg: (0, 0)),
    ]
    args = [pq, pf, pig, pig, lb_logits, o_gain.reshape(1, dv)]
    if has_s0:
        in_specs.append(pl.BlockSpec((None, None, hps, dk, dv), lambda b, hg: (slot, b, hg, 0, 0)))
        args.append(s0)
    n_sets = hps if n_chunks == 1 else 2 * hps
    return pl.pallas_call(
        functools.partial(_hgrn_kernel, layer=layer, t_rows=t_rows, chunk=chunk, hps=hps, has_s0=has_s0),
        out_shape=(jax.ShapeDtypeStruct((m, h * dv), out_dtype),
                   jax.ShapeDtypeStruct((batch, h, dk, dv), F32)),
        grid=(batch, n_hg),
        in_specs=in_specs,
        out_specs=(pl.BlockSpec((t_rows, wk), lambda b, hg: (b, hg)),
                   pl.BlockSpec((None, hps, dk, dv), lambda b, hg: (b, hg, 0, 0))),
        scratch_shapes=[pltpu.VMEM((chunk, dk), F32)] * (3 * n_sets),
        compiler_params=_cparams(("parallel", "parallel")),
        name="hgrn2",
    )(*args)


def _head_norm(x, gain):
    return x * lax.rsqrt(jnp.mean(x * x, axis=-1, keepdims=True) + EPS) * gain


def _alibi_slope(h_index, n_heads):
    hv = jnp.full((1, 1), h_index + 1, jnp.int32).astype(F32)
    return jnp.exp2(hv * (-8.0 / n_heads))


def _moba_prompt_kernel(q_ref, k_ref, v_ref, g_ref, qg_ref, kg_ref, o_ref, kn_ref,
                        ka_scr, vb_scr, km_scr, *, n_blocks):
    blk = MOBA_BLOCK
    h = pl.program_id(1)
    hd = q_ref.shape[1]
    t_all = k_ref.shape[0]

    kn = _head_norm(k_ref[...], kg_ref[...])
    kn_ref[...] = kn
    row_blk = lax.broadcasted_iota(jnp.int32, (t_all, hd), 0) // blk
    lane_k = lax.broadcasted_iota(jnp.int32, (t_all, hd), 1)
    one_hot = jnp.where(lane_k == row_blk, 1.0, 0.0).astype(BF16)
    ka_scr[...] = jnp.concatenate([kn.astype(BF16), one_hot], axis=1)
    vb_scr[...] = v_ref[...].astype(BF16)
    means = [jnp.mean(kn[n * blk:(n + 1) * blk, :], axis=0, keepdims=True) for n in range(n_blocks)]
    km_scr[...] = _pad_rows(jnp.concatenate(means, axis=0), LANES)

    slope2 = _alibi_slope(h, MOBA_HEADS) * LOG2E
    key_off = lax.broadcasted_iota(jnp.int32, (1, blk), 1).astype(F32)
    ti = lax.broadcasted_iota(jnp.int32, (blk, blk), 0)
    sj = lax.broadcasted_iota(jnp.int32, (blk, blk), 1)

    def tile(k):
        gated = k > MOBA_TOPK
        rows = slice(k * blk, (k + 1) * blk)
        qn = _head_norm(q_ref[rows, :], qg_ref[...])
        qs = (qn * (hd ** -0.5 * LOG2E)).astype(BF16)
        if gated:
            gate = _dot_nt_3pass(qn, km_scr[...])
            lane = lax.broadcasted_iota(jnp.int32, gate.shape, 1)
            gm = jnp.where(lane < k, gate, NEG_INF)
            rank = jnp.zeros(gate.shape, jnp.int32)
            for sh in range(1, k):
                other = pltpu.roll(gm, sh, axis=1)
                ahead = (other > gm) | ((other == gm) & (lane >= sh))
                rank = rank + ahead.astype(jnp.int32)
            drop = jnp.where((lane < k) & (rank >= MOBA_TOPK), MASK_NEG, 0.0).astype(BF16)
            q_aug = jnp.concatenate([qs, drop], axis=1)
        scores = []
        for n in range(k):
            keys = ka_scr[n * blk:(n + 1) * blk, :] if gated else ka_scr[n * blk:(n + 1) * blk, :hd]
            bias = slope2 * (key_off + float((n - k) * blk))
            scores.append(_dot_nt(q_aug if gated else qs, keys) + bias)
        s_own = _dot_nt(qs, ka_scr[k * blk:(k + 1) * blk, :hd]) + slope2 * key_off
        scores.append(jnp.where(ti >= sj, s_own, NEG_INF))

        m_el = scores[0]
        for s in scores[1:]:
            m_el = jnp.maximum(m_el, s)
        m = jnp.broadcast_to(jnp.max(m_el, axis=-1, keepdims=True), (blk, blk))
        l_el = jnp.zeros((blk, blk), F32)
        acc = jnp.zeros((blk, hd), F32)
        for n, s in enumerate(scores):
            p = jnp.exp2(s - m)
            l_el = l_el + p
            acc = acc + _dot(p.astype(BF16), vb_scr[n * blk:(n + 1) * blk, :])
        out = acc / jnp.sum(l_el, axis=-1, keepdims=True)
        o_ref[rows, :] = (out * _silu(g_ref[rows, :].astype(F32))).astype(o_ref.dtype)

    for k in range(n_blocks):
        tile(k)


def _moba_prompt(pqk, pv, pg, q_gain, k_gain, batch, t_rows, out_dtype):
    m, n = pqk.shape
    h = MOBA_HEADS
    hd = MOBA_HD
    assert n == 2 * h * hd and t_rows % MOBA_BLOCK == 0
    nb = t_rows // MOBA_BLOCK
    blk = MOBA_BLOCK
    return pl.pallas_call(
        functools.partial(_moba_prompt_kernel, n_blocks=nb),
        out_shape=(jax.ShapeDtypeStruct((m, h * hd), out_dtype),
                   jax.ShapeDtypeStruct((m, h * hd), F32)),
        grid=(batch, h),
        in_specs=[
            pl.BlockSpec((t_rows, hd), lambda b, hh: (b, hh)),
            pl.BlockSpec((t_rows, hd), lambda b, hh: (b, h + hh)),
            pl.BlockSpec((t_rows, hd), lambda b, hh: (b, hh)),
            pl.BlockSpec((t_rows, hd), lambda b, hh: (b, hh)),
            pl.BlockSpec((1, hd), lambda b, hh: (0, 0)),
            pl.BlockSpec((1, hd), lambda b, hh: (0, 0)),
        ],
        out_specs=(pl.BlockSpec((t_rows, hd), lambda b, hh: (b, hh)),
                   pl.BlockSpec((t_rows, hd), lambda b, hh: (b, hh))),
        scratch_shapes=[
            pltpu.VMEM((t_rows, 2 * hd), BF16),
            pltpu.VMEM((t_rows, hd), BF16),
            pltpu.VMEM((LANES, hd), F32),
        ],
        compiler_params=_cparams(("parallel", "parallel")),
        name="moba_prompt",
    )(pqk, pqk, pv, pg, q_gain.reshape(1, hd), k_gain.reshape(1, hd))


KMEAN_PAGES_PER_STEP = 8


def _kmean_kernel(pt_ref, *refs):
    del pt_ref
    page_refs, o_ref = refs[:-1], refs[-1]
    n = pl.program_id(1)
    blocks_per_step = len(page_refs) // 2
    for j in range(blocks_per_step):
        tot = jnp.sum(page_refs[2 * j][...], axis=0) + jnp.sum(page_refs[2 * j + 1][...], axis=0)
        o_ref[n * blocks_per_step + j] = tot * (1.0 / MOBA_BLOCK)


def _moba_kmean(cache, page_table):
    _, page, heads, hd = cache.shape
    db, n_pages = page_table.shape
    pps = math.gcd(n_pages, KMEAN_PAGES_PER_STEP)
    assert MOBA_BLOCK == 2 * page and pps % 2 == 0
    nb = n_pages // 2

    def page_spec(j):
        return pl.BlockSpec((None, page, heads, hd), lambda b, n, pt: (pt[b, pps * n + j], 0, 0, 0))

    return pl.pallas_call(
        _kmean_kernel,
        out_shape=jax.ShapeDtypeStruct((db, nb, heads, hd), F32),
        grid_spec=pltpu.PrefetchScalarGridSpec(
            num_scalar_prefetch=1,
            grid=(db, n_pages // pps),
            in_specs=[page_spec(j) for j in range(pps)],
            out_specs=pl.BlockSpec((None, nb, heads, hd), lambda b, n, pt: (b, 0, 0, 0)),
        ),
        compiler_params=_cparams(("parallel", "arbitrary")),
        name="moba_past_block_means",
    )(page_table, *([cache] * pps))


def _moba_select_kernel(q_ref, k_ref, km_ref, qg_ref, kg_ref, qn_ref, kn_ref, idx_ref, *, n_blocks):
    hd = MOBA_HD
    t = q_ref.shape[0]
    lane = lax.broadcasted_iota(jnp.int32, (t, LANES), 1)
    for h in range(MOBA_HEADS):
        cols = slice(h * hd, (h + 1) * hd)
        qn = _head_norm(q_ref[:, cols], qg_ref[...])
        kn = _head_norm(k_ref[:, cols], kg_ref[...])
        qn_ref[:, cols] = qn
        kn_ref[:, cols] = kn
        km = _pad_rows(km_ref[:, h, :], LANES)
        gate = _dot_nt_3pass(qn, km)
        gm = jnp.where(lane < n_blocks, gate, NEG_INF)
        picks = jnp.zeros((t, LANES), jnp.int32)
        for j in range(MOBA_TOPK):
            best = jnp.max(gm, axis=-1, keepdims=True)
            ix = jnp.min(jnp.where(gm == best, lane, LANES), axis=-1, keepdims=True)
            picks = jnp.where(lane == j, ix, picks)
            gm = jnp.where(lane == ix, NEG_INF, gm)
        idx_ref[h] = picks


def _moba_select(p, kmean, q_gain, k_gain, batch, t_rows):
    m, n = p.shape
    w = MOBA_HEADS * MOBA_HD
    nb = kmean.shape[1]
    assert nb >= MOBA_TOPK and nb <= LANES
    return pl.pallas_call(
        functools.partial(_moba_select_kernel, n_blocks=nb),
        out_shape=(jax.ShapeDtypeStruct((m, w), F32),
                   jax.ShapeDtypeStruct((m, w), F32),
                   jax.ShapeDtypeStruct((batch, MOBA_HEADS, t_rows, LANES), jnp.int32)),
        grid=(batch,),
        in_specs=[
            pl.BlockSpec((t_rows, w), lambda b: (b, 0)),
            pl.BlockSpec((t_rows, w), lambda b: (b, 1)),
            pl.BlockSpec((None, nb, MOBA_HEADS, MOBA_HD), lambda b: (b, 0, 0, 0)),
            pl.BlockSpec((1, MOBA_HD), lambda b: (0, 0)),
            pl.BlockSpec((1, MOBA_HD), lambda b: (0, 0)),
        ],
        out_specs=(pl.BlockSpec((t_rows, w), lambda b: (b, 0)),
                   pl.BlockSpec((t_rows, w), lambda b: (b, 0)),
                   pl.BlockSpec((None, MOBA_HEADS, t_rows, LANES), lambda b: (b, 0, 0, 0))),
        compiler_params=_cparams(("parallel",)),
        name="moba_sample_select",
    )(p, p, kmean, q_gain.reshape(1, MOBA_HD), k_gain.reshape(1, MOBA_HD))


def _moba_sample_attn_kernel(idx_ref, pt_ref, ck_ref, cv_ref, qn_ref, kn_ref, vn_ref, g_ref, o_ref,
                             kbuf, vbuf, sem, *, past_len, t_rows):
    step = pl.program_id(0)
    n_steps = pl.num_programs(0)
    h = step % MOBA_HEADS
    hd = MOBA_HD
    blk = MOBA_BLOCK
    page = blk // 2
    n_sel = t_rows * MOBA_TOPK
    scale = hd ** -0.5
    slope = _alibi_slope(h, MOBA_HEADS)

    def for_each_copy(s, slot, fn):
        sb = s // MOBA_HEADS
        sh = s % MOBA_HEADS

        def body(u, carry):
            blk_id = idx_ref[s * n_sel + u]
            for half in range(2):
                pg = pt_ref[sb, 2 * blk_id + half]
                rows = pl.ds(half * page, page)
                fn(pltpu.make_async_copy(ck_ref.at[pg, :, sh, :], kbuf.at[slot, u, rows, :], sem.at[slot]))
                fn(pltpu.make_async_copy(cv_ref.at[pg, :, sh, :], vbuf.at[slot, u, rows, :], sem.at[slot]))
            return carry

        lax.fori_loop(0, n_sel, body, 0)

    slot = step % 2

    @pl.when(step == 0)
    def _():
        for_each_copy(step, slot, lambda cp: cp.start())

    @pl.when(step + 1 < n_steps)
    def _():
        for_each_copy(step + 1, 1 - slot, lambda cp: cp.start())

    for_each_copy(step, slot, lambda cp: cp.wait())

    offs = lax.broadcasted_iota(jnp.int32, (1, blk), 1)
    key_i = lax.broadcasted_iota(jnp.int32, (t_rows, 1), 0)
    kn = kn_ref[...]
    vn = vn_ref[...]

    for qi in range(t_rows):
        qrow = qn_ref[qi:qi + 1, :]
        q8 = jnp.broadcast_to(qrow, (8, hd)).astype(BF16)
        t_pos = past_len + qi
        s_own = jnp.sum(kn * qrow, axis=-1, keepdims=True) * scale - slope * (qi - key_i).astype(F32)
        s_own = jnp.where(key_i <= qi, s_own, NEG_INF)
        m = jnp.max(s_own, axis=0, keepdims=True)
        scores = []
        for j in range(MOBA_TOPK):
            u = qi * MOBA_TOPK + j
            dist = (t_pos - idx_ref[step * n_sel + u] * blk - offs).astype(F32)
            s = _dot_nt(q8, kbuf[slot, u].astype(BF16))[0:1, :] * scale - slope * dist
            scores.append(s)
            m = jnp.maximum(m, jnp.max(s, axis=-1, keepdims=True))
        p_own = jnp.exp(s_own - m)
        denom = jnp.sum(p_own, axis=0, keepdims=True)
        out = jnp.sum(p_own * vn, axis=0, keepdims=True)
        for j, s in enumerate(scores):
            u = qi * MOBA_TOPK + j
            pj = jnp.exp(s - m)
            denom = denom + jnp.sum(pj, axis=-1, keepdims=True)
            out = out + _dot(jnp.broadcast_to(pj, (8, blk)).astype(BF16), vbuf[slot, u].astype(BF16))[0:1, :]
        gate = g_ref[qi:qi + 1, :]
        o_ref[qi:qi + 1, :] = out / denom * _silu(gate)


def _moba_sample_attn(idx, page_table, cache_k, cache_v, qn, kn, vn, gate, batch, t_rows):
    h = MOBA_HEADS
    hd = MOBA_HD
    _, page, heads, _ = cache_k.shape
    n_pages = page_table.shape[1]
    assert MOBA_BLOCK == 2 * page and heads == h
    n_sel = t_rows * MOBA_TOPK
    row_spec = pl.BlockSpec((t_rows, hd), lambda s, ix, pt: (s // h, s % h))
    return pl.pallas_call(
        functools.partial(_moba_sample_attn_kernel, past_len=n_pages * page, t_rows=t_rows),
        out_shape=jax.ShapeDtypeStruct((batch * t_rows, h * hd), F32),
        grid_spec=pltpu.PrefetchScalarGridSpec(
            num_scalar_prefetch=2,
            grid=(batch * h,),
            in_specs=[pl.BlockSpec(memory_space=pl.ANY), pl.BlockSpec(memory_space=pl.ANY),
                      row_spec, row_spec, row_spec, row_spec],
            out_specs=row_spec,
            scratch_shapes=[pltpu.VMEM((2, n_sel, MOBA_BLOCK, hd), F32),
                            pltpu.VMEM((2, n_sel, MOBA_BLOCK, hd), F32),
                            pltpu.SemaphoreType.DMA((2,))],
        ),
        compiler_params=_cparams(("arbitrary",)),
        name="moba_sample_attention",
    )(idx, page_table, cache_k, cache_v, qn, kn, vn, gate)


def _retention_layer(hp, hs, gain, w_in, w_out, slot, state, bp, tp, bs, ts):
    (pp,), (ps,) = _in_proj(hp, hs, gain, w_in, slot, [(w_in.shape[2], BF16)])
    op, st_p = _retention(pp, None, 0, bp, tp, BF16)
    o_s, st_s = _retention(ps, state, slot, bs, ts, F32)
    hp, hs = _out_proj(op, o_s, w_out, slot, hp, hs)
    return hp, hs, st_p, st_s


def _hgrn_layer(hp, hs, gain, w_in, lb_logits, o_gain, w_out, slot, state, layer, bp, tp, bs, ts):
    w = w_in.shape[2] // 4
    (pq, pf, pig), (sq, sf, sig) = _in_proj(hp, hs, gain, w_in, slot, [(w, BF16), (w, F32), (2 * w, BF16)])
    op, st_p = _hgrn(pq, pf, pig, lb_logits, o_gain, None, 0, layer, bp, tp, BF16)
    o_s, st_s = _hgrn(sq, sf, sig, lb_logits, o_gain, state, slot, layer, bs, ts, F32)
    hp, hs = _out_proj(op, o_s, w_out, slot, hp, hs)
    return hp, hs, st_p, st_s


def _moba_layer(hp, hs, gain, w_in, q_gain, k_gain, w_out, slot, cache_k, cache_v, page_table, bp, tp, bs, ts):
    w = MOBA_HEADS * MOBA_HD
    (pqk, pv, pg), (sqk, sv, sg) = _in_proj(hp, hs, gain, w_in, slot, [(2 * w, F32), (w, F32), (w, BF16)])
    kmean = _moba_kmean(cache_k, page_table)
    qn, kn_s, picks = _moba_select(sqk, kmean, q_gain, k_gain, bs, ts)
    idx = picks[..., :MOBA_TOPK].reshape(-1)
    o_s = _moba_sample_attn(idx, page_table, cache_k, cache_v, qn, kn_s, sv, sg, bs, ts)
    op, kn_p = _moba_prompt(pqk, pv, pg, q_gain, k_gain, bp, tp, BF16)
    hp, hs = _out_proj(op, o_s, w_out, slot, hp, hs)
    return hp, hs, kn_p, pv, kn_s, sv


def kernel(x_prompt, x_sample, state_ret, cache_k, cache_v, state_hgrn, page_table, norm_gain, ret_w_in, ret_w_out, moba_w_in, moba_q_gain, moba_k_gain, moba_w_out, hgrn_w_in, hgrn_lb_logits, hgrn_o_gain, hgrn_w_out):
    bp, tp, d = x_prompt.shape
    bs, ts, _ = x_sample.shape
    depth = norm_gain.shape[0]
    n_pool = cache_k.shape[1]
    hp = x_prompt.reshape(bp * tp, d)
    hs = x_sample.reshape(bs * ts, d)
    ck = cache_k.reshape((-1,) + cache_k.shape[2:])
    cv = cache_v.reshape((-1,) + cache_v.shape[2:])
    ret_w_in, ret_w_out, moba_w_in, moba_w_out, hgrn_w_in, hgrn_w_out = (
        w.astype(BF16) for w in (ret_w_in, ret_w_out, moba_w_in, moba_w_out, hgrn_w_in, hgrn_w_out))
    kp_l, vp_l, ks_l, vs_l, rp_l, rs_l, gp_l, gs_l = [], [], [], [], [], [], [], []
    for layer in range(depth):
        kind, slot = layer % N_MIXERS, layer // N_MIXERS
        gain = norm_gain[layer]
        if kind == 0:
            hp, hs, st_p, st_s = _retention_layer(hp, hs, gain, ret_w_in, ret_w_out, slot, state_ret,
                                                  bp, tp, bs, ts)
            rp_l.append(st_p)
            rs_l.append(st_s)
        elif kind == 1:
            hp, hs, kp, vp, ks_, vs_ = _moba_layer(hp, hs, gain, moba_w_in, moba_q_gain[slot], moba_k_gain[slot],
                                                   moba_w_out, slot, ck, cv, page_table + slot * n_pool,
                                                   bp, tp, bs, ts)
            kp, vp, hp = lax.optimization_barrier((kp, vp, hp))
            shp = (MOBA_HEADS, MOBA_HD)
            kp_l.append(kp.reshape(bp, tp, *shp))
            vp_l.append(vp.reshape(bp, tp, *shp))
            ks_l.append(ks_.reshape(bs, ts, *shp))
            vs_l.append(vs_.reshape(bs, ts, *shp))
        else:
            hp, hs, st_p, st_s = _hgrn_layer(hp, hs, gain, hgrn_w_in, hgrn_lb_logits, hgrn_o_gain[slot],
                                             hgrn_w_out, slot, state_hgrn, layer, bp, tp, bs, ts)
            gp_l.append(st_p)
            gs_l.append(st_s)
    return (hp.reshape(bp, tp, d), hs.reshape(bs, ts, d),
            jnp.stack(kp_l), jnp.stack(vp_l), jnp.stack(ks_l), jnp.stack(vs_l),
            jnp.stack(rp_l), jnp.stack(rs_l), jnp.stack(gp_l), jnp.stack(gs_l))
```

```python
import functools
import math

import numpy as np
import jax
import jax.numpy as jnp
from jax import lax
from jax.experimental import pallas as pl
from jax.experimental.pallas import tpu as pltpu

F32 = jnp.float32
BF16 = jnp.bfloat16
EPS = 1e-6
NEG_INF = float("-inf")
LOG2E = 1.4426950408889634
MASK_NEG = -1e30

N_MIXERS = 3
RET_HEADS = 8
RET_CHUNK = 128
MOBA_HEADS = 16
MOBA_HD = 128
MOBA_BLOCK = 256
MOBA_TOPK = 3
HG_HEADS = 16
HG_DK = 128
HG_CHUNK = 128
HG_SUB = 16
HG_FACTORED_SUB = 64
HG_FACTORED_MAX_LOG2_SPAN = 100.0

VMEM_LIMIT_BYTES = 52 * 1024 * 1024
LANES = 128


def _cparams(sem):
    return pltpu.CompilerParams(dimension_semantics=sem, vmem_limit_bytes=VMEM_LIMIT_BYTES)


def _dot(a, b):
    return jnp.dot(a, b, preferred_element_type=F32)


def _dot_nt(a, b):
    return lax.dot_general(a, b, (((1,), (1,)), ((), ())), preferred_element_type=F32)


def _dot_tn(a, b):
    return lax.dot_general(a, b, (((0,), (0,)), ((), ())), preferred_element_type=F32)


def _split_bf16(x, parts):
    pieces = []
    for _ in range(parts):
        hi = x.astype(BF16)
        pieces.append(hi)
        x = x - hi.astype(F32)
    return pieces


def _dot_nt_3pass(a, b):
    a_hi, a_lo = _split_bf16(a, 2)
    b_hi, b_lo = _split_bf16(b, 2)
    return _dot_nt(a_hi, b_hi) + (_dot_nt(a_hi, b_lo) + _dot_nt(a_lo, b_hi))


def _silu(x):
    return x * jax.nn.sigmoid(x)


def _pad_rows(x, rows):
    if x.shape[0] == rows:
        return x
    return jnp.concatenate([x, jnp.zeros((rows - x.shape[0], x.shape[1]), x.dtype)], axis=0)


PROJ_TM, PROJ_TN = 2048, 256
PROJ_SUB_ROWS = 512


def _proj_kernel(*refs, n_i, n_groups, bounds, norm, residual):
    it = iter(refs)
    xp_ref, xs_ref = next(it), next(it)
    gain_ref = next(it) if norm else None
    w_first_ref, w_next_ref = next(it), next(it)
    res_p = [next(it) for _ in range(n_groups)] if residual else None
    res_s = [next(it) for _ in range(n_groups)] if residual else None
    outs_p = [next(it) for _ in range(n_groups)]
    outs_s = [next(it) for _ in range(n_groups)]
    wb_even, wb_odd = next(it), next(it)
    xn_ref = next(it) if norm else None
    i = pl.program_id(0)
    j = pl.program_id(1)
    ms_rows = xs_ref.shape[0]

    if norm:
        def normed(x):
            return (x * lax.rsqrt(jnp.mean(x * x, axis=-1, keepdims=True) + EPS) * gain_ref[...]).astype(BF16)

        @pl.when((j == 0) & (i < n_i))
        def _():
            xn_ref[...] = normed(xp_ref[...])

        @pl.when((j == 0) & (i == n_i))
        def _():
            xn_ref[:ms_rows, :] = normed(xs_ref[...])

    @pl.when((i == 0) & (j == 0))
    def _():
        wb_even[...] = w_first_ref[...].astype(BF16)

    def step(w_cur, w_nxt):
        for g, (j0, j1) in enumerate(bounds):
            in_group = (j >= j0) & (j < j1)

            @pl.when(in_group & (i < n_i))
            def _(g=g):
                w_nxt[...] = w_next_ref[...].astype(BF16)
                lhs_ref = xn_ref if norm else xp_ref
                rows_total = lhs_ref.shape[0]
                sub = min(PROJ_SUB_ROWS, rows_total)
                for r0 in range(0, rows_total, sub):
                    acc = _dot(lhs_ref[r0:r0 + sub, :], w_cur[...])
                    if residual:
                        acc = acc + res_p[g][r0:r0 + sub, :]
                    outs_p[g][r0:r0 + sub, :] = acc.astype(outs_p[g].dtype)

            @pl.when(in_group & (i == n_i))
            def _(g=g):
                w_nxt[...] = w_next_ref[...].astype(BF16)
                acc = _dot(xn_ref[:ms_rows, :] if norm else xs_ref[...].astype(BF16), w_cur[...])
                if residual:
                    acc = acc + res_s[g][...]
                outs_s[g][...] = acc

    pl.when(j % 2 == 0)(functools.partial(step, wb_even, wb_odd))
    pl.when(j % 2 == 1)(functools.partial(step, wb_odd, wb_even))


def _proj(xp, xs, w, slot, groups, gain=None, residual=None, name="proj"):
    mp, d = xp.shape
    ms = xs.shape[0]
    n = w.shape[2]
    tm, tn = min(PROJ_TM, mp), PROJ_TN
    n_i, n_j = mp // tm, n // tn
    assert mp % tm == 0 and ms <= tm and n % tn == 0 and n_j % 2 == 0
    assert sum(c for c, _ in groups) == n and all(c % tn == 0 for c, _ in groups)
    assert residual is None or len(groups) == 1
    norm = gain is not None
    bounds, j0 = [], 0
    for c, _ in groups:
        bounds.append((j0, j0 + c // tn))
        j0 += c // tn

    def prompt_map(j0, j1):
        def imap(i, j):
            jj = jnp.where(i < n_i, j, n_j - 1)
            return (jnp.minimum(i, n_i - 1), jnp.clip(jj - j0, 0, j1 - j0 - 1))
        return imap

    def sample_map(j0, j1):
        def imap(i, j):
            jj = jnp.where(i < n_i, 0, j)
            return (0, jnp.clip(jj - j0, 0, j1 - j0 - 1))
        return imap

    p_specs = [pl.BlockSpec((tm, tn), prompt_map(*b)) for b in bounds]
    s_specs = [pl.BlockSpec((ms, tn), sample_map(*b)) for b in bounds]
    in_specs = [
        pl.BlockSpec((tm, d), lambda i, j: (jnp.minimum(i, n_i - 1), 0), pipeline_mode=pl.Buffered(1)),
        pl.BlockSpec((ms, d), lambda i, j: (0, 0)),
    ]
    args = [xp, xs]
    if norm:
        in_specs.append(pl.BlockSpec((1, d), lambda i, j: (0, 0)))
        args.append(gain.reshape(1, d))
    in_specs += [
        pl.BlockSpec((None, d, tn), lambda i, j: (slot, 0, 0), pipeline_mode=pl.Buffered(1)),
        pl.BlockSpec((None, d, tn), lambda i, j: (slot, 0, (j + 1) % n_j)),
    ]
    args += [w, w]
    if residual is not None:
        in_specs += p_specs + s_specs
        args += list(residual)
    scratch = [pltpu.VMEM((d, tn), BF16), pltpu.VMEM((d, tn), BF16)]
    if norm:
        scratch.append(pltpu.VMEM((tm, d), BF16))
    outs = pl.pallas_call(
        functools.partial(_proj_kernel, n_i=n_i, n_groups=len(groups), bounds=tuple(bounds),
                          norm=norm, residual=residual is not None),
        out_shape=([jax.ShapeDtypeStruct((mp, c), dt) for c, dt in groups]
                   + [jax.ShapeDtypeStruct((ms, c), F32) for c, _ in groups]),
        grid=(n_i + 1, n_j),
        in_specs=in_specs,
        out_specs=p_specs + s_specs,
        scratch_shapes=scratch,
        compiler_params=_cparams(("arbitrary", "arbitrary")),
        name=name,
    )(*args)
    return outs[:len(groups)], outs[len(groups):]


def _in_proj(xp, xs, gain, w, slot, groups):
    return _proj(xp, xs, w, slot, groups, gain=gain, name="norm_in_proj")


def _out_proj(ap, a_s, w, slot, rp, rs):
    (op,), (o_s,) = _proj(ap, a_s, w, slot, [(w.shape[2], F32)], residual=(rp, rs), name="out_proj_residual")
    return op, o_s


def _ret_tables(c, dk):
    scale = dk ** -0.5
    log_g = np.log1p(-np.exp2(-5.0 - np.arange(RET_HEADS, dtype=np.float64)))
    i = np.arange(c, dtype=np.float64)
    rel = i[:, None] - i[None, :]
    intra = np.where(rel >= 0, np.exp(log_g[:, None, None] * np.maximum(rel, 0.0)), 0.0) * scale
    q_dec = np.exp(log_g[:, None] * (i + 1.0))
    k_dec = np.exp(log_g[:, None] * (c - 1.0 - i)) * scale
    c_dec = np.exp(log_g * c)
    big = RET_CHUNK
    intra_p = np.zeros((RET_HEADS, big, big), np.float32)
    intra_p[:, :c, :c] = intra
    qd = np.zeros((RET_HEADS, big, 1), np.float32)
    qd[:, :c, 0] = q_dec
    kd = np.zeros((RET_HEADS, big, 1), np.float32)
    kd[:, :c, 0] = k_dec
    cd = c_dec.astype(np.float32).reshape(RET_HEADS, 1, 1)
    return jnp.asarray(intra_p), jnp.asarray(qd), jnp.asarray(kd), jnp.asarray(cd)


def _ret_kernel(*refs, t_rows, hps, has_s0, n_prev):
    it = iter(refs)
    q_ref, k_ref, v_ref, g_ref, intra_ref, qd_ref, kd_ref, cd_ref = (next(it) for _ in range(8))
    s0_ref = next(it) if has_s0 else None
    prev_ref = next(it) if n_prev else None
    o_ref, s_ref, st = next(it), next(it), next(it)
    st[...] = s0_ref[...] if has_s0 else jnp.zeros_like(st)
    c = RET_CHUNK
    t_pad = max(t_rows, c)
    dk = q_ref.shape[1] // hps
    dv = v_ref.shape[1] // hps

    def chunk(ci, carry):
        if t_rows >= c:
            rows = pl.ds(pl.multiple_of(ci * c, c), c)
        else:
            rows = pl.ds(0, t_rows)
        for hj in range(hps):
            kcols = slice(hj * dk, (hj + 1) * dk)
            vcols = slice(hj * dv, (hj + 1) * dv)
            qc = _pad_rows(q_ref[rows, kcols], c).astype(BF16)
            kf = _pad_rows(k_ref[rows, kcols], c).astype(F32)
            vc = _pad_rows(v_ref[rows, vcols], c).astype(BF16)
            s = st[hj]
            att = _dot_nt(qc, kf.astype(BF16)) * intra_ref[hj]
            o = _dot(att.astype(BF16), vc) + qd_ref[hj] * _dot(qc, s.astype(BF16))
            st[hj] = s * cd_ref[hj] + _dot_tn((kf * kd_ref[hj]).astype(BF16), vc)
            o = o * lax.rsqrt(jnp.mean(o * o, axis=-1, keepdims=True) + EPS)
            gate = g_ref[rows, vcols].astype(F32)
            o_ref[rows, vcols] = (o[: gate.shape[0]] * _silu(gate)).astype(o_ref.dtype)
        return carry

    n_chunks = t_pad // c
    lax.fori_loop(0, n_chunks, chunk, 0, unroll=2 if n_chunks % 2 == 0 else 1)
    if n_prev:
        s_ref[:n_prev] = prev_ref[...]
    s_ref[n_prev] = st[...]


def _retention(p, s0, slot, batch, t_rows, out_dtype, prev):
    m, n = p.shape
    h = RET_HEADS
    dk = n // (6 * h)
    dv = 2 * dk
    c_eff = math.gcd(t_rows, RET_CHUNK)
    assert t_rows % c_eff == 0 and (t_rows >= RET_CHUNK or t_rows == c_eff)
    intra, qd, kd, cd = _ret_tables(c_eff, dk)
    has_s0 = s0 is not None
    hps = 2 if t_rows > RET_CHUNK else h
    n_hg = h // hps
    in_specs = [
        pl.BlockSpec((t_rows, hps * dk), lambda b, hg: (b, hg)),
        pl.BlockSpec((t_rows, hps * dk), lambda b, hg: (b, n_hg + hg)),
        pl.BlockSpec((t_rows, hps * dv), lambda b, hg: (b, n_hg + hg)),
        pl.BlockSpec((t_rows, hps * dv), lambda b, hg: (b, 2 * n_hg + hg)),
        pl.BlockSpec((hps, RET_CHUNK, RET_CHUNK), lambda b, hg: (hg, 0, 0)),
        pl.BlockSpec((hps, RET_CHUNK, 1), lambda b, hg: (hg, 0, 0)),
        pl.BlockSpec((hps, RET_CHUNK, 1), lambda b, hg: (hg, 0, 0)),
        pl.BlockSpec((hps, 1, 1), lambda b, hg: (hg, 0, 0)),
    ]
    args = [p, p, p, p, intra, qd, kd, cd]
    if has_s0:
        in_specs.append(pl.BlockSpec((None, None, hps, dk, dv), lambda b, hg: (slot, b, hg, 0, 0)))
        args.append(s0)
    n_prev = 0 if prev is None else prev.shape[0]
    if n_prev:
        in_specs.append(pl.BlockSpec((n_prev, None, hps, dk, dv), lambda b, hg: (0, b, hg, 0, 0)))
        args.append(prev)
    return pl.pallas_call(
        functools.partial(_ret_kernel, t_rows=t_rows, hps=hps, has_s0=has_s0, n_prev=n_prev),
        out_shape=(jax.ShapeDtypeStruct((m, h * dv), out_dtype),
                   jax.ShapeDtypeStruct((n_prev + 1, batch, h, dk, dv), F32)),
        grid=(batch, n_hg),
        in_specs=in_specs,
        out_specs=(pl.BlockSpec((t_rows, hps * dv), lambda b, hg: (b, hg)),
                   pl.BlockSpec((n_prev + 1, None, hps, dk, dv), lambda b, hg: (0, b, hg, 0, 0))),
        scratch_shapes=[pltpu.VMEM((hps, dk, dv), F32)],
        compiler_params=_cparams(("parallel", "parallel")),
        name="retention",
    )(*args)


def _hgrn_kernel(*refs, layer, t_rows, chunk, hps, has_s0):
    n_in = 7 if has_s0 else 6
    q_ref, f_ref, i_ref, g_ref, lbl_ref, og_ref = refs[:6]
    o_ref, s_ref = refs[n_in:n_in + 2]
    scratch = refs[n_in + 2:]
    scratch_sets = [scratch[3 * n:3 * n + 3] for n in range(len(scratch) // 3)]
    c = chunk
    r = HG_SUB
    n_sub = c // r
    n_chunks = max(t_rows, c) // c
    dk = q_ref.shape[1] // hps

    logits = lbl_ref[...]
    e = jnp.exp(logits - jnp.max(logits, axis=0, keepdims=True))
    prob = e / jnp.sum(e, axis=0, keepdims=True)
    lrow = lax.broadcasted_iota(jnp.int32, prob.shape, 0)
    in_lb = (lrow >= 1) & (lrow <= layer)
    lb_all = jnp.sum(jnp.where(in_lb, prob, 0.0), axis=0, keepdims=True)
    one_m_lb_all = jnp.sum(jnp.where(in_lb, 0.0, prob), axis=0, keepdims=True)

    ri = lax.broadcasted_iota(jnp.int32, (c, LANES), 0)
    cj = lax.broadcasted_iota(jnp.int32, (c, LANES), 1)
    tri_mask = ri >= cj
    tri = jnp.where(tri_mask, 1.0, 0.0).astype(BF16)
    row_c = lax.broadcasted_iota(jnp.int32, (c, dk), 0)
    row_r = lax.broadcasted_iota(jnp.int32, (r, dk), 0)
    og = og_ref[...]

    def one_chunk(ci, hj, s_t, b_scr, k_scr, v_scr, factored):
        if t_rows >= c:
            rows = pl.ds(pl.multiple_of(ci * c, c), c)
        else:
            rows = pl.ds(0, t_rows)
        cols = slice(hj * dk, (hj + 1) * dk)
        lb = lb_all[:, cols]
        one_m_lb = one_m_lb_all[:, cols]
        z = _pad_rows(f_ref[rows, cols].astype(F32), c)
        qraw = _pad_rows(q_ref[rows, cols].astype(F32), c)
        v = _pad_rows(i_ref[rows, cols].astype(F32), c)
        f = lb + one_m_lb * jax.nn.sigmoid(z)
        logf = jnp.log(f)
        kk = one_m_lb * jax.nn.sigmoid(-z)
        if t_rows < c:
            valid = row_c < t_rows
            logf = jnp.where(valid, logf, 0.0)
            kk = jnp.where(valid, kk, 0.0)
        lf_hi, lf_mid, lf_lo = _split_bf16(_pad_rows(logf, LANES), 3)
        b = (_dot(tri, lf_hi) + (_dot(tri, lf_mid) + _dot(tri, lf_lo))) * LOG2E
        qs = _silu(qraw)
        vb = v.astype(BF16)
        o = _dot_nt((qs * jnp.exp2(b)).astype(BF16), s_t.astype(BF16))

        def sub_block_rows(rq, with_diagonal):
            a_rows = []
            for lo in range(0, c, rq):
                hi = lo + rq if with_diagonal else lo
                if hi == 0:
                    a_rows.append(jnp.zeros((rq, c), F32))
                    continue
                b0 = b[lo - 1:lo, :] if lo else jnp.zeros((1, dk), F32)
                qh = qs[lo:lo + rq, :] * jnp.exp2(b[lo:lo + rq, :] - b0)
                kh = kk[:hi, :] * jnp.exp2(b0 - b[:hi, :])
                a_rows.append(_dot_nt(qh.astype(BF16), _pad_rows(kh, c).astype(BF16)))
            return jnp.concatenate(a_rows, axis=0)

        def intra_factored():
            att = jnp.where(tri_mask, sub_block_rows(HG_FACTORED_SUB, True), 0.0)
            return _dot(att.astype(BF16), vb)

        def intra_pairwise():
            b_scr[...] = b
            k_scr[...] = kk
            v_scr[...] = v
            o_diag = []
            for sb in range(n_sub):
                lo = sb * r
                bi = b[lo:lo + r, :]
                qi = qs[lo:lo + r, :]
                acc = jnp.zeros((r, dk), F32)
                for s in range(min(r, t_rows)):
                    bs = b_scr[lo + s:lo + s + 1, :]
                    ks = k_scr[lo + s:lo + s + 1, :]
                    vs = v_scr[lo + s:lo + s + 1, :]
                    d = bi - bs
                    if s > 0:
                        d = jnp.where(row_r >= s, d, NEG_INF)
                    col = jnp.sum(qi * jnp.exp2(d) * ks, axis=-1, keepdims=True)
                    acc = acc + col * vs
                o_diag.append(acc)
            out = jnp.concatenate(o_diag, axis=0)
            if n_sub > 1:
                out = out + _dot(sub_block_rows(r, False).astype(BF16), vb)
            return out

        o = o + (intra_factored() if factored else intra_pairwise())

        b_end = b[c - 1:c, :]
        kd = kk * jnp.exp2(b_end - b)
        s_new = s_t * jnp.exp2(b_end) + _dot_tn(_pad_rows(vb, LANES), _pad_rows(kd.astype(BF16), LANES))

        o = o * lax.rsqrt(jnp.mean(o * o, axis=-1, keepdims=True) + EPS) * og
        gate = g_ref[rows, cols].astype(F32)
        o_ref[rows, cols] = (o[: gate.shape[0]] * _silu(gate)).astype(o_ref.dtype)
        return s_new

    s_init = tuple(refs[6][hj].T if has_s0 else jnp.zeros((dk, dk), F32) for hj in range(hps))

    def run(factored):
        if n_chunks == 1:
            return tuple(one_chunk(0, hj, s_init[hj], *scratch_sets[hj], factored) for hj in range(hps))

        def pair(i, states):
            states = tuple(one_chunk(2 * i, hj, states[hj], *scratch_sets[2 * hj], factored)
                           for hj in range(hps))
            return tuple(one_chunk(2 * i + 1, hj, states[hj], *scratch_sets[2 * hj + 1], factored)
                         for hj in range(hps))

        return lax.fori_loop(0, n_chunks // 2, pair, s_init)

    if n_sub > 1:
        factored_ok = jnp.min(lb_all) >= 2.0 ** (-HG_FACTORED_MAX_LOG2_SPAN / HG_FACTORED_SUB)
        s_fin = lax.cond(factored_ok, functools.partial(run, True), functools.partial(run, False))
    else:
        s_fin = run(False)
    for hj in range(hps):
        s_ref[hj] = s_fin[hj].T


def _hgrn(pq, pf, pig, lb_logits, o_gain, s0, slot, layer, batch, t_rows, out_dtype):
    m, n = pq.shape
    h = HG_HEADS
    dk = HG_DK
    dv = n // h
    depth = lb_logits.shape[0]
    chunk = HG_CHUNK if t_rows >= HG_CHUNK else HG_SUB
    n_chunks = max(t_rows, chunk) // chunk
    assert dv == dk == LANES and (t_rows % chunk == 0 or t_rows < chunk) and (n_chunks == 1 or n_chunks % 2 == 0)
    has_s0 = s0 is not None
    hps = 2 if n_chunks > 1 else h
    n_hg = h // hps
    wk = hps * dk
    in_specs = [
        pl.BlockSpec((t_rows, wk), lambda b, hg: (b, hg)),
        pl.BlockSpec((t_rows, wk), lambda b, hg: (b, hg)),
        pl.BlockSpec((t_rows, wk), lambda b, hg: (b, hg)),
        pl.BlockSpec((t_rows, wk), lambda b, hg: (b, n_hg + hg)),
        pl.BlockSpec((depth, wk), lambda b, hg: (0, hg)),
        pl.BlockSpec((1, dv), lambda b, hg: (0, 0)),
    ]
    args = [pq, pf, pig, pig, lb_logits, o_gain.reshape(1, dv)]
    if has_s0:
        in_specs.append(pl.BlockSpec((None, None, hps, dk, dv), lambda b, hg: (slot, b, hg, 0, 0)))
        args.append(s0)
    n_sets = hps if n_chunks == 1 else 2 * hps
    return pl.pallas_call(
        functools.partial(_hgrn_kernel, layer=layer, t_rows=t_rows, chunk=chunk, hps=hps, has_s0=has_s0),
        out_shape=(jax.ShapeDtypeStruct((m, h * dv), out_dtype),
                   jax.ShapeDtypeStruct((batch, h, dk, dv), F32)),
        grid=(batch, n_hg),
        in_specs=in_specs,
        out_specs=(pl.BlockSpec((t_rows, wk), lambda b, hg: (b, hg)),
                   pl.BlockSpec((None, hps, dk, dv), lambda b, hg: (b, hg, 0, 0))),
        scratch_shapes=[pltpu.VMEM((chunk, dk), F32)] * (3 * n_sets),
        compiler_params=_cparams(("parallel", "parallel")),
        name="hgrn2",
    )(*args)


def _head_norm(x, gain):
    return x * lax.rsqrt(jnp.mean(x * x, axis=-1, keepdims=True) + EPS) * gain


def _alibi_slope(h_index, n_heads):
    hv = jnp.full((1, 1), h_index + 1, jnp.int32).astype(F32)
    return jnp.exp2(hv * (-8.0 / n_heads))


def _moba_prompt_kernel(q_ref, k_ref, v_ref, g_ref, qg_ref, kg_ref, o_ref, kn_ref,
                        ka_scr, vb_scr, km_scr, *, n_blocks):
    blk = MOBA_BLOCK
    h = pl.program_id(1)
    hd = q_ref.shape[1]
    t_all = k_ref.shape[0]

    kn = _head_norm(k_ref[...], kg_ref[...])
    kn_ref[...] = kn
    row_blk = lax.broadcasted_iota(jnp.int32, (t_all, hd), 0) // blk
    lane_k = lax.broadcasted_iota(jnp.int32, (t_all, hd), 1)
    one_hot = jnp.where(lane_k == row_blk, 1.0, 0.0).astype(BF16)
    ka_scr[...] = jnp.concatenate([kn.astype(BF16), one_hot], axis=1)
    vb_scr[...] = v_ref[...].astype(BF16)
    means = [jnp.mean(kn[n * blk:(n + 1) * blk, :], axis=0, keepdims=True) for n in range(n_blocks)]
    km_scr[...] = _pad_rows(jnp.concatenate(means, axis=0), LANES)

    slope2 = _alibi_slope(h, MOBA_HEADS) * LOG2E
    key_off = lax.broadcasted_iota(jnp.int32, (1, blk), 1).astype(F32)
    ti = lax.broadcasted_iota(jnp.int32, (blk, blk), 0)
    sj = lax.broadcasted_iota(jnp.int32, (blk, blk), 1)

    def tile(k):
        gated = k > MOBA_TOPK
        rows = slice(k * blk, (k + 1) * blk)
        qn = _head_norm(q_ref[rows, :], qg_ref[...])
        qs = (qn * (hd ** -0.5 * LOG2E)).astype(BF16)
        if gated:
            gate = _dot_nt_3pass(qn, km_scr[...])
            lane = lax.broadcasted_iota(jnp.int32, gate.shape, 1)
            gm = jnp.where(lane < k, gate, NEG_INF)
            rank = jnp.zeros(gate.shape, jnp.int32)
            for sh in range(1, k):
                other = pltpu.roll(gm, sh, axis=1)
                ahead = (other > gm) | ((other == gm) & (lane >= sh))
                rank = rank + ahead.astype(jnp.int32)
            drop = jnp.where((lane < k) & (rank >= MOBA_TOPK), MASK_NEG, 0.0).astype(BF16)
            q_aug = jnp.concatenate([qs, drop], axis=1)
        scores = []
        for n in range(k):
            keys = ka_scr[n * blk:(n + 1) * blk, :] if gated else ka_scr[n * blk:(n + 1) * blk, :hd]
            bias = slope2 * (key_off + float((n - k) * blk))
            scores.append(_dot_nt(q_aug if gated else qs, keys) + bias)
        s_own = _dot_nt(qs, ka_scr[k * blk:(k + 1) * blk, :hd]) + slope2 * key_off
        scores.append(jnp.where(ti >= sj, s_own, NEG_INF))

        m_el = scores[0]
        for s in scores[1:]:
            m_el = jnp.maximum(m_el, s)
        m = jnp.broadcast_to(jnp.max(m_el, axis=-1, keepdims=True), (blk, blk))
        l_el = jnp.zeros((blk, blk), F32)
        acc = jnp.zeros((blk, hd), F32)
        for n, s in enumerate(scores):
            p = jnp.exp2(s - m)
            l_el = l_el + p
            acc = acc + _dot(p.astype(BF16), vb_scr[n * blk:(n + 1) * blk, :])
        out = acc / jnp.sum(l_el, axis=-1, keepdims=True)
        o_ref[rows, :] = (out * _silu(g_ref[rows, :].astype(F32))).astype(o_ref.dtype)

    for k in range(n_blocks):
        tile(k)


def _moba_prompt(pqk, pv, pg, q_gain, k_gain, batch, t_rows, out_dtype):
    m, n = pqk.shape
    h = MOBA_HEADS
    hd = MOBA_HD
    assert n == 2 * h * hd and t_rows % MOBA_BLOCK == 0
    nb = t_rows // MOBA_BLOCK
    blk = MOBA_BLOCK
    return pl.pallas_call(
        functools.partial(_moba_prompt_kernel, n_blocks=nb),
        out_shape=(jax.ShapeDtypeStruct((m, h * hd), out_dtype),
                   jax.ShapeDtypeStruct((m, h * hd), F32)),
        grid=(batch, h),
        in_specs=[
            pl.BlockSpec((t_rows, hd), lambda b, hh: (b, hh)),
            pl.BlockSpec((t_rows, hd), lambda b, hh: (b, h + hh)),
            pl.BlockSpec((t_rows, hd), lambda b, hh: (b, hh)),
            pl.BlockSpec((t_rows, hd), lambda b, hh: (b, hh)),
            pl.BlockSpec((1, hd), lambda b, hh: (0, 0)),
            pl.BlockSpec((1, hd), lambda b, hh: (0, 0)),
        ],
        out_specs=(pl.BlockSpec((t_rows, hd), lambda b, hh: (b, hh)),
                   pl.BlockSpec((t_rows, hd), lambda b, hh: (b, hh))),
        scratch_shapes=[
            pltpu.VMEM((t_rows, 2 * hd), BF16),
            pltpu.VMEM((t_rows, hd), BF16),
            pltpu.VMEM((LANES, hd), F32),
        ],
        compiler_params=_cparams(("parallel", "parallel")),
        name="moba_prompt",
    )(pqk, pqk, pv, pg, q_gain.reshape(1, hd), k_gain.reshape(1, hd))


KMEAN_PAGES_PER_STEP = 8


def _kmean_kernel(pt_ref, *refs):
    del pt_ref
    page_refs, o_ref = refs[:-1], refs[-1]
    n = pl.program_id(1)
    blocks_per_step = len(page_refs) // 2
    for j in range(blocks_per_step):
        tot = jnp.sum(page_refs[2 * j][...], axis=0) + jnp.sum(page_refs[2 * j + 1][...], axis=0)
        o_ref[n * blocks_per_step + j] = tot * (1.0 / MOBA_BLOCK)


def _moba_kmean(cache, page_table):
    _, page, heads, hd = cache.shape
    db, n_pages = page_table.shape
    pps = math.gcd(n_pages, KMEAN_PAGES_PER_STEP)
    assert MOBA_BLOCK == 2 * page and pps % 2 == 0
    nb = n_pages // 2

    def page_spec(j):
        return pl.BlockSpec((None, page, heads, hd), lambda b, n, pt: (pt[b, pps * n + j], 0, 0, 0))

    return pl.pallas_call(
        _kmean_kernel,
        out_shape=jax.ShapeDtypeStruct((db, nb, heads, hd), F32),
        grid_spec=pltpu.PrefetchScalarGridSpec(
            num_scalar_prefetch=1,
            grid=(db, n_pages // pps),
            in_specs=[page_spec(j) for j in range(pps)],
            out_specs=pl.BlockSpec((None, nb, heads, hd), lambda b, n, pt: (b, 0, 0, 0)),
        ),
        compiler_params=_cparams(("parallel", "arbitrary")),
        name="moba_past_block_means",
    )(page_table, *([cache] * pps))


def _moba_select_kernel(q_ref, k_ref, km_ref, qg_ref, kg_ref, qn_ref, kn_ref, idx_ref, *, n_blocks):
    hd = MOBA_HD
    t = q_ref.shape[0]
    lane = lax.broadcasted_iota(jnp.int32, (t, LANES), 1)
    for h in range(MOBA_HEADS):
        cols = slice(h * hd, (h + 1) * hd)
        qn = _head_norm(q_ref[:, cols], qg_ref[...])
        kn = _head_norm(k_ref[:, cols], kg_ref[...])
        qn_ref[:, cols] = qn
        kn_ref[:, cols] = kn
        km = _pad_rows(km_ref[:, h, :], LANES)
        gate = _dot_nt_3pass(qn, km)
        gm = jnp.where(lane < n_blocks, gate, NEG_INF)
        picks = jnp.zeros((t, LANES), jnp.int32)
        for j in range(MOBA_TOPK):
            best = jnp.max(gm, axis=-1, keepdims=True)
            ix = jnp.min(jnp.where(gm == best, lane, LANES), axis=-1, keepdims=True)
            picks = jnp.where(lane == j, ix, picks)
            gm = jnp.where(lane == ix, NEG_INF, gm)
        idx_ref[h] = picks


def _moba_select(p, kmean, q_gain, k_gain, batch, t_rows):
    m, n = p.shape
    w = MOBA_HEADS * MOBA_HD
    nb = kmean.shape[1]
    assert nb >= MOBA_TOPK and nb <= LANES
    return pl.pallas_call(
        functools.partial(_moba_select_kernel, n_blocks=nb),
        out_shape=(jax.ShapeDtypeStruct((m, w), F32),
                   jax.ShapeDtypeStruct((m, w), F32),
                   jax.ShapeDtypeStruct((batch, MOBA_HEADS, t_rows, LANES), jnp.int32)),
        grid=(batch,),
        in_specs=[
            pl.BlockSpec((t_rows, w), lambda b: (b, 0)),
            pl.BlockSpec((t_rows, w), lambda b: (b, 1)),
            pl.BlockSpec((None, nb, MOBA_HEADS, MOBA_HD), lambda b: (b, 0, 0, 0)),
            pl.BlockSpec((1, MOBA_HD), lambda b: (0, 0)),
            pl.BlockSpec((1, MOBA_HD), lambda b: (0, 0)),
        ],
        out_specs=(pl.BlockSpec((t_rows, w), lambda b: (b, 0)),
                   pl.BlockSpec((t_rows, w), lambda b: (b, 0)),
                   pl.BlockSpec((None, MOBA_HEADS, t_rows, LANES), lambda b: (b, 0, 0, 0))),
        compiler_params=_cparams(("parallel",)),
        name="moba_sample_select",
    )(p, p, kmean, q_gain.reshape(1, MOBA_HD), k_gain.reshape(1, MOBA_HD))


def _moba_sample_attn_kernel(idx_ref, pt_ref, ck_ref, cv_ref, qn_ref, kn_ref, vn_ref, g_ref, o_ref,
                             kbuf, vbuf, sem, *, past_len, t_rows):
    step = pl.program_id(0)
    n_steps = pl.num_programs(0)
    h = step % MOBA_HEADS
    hd = MOBA_HD
    blk = MOBA_BLOCK
    page = blk // 2
    n_sel = t_rows * MOBA_TOPK
    scale = hd ** -0.5
    slope = _alibi_slope(h, MOBA_HEADS)

    def for_each_copy(s, slot, fn):
        sb = s // MOBA_HEADS
        sh = s % MOBA_HEADS

        def body(u, carry):
            blk_id = idx_ref[s * n_sel + u]
            for half in range(2):
                pg = pt_ref[sb, 2 * blk_id + half]
                rows = pl.ds(half * page, page)
                fn(pltpu.make_async_copy(ck_ref.at[pg, :, sh, :], kbuf.at[slot, u, rows, :], sem.at[slot]))
                fn(pltpu.make_async_copy(cv_ref.at[pg, :, sh, :], vbuf.at[slot, u, rows, :], sem.at[slot]))
            return carry

        lax.fori_loop(0, n_sel, body, 0)

    slot = step % 2

    @pl.when(step == 0)
    def _():
        for_each_copy(step, slot, lambda cp: cp.start())

    @pl.when(step + 1 < n_steps)
    def _():
        for_each_copy(step + 1, 1 - slot, lambda cp: cp.start())

    for_each_copy(step, slot, lambda cp: cp.wait())

    offs = lax.broadcasted_iota(jnp.int32, (1, blk), 1)
    key_i = lax.broadcasted_iota(jnp.int32, (t_rows, 1), 0)
    kn = kn_ref[...]
    vn = vn_ref[...]

    for qi in range(t_rows):
        qrow = qn_ref[qi:qi + 1, :]
        q8 = jnp.broadcast_to(qrow, (8, hd)).astype(BF16)
        t_pos = past_len + qi
        s_own = jnp.sum(kn * qrow, axis=-1, keepdims=True) * scale - slope * (qi - key_i).astype(F32)
        s_own = jnp.where(key_i <= qi, s_own, NEG_INF)
        m = jnp.max(s_own, axis=0, keepdims=True)
        scores = []
        for j in range(MOBA_TOPK):
            u = qi * MOBA_TOPK + j
            dist = (t_pos - idx_ref[step * n_sel + u] * blk - offs).astype(F32)
            s = _dot_nt(q8, kbuf[slot, u].astype(BF16))[0:1, :] * scale - slope * dist
            scores.append(s)
            m = jnp.maximum(m, jnp.max(s, axis=-1, keepdims=True))
        p_own = jnp.exp(s_own - m)
        denom = jnp.sum(p_own, axis=0, keepdims=True)
        out = jnp.sum(p_own * vn, axis=0, keepdims=True)
        for j, s in enumerate(scores):
            u = qi * MOBA_TOPK + j
            pj = jnp.exp(s - m)
            denom = denom + jnp.sum(pj, axis=-1, keepdims=True)
            out = out + _dot(jnp.broadcast_to(pj, (8, blk)).astype(BF16), vbuf[slot, u].astype(BF16))[0:1, :]
        gate = g_ref[qi:qi + 1, :]
        o_ref[qi:qi + 1, :] = out / denom * _silu(gate)


def _moba_sample_attn(idx, page_table, cache_k, cache_v, qn, kn, vn, gate, batch, t_rows):
    h = MOBA_HEADS
    hd = MOBA_HD
    _, page, heads, _ = cache_k.shape
    n_pages = page_table.shape[1]
    assert MOBA_BLOCK == 2 * page and heads == h
    n_sel = t_rows * MOBA_TOPK
    row_spec = pl.BlockSpec((t_rows, hd), lambda s, ix, pt: (s // h, s % h))
    return pl.pallas_call(
        functools.partial(_moba_sample_attn_kernel, past_len=n_pages * page, t_rows=t_rows),
        out_shape=jax.ShapeDtypeStruct((batch * t_rows, h * hd), F32),
        grid_spec=pltpu.PrefetchScalarGridSpec(
            num_scalar_prefetch=2,
            grid=(batch * h,),
            in_specs=[pl.BlockSpec(memory_space=pl.ANY), pl.BlockSpec(memory_space=pl.ANY),
                      row_spec, row_spec, row_spec, row_spec],
            out_specs=row_spec,
            scratch_shapes=[pltpu.VMEM((2, n_sel, MOBA_BLOCK, hd), F32),
                            pltpu.VMEM((2, n_sel, MOBA_BLOCK, hd), F32),
                            pltpu.SemaphoreType.DMA((2,))],
        ),
        compiler_params=_cparams(("arbitrary",)),
        name="moba_sample_attention",
    )(idx, page_table, cache_k, cache_v, qn, kn, vn, gate)


def _retention_layer(hp, hs, gain, w_in, w_out, slot, state, prev_p, prev_s, bp, tp, bs, ts):
    (pp,), (ps,) = _in_proj(hp, hs, gain, w_in, slot, [(w_in.shape[2], BF16)])
    op, st_p = _retention(pp, None, 0, bp, tp, BF16, prev_p)
    o_s, st_s = _retention(ps, state, slot, bs, ts, F32, prev_s)
    hp, hs = _out_proj(op, o_s, w_out, slot, hp, hs)
    return hp, hs, st_p, st_s


def _hgrn_layer(hp, hs, gain, w_in, lb_logits, o_gain, w_out, slot, state, layer, bp, tp, bs, ts):
    w = w_in.shape[2] // 4
    (pq, pf, pig), (sq, sf, sig) = _in_proj(hp, hs, gain, w_in, slot, [(w, BF16), (w, F32), (2 * w, BF16)])
    op, st_p = _hgrn(pq, pf, pig, lb_logits, o_gain, None, 0, layer, bp, tp, BF16)
    o_s, st_s = _hgrn(sq, sf, sig, lb_logits, o_gain, state, slot, layer, bs, ts, F32)
    hp, hs = _out_proj(op, o_s, w_out, slot, hp, hs)
    return hp, hs, st_p, st_s


def _moba_layer(hp, hs, gain, w_in, q_gain, k_gain, w_out, slot, cache_k, cache_v, page_table, bp, tp, bs, ts):
    w = MOBA_HEADS * MOBA_HD
    (pqk, pv, pg), (sqk, sv, sg) = _in_proj(hp, hs, gain, w_in, slot, [(2 * w, F32), (w, F32), (w, BF16)])
    kmean = _moba_kmean(cache_k, page_table)
    qn, kn_s, picks = _moba_select(sqk, kmean, q_gain, k_gain, bs, ts)
    idx = picks[..., :MOBA_TOPK].reshape(-1)
    o_s = _moba_sample_attn(idx, page_table, cache_k, cache_v, qn, kn_s, sv, sg, bs, ts)
    op, kn_p = _moba_prompt(pqk, pv, pg, q_gain, k_gain, bp, tp, BF16)
    hp, hs = _out_proj(op, o_s, w_out, slot, hp, hs)
    return hp, hs, kn_p, pv, kn_s, sv


def kernel(x_prompt, x_sample, state_ret, cache_k, cache_v, state_hgrn, page_table, norm_gain, ret_w_in, ret_w_out, moba_w_in, moba_q_gain, moba_k_gain, moba_w_out, hgrn_w_in, hgrn_lb_logits, hgrn_o_gain, hgrn_w_out):
    bp, tp, d = x_prompt.shape
    bs, ts, _ = x_sample.shape
    depth = norm_gain.shape[0]
    n_pool = cache_k.shape[1]
    hp = x_prompt.reshape(bp * tp, d)
    hs = x_sample.reshape(bs * ts, d)
    ck = cache_k.reshape((-1,) + cache_k.shape[2:])
    cv = cache_v.reshape((-1,) + cache_v.shape[2:])
    kp_l, vp_l, ks_l, vs_l, gp_l, gs_l = [], [], [], [], [], []
    ret_p = ret_s = None
    for layer in range(depth):
        kind, slot = layer % N_MIXERS, layer // N_MIXERS
        gain = norm_gain[layer]
        if kind == 0:
            hp, hs, ret_p, ret_s = _retention_layer(hp, hs, gain, ret_w_in, ret_w_out, slot, state_ret,
                                                    ret_p, ret_s, bp, tp, bs, ts)
        elif kind == 1:
            hp, hs, kp, vp, ks_, vs_ = _moba_layer(hp, hs, gain, moba_w_in, moba_q_gain[slot], moba_k_gain[slot],
                                                   moba_w_out, slot, ck, cv, page_table + slot * n_pool,
                                                   bp, tp, bs, ts)
            kp, vp, hp = lax.optimization_barrier((kp, vp, hp))
            shp = (MOBA_HEADS, MOBA_HD)
            kp_l.append(kp.reshape(bp, tp, *shp))
            vp_l.append(vp.reshape(bp, tp, *shp))
            ks_l.append(ks_.reshape(bs, ts, *shp))
            vs_l.append(vs_.reshape(bs, ts, *shp))
        else:
            hp, hs, st_p, st_s = _hgrn_layer(hp, hs, gain, hgrn_w_in, hgrn_lb_logits, hgrn_o_gain[slot],
                                             hgrn_w_out, slot, state_hgrn, layer, bp, tp, bs, ts)
            gp_l.append(st_p)
            gs_l.append(st_s)
    return (hp.reshape(bp, tp, d), hs.reshape(bs, ts, d),
            jnp.stack(kp_l), jnp.stack(vp_l), jnp.stack(ks_l), jnp.stack(vs_l),
            ret_p, ret_s, jnp.stack(gp_l), jnp.stack(gs_l))
```

```python
import functools
import math

import numpy as np
import jax
import jax.numpy as jnp
from jax import lax
from jax.experimental import pallas as pl
from jax.experimental.pallas import tpu as pltpu

F32 = jnp.float32
BF16 = jnp.bfloat16
EPS = 1e-6
NEG_INF = float("-inf")
LOG2E = 1.4426950408889634
MASK_NEG = -1e30

N_MIXERS = 3
RET_HEADS = 8
RET_CHUNK = 128
MOBA_HEADS = 16
MOBA_HD = 128
MOBA_BLOCK = 256
MOBA_TOPK = 3
HG_HEADS = 16
HG_DK = 128
HG_CHUNK = 128
HG_SUB = 16
HG_FACTORED_SUB = 64
HG_FACTORED_MAX_LOG2_SPAN = 100.0

VMEM_LIMIT_BYTES = 52 * 1024 * 1024
LANES = 128


def _cparams(sem):
    return pltpu.CompilerParams(dimension_semantics=sem, vmem_limit_bytes=VMEM_LIMIT_BYTES)


def _dot(a, b):
    return jnp.dot(a, b, preferred_element_type=F32)


def _dot_nt(a, b):
    return lax.dot_general(a, b, (((1,), (1,)), ((), ())), preferred_element_type=F32)


def _dot_tn(a, b):
    return lax.dot_general(a, b, (((0,), (0,)), ((), ())), preferred_element_type=F32)


def _split_bf16(x, parts):
    pieces = []
    for _ in range(parts):
        hi = x.astype(BF16)
        pieces.append(hi)
        x = x - hi.astype(F32)
    return pieces


def _dot_nt_3pass(a, b):
    a_hi, a_lo = _split_bf16(a, 2)
    b_hi, b_lo = _split_bf16(b, 2)
    return _dot_nt(a_hi, b_hi) + (_dot_nt(a_hi, b_lo) + _dot_nt(a_lo, b_hi))


def _silu(x):
    return x * jax.nn.sigmoid(x)


def _pad_rows(x, rows):
    if x.shape[0] == rows:
        return x
    return jnp.concatenate([x, jnp.zeros((rows - x.shape[0], x.shape[1]), x.dtype)], axis=0)


PROJ_TM, PROJ_TN = 2048, 256
PROJ_SUB_ROWS = 512


def _proj_kernel(*refs, n_i, n_groups, bounds, norm, residual):
    it = iter(refs)
    xp_ref, xs_ref = next(it), next(it)
    gain_ref = next(it) if norm else None
    w_first_ref, w_next_ref = next(it), next(it)
    res_p = [next(it) for _ in range(n_groups)] if residual else None
    res_s = [next(it) for _ in range(n_groups)] if residual else None
    outs_p = [next(it) for _ in range(n_groups)]
    outs_s = [next(it) for _ in range(n_groups)]
    wb_even, wb_odd = next(it), next(it)
    xn_ref = next(it) if norm else None
    i = pl.program_id(0)
    j = pl.program_id(1)
    ms_rows = xs_ref.shape[0]

    if norm:
        def normed(x):
            return (x * lax.rsqrt(jnp.mean(x * x, axis=-1, keepdims=True) + EPS) * gain_ref[...]).astype(BF16)

        @pl.when((j == 0) & (i < n_i))
        def _():
            xn_ref[...] = normed(xp_ref[...])

        @pl.when((j == 0) & (i == n_i))
        def _():
            xn_ref[:ms_rows, :] = normed(xs_ref[...])

    @pl.when((i == 0) & (j == 0))
    def _():
        wb_even[...] = w_first_ref[...].astype(BF16)

    def step(w_cur, w_nxt):
        for g, (j0, j1) in enumerate(bounds):
            in_group = (j >= j0) & (j < j1)

            @pl.when(in_group & (i < n_i))
            def _(g=g):
                w_nxt[...] = w_next_ref[...].astype(BF16)
                lhs_ref = xn_ref if norm else xp_ref
                rows_total = lhs_ref.shape[0]
                sub = min(PROJ_SUB_ROWS, rows_total)
                for r0 in range(0, rows_total, sub):
                    acc = _dot(lhs_ref[r0:r0 + sub, :], w_cur[...])
                    if residual:
                        acc = acc + res_p[g][r0:r0 + sub, :]
                    outs_p[g][r0:r0 + sub, :] = acc.astype(outs_p[g].dtype)

            @pl.when(in_group & (i == n_i))
            def _(g=g):
                w_nxt[...] = w_next_ref[...].astype(BF16)
                acc = _dot(xn_ref[:ms_rows, :] if norm else xs_ref[...].astype(BF16), w_cur[...])
                if residual:
                    acc = acc + res_s[g][...]
                outs_s[g][...] = acc

    pl.when(j % 2 == 0)(functools.partial(step, wb_even, wb_odd))
    pl.when(j % 2 == 1)(functools.partial(step, wb_odd, wb_even))


def _proj(xp, xs, w, slot, groups, gain=None, residual=None, name="proj"):
    mp, d = xp.shape
    ms = xs.shape[0]
    n = w.shape[2]
    tm, tn = min(PROJ_TM, mp), PROJ_TN
    n_i, n_j = mp // tm, n // tn
    assert mp % tm == 0 and ms <= tm and n % tn == 0 and n_j % 2 == 0
    assert sum(c for c, _ in groups) == n and all(c % tn == 0 for c, _ in groups)
    assert residual is None or len(groups) == 1
    norm = gain is not None
    bounds, j0 = [], 0
    for c, _ in groups:
        bounds.append((j0, j0 + c // tn))
        j0 += c // tn

    def prompt_map(j0, j1):
        def imap(i, j):
            jj = jnp.where(i < n_i, j, n_j - 1)
            return (jnp.minimum(i, n_i - 1), jnp.clip(jj - j0, 0, j1 - j0 - 1))
        return imap

    def sample_map(j0, j1):
        def imap(i, j):
            jj = jnp.where(i < n_i, 0, j)
            return (0, jnp.clip(jj - j0, 0, j1 - j0 - 1))
        return imap

    p_specs = [pl.BlockSpec((tm, tn), prompt_map(*b)) for b in bounds]
    s_specs = [pl.BlockSpec((ms, tn), sample_map(*b)) for b in bounds]
    in_specs = [
        pl.BlockSpec((tm, d), lambda i, j: (jnp.minimum(i, n_i - 1), 0), pipeline_mode=pl.Buffered(1)),
        pl.BlockSpec((ms, d), lambda i, j: (0, 0)),
    ]
    args = [xp, xs]
    if norm:
        in_specs.append(pl.BlockSpec((1, d), lambda i, j: (0, 0)))
        args.append(gain.reshape(1, d))
    in_specs += [
        pl.BlockSpec((None, d, tn), lambda i, j: (slot, 0, 0), pipeline_mode=pl.Buffered(1)),
        pl.BlockSpec((None, d, tn), lambda i, j: (slot, 0, (j + 1) % n_j)),
    ]
    args += [w, w]
    if residual is not None:
        in_specs += p_specs + s_specs
        args += list(residual)
    scratch = [pltpu.VMEM((d, tn), BF16), pltpu.VMEM((d, tn), BF16)]
    if norm:
        scratch.append(pltpu.VMEM((tm, d), BF16))
    outs = pl.pallas_call(
        functools.partial(_proj_kernel, n_i=n_i, n_groups=len(groups), bounds=tuple(bounds),
                          norm=norm, residual=residual is not None),
        out_shape=([jax.ShapeDtypeStruct((mp, c), dt) for c, dt in groups]
                   + [jax.ShapeDtypeStruct((ms, c), F32) for c, _ in groups]),
        grid=(n_i + 1, n_j),
        in_specs=in_specs,
        out_specs=p_specs + s_specs,
        scratch_shapes=scratch,
        compiler_params=_cparams(("arbitrary", "arbitrary")),
        name=name,
    )(*args)
    return outs[:len(groups)], outs[len(groups):]


def _in_proj(xp, xs, gain, w, slot, groups):
    return _proj(xp, xs, w, slot, groups, gain=gain, name="norm_in_proj")


def _out_proj(ap, a_s, w, slot, rp, rs):
    (op,), (o_s,) = _proj(ap, a_s, w, slot, [(w.shape[2], F32)], residual=(rp, rs), name="out_proj_residual")
    return op, o_s


def _ret_tables(c, dk):
    scale = dk ** -0.5
    log_g = np.log1p(-np.exp2(-5.0 - np.arange(RET_HEADS, dtype=np.float64)))
    i = np.arange(c, dtype=np.float64)
    rel = i[:, None] - i[None, :]
    intra = np.where(rel >= 0, np.exp(log_g[:, None, None] * np.maximum(rel, 0.0)), 0.0) * scale
    q_dec = np.exp(log_g[:, None] * (i + 1.0))
    k_dec = np.exp(log_g[:, None] * (c - 1.0 - i)) * scale
    c_dec = np.exp(log_g * c)
    big = RET_CHUNK
    intra_p = np.zeros((RET_HEADS, big, big), np.float32)
    intra_p[:, :c, :c] = intra
    qd = np.zeros((RET_HEADS, big, 1), np.float32)
    qd[:, :c, 0] = q_dec
    kd = np.zeros((RET_HEADS, big, 1), np.float32)
    kd[:, :c, 0] = k_dec
    cd = c_dec.astype(np.float32).reshape(RET_HEADS, 1, 1)
    return jnp.asarray(intra_p), jnp.asarray(qd), jnp.asarray(kd), jnp.asarray(cd)


def _ret_kernel(*refs, t_rows, hps, has_s0, n_prev):
    it = iter(refs)
    q_ref, k_ref, v_ref, g_ref, intra_ref, qd_ref, kd_ref, cd_ref = (next(it) for _ in range(8))
    s0_ref = next(it) if has_s0 else None
    prev_ref = next(it) if n_prev else None
    o_ref, s_ref, st = next(it), next(it), next(it)
    st[...] = s0_ref[...] if has_s0 else jnp.zeros_like(st)
    c = RET_CHUNK
    t_pad = max(t_rows, c)
    dk = q_ref.shape[1] // hps
    dv = v_ref.shape[1] // hps

    def chunk(ci, carry):
        if t_rows >= c:
            rows = pl.ds(pl.multiple_of(ci * c, c), c)
        else:
            rows = pl.ds(0, t_rows)
        for hj in range(hps):
            kcols = slice(hj * dk, (hj + 1) * dk)
            vcols = slice(hj * dv, (hj + 1) * dv)
            qc = _pad_rows(q_ref[rows, kcols], c).astype(BF16)
            kf = _pad_rows(k_ref[rows, kcols], c).astype(F32)
            vc = _pad_rows(v_ref[rows, vcols], c).astype(BF16)
            s = st[hj]
            att = _dot_nt(qc, kf.astype(BF16)) * intra_ref[hj]
            o = _dot(att.astype(BF16), vc) + qd_ref[hj] * _dot(qc, s.astype(BF16))
            st[hj] = s * cd_ref[hj] + _dot_tn((kf * kd_ref[hj]).astype(BF16), vc)
            o = o * lax.rsqrt(jnp.mean(o * o, axis=-1, keepdims=True) + EPS)
            gate = g_ref[rows, vcols].astype(F32)
            o_ref[rows, vcols] = (o[: gate.shape[0]] * _silu(gate)).astype(o_ref.dtype)
        return carry

    n_chunks = t_pad // c
    lax.fori_loop(0, n_chunks, chunk, 0, unroll=2 if n_chunks % 2 == 0 else 1)
    if n_prev:
        s_ref[:n_prev] = prev_ref[...]
    s_ref[n_prev] = st[...]


def _retention(p, s0, slot, batch, t_rows, out_dtype, prev):
    m, n = p.shape
    h = RET_HEADS
    dk = n // (6 * h)
    dv = 2 * dk
    c_eff = math.gcd(t_rows, RET_CHUNK)
    assert t_rows % c_eff == 0 and (t_rows >= RET_CHUNK or t_rows == c_eff)
    intra, qd, kd, cd = _ret_tables(c_eff, dk)
    has_s0 = s0 is not None
    hps = 2 if t_rows > RET_CHUNK else h
    n_hg = h // hps
    in_specs = [
        pl.BlockSpec((t_rows, hps * dk), lambda b, hg: (b, hg)),
        pl.BlockSpec((t_rows, hps * dk), lambda b, hg: (b, n_hg + hg)),
        pl.BlockSpec((t_rows, hps * dv), lambda b, hg: (b, n_hg + hg)),
        pl.BlockSpec((t_rows, hps * dv), lambda b, hg: (b, 2 * n_hg + hg)),
        pl.BlockSpec((hps, RET_CHUNK, RET_CHUNK), lambda b, hg: (hg, 0, 0)),
        pl.BlockSpec((hps, RET_CHUNK, 1), lambda b, hg: (hg, 0, 0)),
        pl.BlockSpec((hps, RET_CHUNK, 1), lambda b, hg: (hg, 0, 0)),
        pl.BlockSpec((hps, 1, 1), lambda b, hg: (hg, 0, 0)),
    ]
    args = [p, p, p, p, intra, qd, kd, cd]
    if has_s0:
        in_specs.append(pl.BlockSpec((None, None, hps, dk, dv), lambda b, hg: (slot, b, hg, 0, 0)))
        args.append(s0)
    n_prev = 0 if prev is None else prev.shape[0]
    if n_prev:
        in_specs.append(pl.BlockSpec((n_prev, None, hps, dk, dv), lambda b, hg: (0, b, hg, 0, 0)))
        args.append(prev)
    return pl.pallas_call(
        functools.partial(_ret_kernel, t_rows=t_rows, hps=hps, has_s0=has_s0, n_prev=n_prev),
        out_shape=(jax.ShapeDtypeStruct((m, h * dv), out_dtype),
                   jax.ShapeDtypeStruct((n_prev + 1, batch, h, dk, dv), F32)),
        grid=(batch, n_hg),
        in_specs=in_specs,
        out_specs=(pl.BlockSpec((t_rows, hps * dv), lambda b, hg: (b, hg)),
                   pl.BlockSpec((n_prev + 1, None, hps, dk, dv), lambda b, hg: (0, b, hg, 0, 0))),
        scratch_shapes=[pltpu.VMEM((hps, dk, dv), F32)],
        compiler_params=_cparams(("parallel", "parallel")),
        name="retention",
    )(*args)


def _hgrn_kernel(*refs, layer, t_rows, chunk, hps, has_s0):
    n_in = 7 if has_s0 else 6
    q_ref, f_ref, i_ref, g_ref, lbl_ref, og_ref = refs[:6]
    o_ref, s_ref = refs[n_in:n_in + 2]
    scratch = refs[n_in + 2:]
    scratch_sets = [scratch[3 * n:3 * n + 3] for n in range(len(scratch) // 3)]
    c = chunk
    r = HG_SUB
    n_sub = c // r
    n_chunks = max(t_rows, c) // c
    dk = q_ref.shape[1] // hps

    logits = lbl_ref[...]
    e = jnp.exp(logits - jnp.max(logits, axis=0, keepdims=True))
    prob = e / jnp.sum(e, axis=0, keepdims=True)
    lrow = lax.broadcasted_iota(jnp.int32, prob.shape, 0)
    in_lb = (lrow >= 1) & (lrow <= layer)
    lb_all = jnp.sum(jnp.where(in_lb, prob, 0.0), axis=0, keepdims=True)
    one_m_lb_all = jnp.sum(jnp.where(in_lb, 0.0, prob), axis=0, keepdims=True)

    ri = lax.broadcasted_iota(jnp.int32, (c, LANES), 0)
    cj = lax.broadcasted_iota(jnp.int32, (c, LANES), 1)
    tri_mask = ri >= cj
    tri = jnp.where(tri_mask, 1.0, 0.0).astype(BF16)
    row_c = lax.broadcasted_iota(jnp.int32, (c, dk), 0)
    row_r = lax.broadcasted_iota(jnp.int32, (r, dk), 0)
    og = og_ref[...]

    def one_chunk(ci, hj, s_t, b_scr, k_scr, v_scr, factored):
        if t_rows >= c:
            rows = pl.ds(pl.multiple_of(ci * c, c), c)
        else:
            rows = pl.ds(0, t_rows)
        cols = slice(hj * dk, (hj + 1) * dk)
        lb = lb_all[:, cols]
        one_m_lb = one_m_lb_all[:, cols]
        z = _pad_rows(f_ref[rows, cols].astype(F32), c)
        qraw = _pad_rows(q_ref[rows, cols].astype(F32), c)
        v = _pad_rows(i_ref[rows, cols].astype(F32), c)
        f = lb + one_m_lb * jax.nn.sigmoid(z)
        logf = jnp.log(f)
        kk = one_m_lb * jax.nn.sigmoid(-z)
        if t_rows < c:
            valid = row_c < t_rows
            logf = jnp.where(valid, logf, 0.0)
            kk = jnp.where(valid, kk, 0.0)
        lf_hi, lf_mid, lf_lo = _split_bf16(_pad_rows(logf, LANES), 3)
        b = (_dot(tri, lf_hi) + (_dot(tri, lf_mid) + _dot(tri, lf_lo))) * LOG2E
        qs = _silu(qraw)
        vb = v.astype(BF16)
        o = _dot_nt((qs * jnp.exp2(b)).astype(BF16), s_t.astype(BF16))

        def sub_block_rows(rq, with_diagonal):
            a_rows = []
            for lo in range(0, c, rq):
                hi = lo + rq if with_diagonal else lo
                if hi == 0:
                    a_rows.append(jnp.zeros((rq, c), F32))
                    continue
                b0 = b[lo - 1:lo, :] if lo else jnp.zeros((1, dk), F32)
                qh = qs[lo:lo + rq, :] * jnp.exp2(b[lo:lo + rq, :] - b0)
                kh = kk[:hi, :] * jnp.exp2(b0 - b[:hi, :])
                a_rows.append(_dot_nt(qh.astype(BF16), _pad_rows(kh, c).astype(BF16)))
            return jnp.concatenate(a_rows, axis=0)

        def intra_factored():
            att = jnp.where(tri_mask, sub_block_rows(HG_FACTORED_SUB, True), 0.0)
            return _dot(att.astype(BF16), vb)

        def intra_pairwise():
            b_scr[...] = b
            k_scr[...] = kk
            v_scr[...] = v
            o_diag = []
            for sb in range(n_sub):
                lo = sb * r
                bi = b[lo:lo + r, :]
                qi = qs[lo:lo + r, :]
                acc = jnp.zeros((r, dk), F32)
                for s in range(min(r, t_rows)):
                    bs = b_scr[lo + s:lo + s + 1, :]
                    ks = k_scr[lo + s:lo + s + 1, :]
                    vs = v_scr[lo + s:lo + s + 1, :]
                    d = bi - bs
                    if s > 0:
                        d = jnp.where(row_r >= s, d, NEG_INF)
                    col = jnp.sum(qi * jnp.exp2(d) * ks, axis=-1, keepdims=True)
                    acc = acc + col * vs
                o_diag.append(acc)
            out = jnp.concatenate(o_diag, axis=0)
            if n_sub > 1:
                out = out + _dot(sub_block_rows(r, False).astype(BF16), vb)
            return out

        o = o + (intra_factored() if factored else intra_pairwise())

        b_end = b[c - 1:c, :]
        kd = kk * jnp.exp2(b_end - b)
        s_new = s_t * jnp.exp2(b_end) + _dot_tn(_pad_rows(vb, LANES), _pad_rows(kd.astype(BF16), LANES))

        o = o * lax.rsqrt(jnp.mean(o * o, axis=-1, keepdims=True) + EPS) * og
        gate = g_ref[rows, cols].astype(F32)
        o_ref[rows, cols] = (o[: gate.shape[0]] * _silu(gate)).astype(o_ref.dtype)
        return s_new

    s_init = tuple(refs[6][hj].T if has_s0 else jnp.zeros((dk, dk), F32) for hj in range(hps))

    def run(factored):
        if n_chunks == 1:
            return tuple(one_chunk(0, hj, s_init[hj], *scratch_sets[hj], factored) for hj in range(hps))

        def pair(i, states):
            states = tuple(one_chunk(2 * i, hj, states[hj], *scratch_sets[2 * hj], factored)
                           for hj in range(hps))
            return tuple(one_chunk(2 * i + 1, hj, states[hj], *scratch_sets[2 * hj + 1], factored)
                         for hj in range(hps))

        return lax.fori_loop(0, n_chunks // 2, pair, s_init)

    if n_sub > 1:
        factored_ok = jnp.min(lb_all) >= 2.0 ** (-HG_FACTORED_MAX_LOG2_SPAN / HG_FACTORED_SUB)
        s_fin = lax.cond(factored_ok, functools.partial(run, True), functools.partial(run, False))
    else:
        s_fin = run(False)
    for hj in range(hps):
        s_ref[hj] = s_fin[hj].T


def _hgrn(pq, pf, pig, lb_logits, o_gain, s0, slot, layer, batch, t_rows, out_dtype):
    m, n = pq.shape
    h = HG_HEADS
    dk = HG_DK
    dv = n // h
    depth = lb_logits.shape[0]
    chunk = HG_CHUNK if t_rows >= HG_CHUNK else HG_SUB
    n_chunks = max(t_rows, chunk) // chunk
    assert dv == dk == LANES and (t_rows % chunk == 0 or t_rows < chunk) and (n_chunks == 1 or n_chunks % 2 == 0)
    has_s0 = s0 is not None
    hps = 2 if n_chunks > 1 else h
    n_hg = h // hps
    wk = hps * dk
    in_specs = [
        pl.BlockSpec((t_rows, wk), lambda b, hg: (b, hg)),
        pl.BlockSpec((t_rows, wk), lambda b, hg: (b, hg)),
        pl.BlockSpec((t_rows, wk), lambda b, hg: (b, hg)),
        pl.BlockSpec((t_rows, wk), lambda b, hg: (b, n_hg + hg)),
        pl.BlockSpec((depth, wk), lambda b, hg: (0, hg)),
        pl.BlockSpec((1, dv), lambda b, hg: (0, 0)),
    ]
    args = [pq, pf, pig, pig, lb_logits, o_gain.reshape(1, dv)]
    if has_s0:
        in_specs.append(pl.BlockSpec((None, None, hps, dk, dv), lambda b, hg: (slot, b, hg, 0, 0)))
        args.append(s0)
    n_sets = hps if n_chunks == 1 else 2 * hps
    return pl.pallas_call(
        functools.partial(_hgrn_kernel, layer=layer, t_rows=t_rows, chunk=chunk, hps=hps, has_s0=has_s0),
        out_shape=(jax.ShapeDtypeStruct((m, h * dv), out_dtype),
                   jax.ShapeDtypeStruct((batch, h, dk, dv), F32)),
        grid=(batch, n_hg),
        in_specs=in_specs,
        out_specs=(pl.BlockSpec((t_rows, wk), lambda b, hg: (b, hg)),
                   pl.BlockSpec((None, hps, dk, dv), lambda b, hg: (b, hg, 0, 0))),
        scratch_shapes=[pltpu.VMEM((chunk, dk), F32)] * (3 * n_sets),
        compiler_params=_cparams(("parallel", "parallel")),
        name="hgrn2",
    )(*args)


def _head_norm(x, gain):
    return x * lax.rsqrt(jnp.mean(x * x, axis=-1, keepdims=True) + EPS) * gain


def _alibi_slope(h_index, n_heads):
    hv = jnp.full((1, 1), h_index + 1, jnp.int32).astype(F32)
    return jnp.exp2(hv * (-8.0 / n_heads))


def _moba_prompt_kernel(q_ref, k_ref, v_ref, g_ref, qg_ref, kg_ref, o_ref, kn_ref,
                        ka_scr, vb_scr, km_scr, *, n_blocks):
    blk = MOBA_BLOCK
    h = pl.program_id(1)
    hd = q_ref.shape[1]
    t_all = k_ref.shape[0]

    kn = _head_norm(k_ref[...], kg_ref[...])
    kn_ref[...] = kn
    row_blk = lax.broadcasted_iota(jnp.int32, (t_all, hd), 0) // blk
    lane_k = lax.broadcasted_iota(jnp.int32, (t_all, hd), 1)
    one_hot = jnp.where(lane_k == row_blk, 1.0, 0.0).astype(BF16)
    ka_scr[...] = jnp.concatenate([kn.astype(BF16), one_hot], axis=1)
    vb_scr[...] = v_ref[...].astype(BF16)
    means = [jnp.mean(kn[n * blk:(n + 1) * blk, :], axis=0, keepdims=True) for n in range(n_blocks)]
    km_scr[...] = _pad_rows(jnp.concatenate(means, axis=0), LANES)

    slope2 = _alibi_slope(h, MOBA_HEADS) * LOG2E
    key_off = lax.broadcasted_iota(jnp.int32, (1, blk), 1).astype(F32)
    ti = lax.broadcasted_iota(jnp.int32, (blk, blk), 0)
    sj = lax.broadcasted_iota(jnp.int32, (blk, blk), 1)

    def tile(k):
        gated = k > MOBA_TOPK
        rows = slice(k * blk, (k + 1) * blk)
        qn = _head_norm(q_ref[rows, :], qg_ref[...])
        qs = (qn * (hd ** -0.5 * LOG2E)).astype(BF16)
        if gated:
            gate = _dot_nt_3pass(qn, km_scr[...])
            lane = lax.broadcasted_iota(jnp.int32, gate.shape, 1)
            gm = jnp.where(lane < k, gate, NEG_INF)
            rank = jnp.zeros(gate.shape, jnp.int32)
            for sh in range(1, k):
                other = pltpu.roll(gm, sh, axis=1)
                ahead = (other > gm) | ((other == gm) & (lane >= sh))
                rank = rank + ahead.astype(jnp.int32)
            drop = jnp.where((lane < k) & (rank >= MOBA_TOPK), MASK_NEG, 0.0).astype(BF16)
            q_aug = jnp.concatenate([qs, drop], axis=1)
        scores = []
        for n in range(k):
            keys = ka_scr[n * blk:(n + 1) * blk, :] if gated else ka_scr[n * blk:(n + 1) * blk, :hd]
            bias = slope2 * (key_off + float((n - k) * blk))
            scores.append(_dot_nt(q_aug if gated else qs, keys) + bias)
        s_own = _dot_nt(qs, ka_scr[k * blk:(k + 1) * blk, :hd]) + slope2 * key_off
        scores.append(jnp.where(ti >= sj, s_own, NEG_INF))

        m_el = scores[0]
        for s in scores[1:]:
            m_el = jnp.maximum(m_el, s)
        m = jnp.broadcast_to(jnp.max(m_el, axis=-1, keepdims=True), (blk, blk))
        l_el = jnp.zeros((blk, blk), F32)
        acc = jnp.zeros((blk, hd), F32)
        for n, s in enumerate(scores):
            p = jnp.exp2(s - m)
            l_el = l_el + p
            acc = acc + _dot(p.astype(BF16), vb_scr[n * blk:(n + 1) * blk, :])
        out = acc / jnp.sum(l_el, axis=-1, keepdims=True)
        o_ref[rows, :] = (out * _silu(g_ref[rows, :].astype(F32))).astype(o_ref.dtype)

    for k in range(n_blocks):
        tile(k)


def _moba_prompt(pqk, pv, pg, q_gain, k_gain, batch, t_rows, out_dtype):
    m, n = pqk.shape
    h = MOBA_HEADS
    hd = MOBA_HD
    assert n == 2 * h * hd and t_rows % MOBA_BLOCK == 0
    nb = t_rows // MOBA_BLOCK
    blk = MOBA_BLOCK
    return pl.pallas_call(
        functools.partial(_moba_prompt_kernel, n_blocks=nb),
        out_shape=(jax.ShapeDtypeStruct((m, h * hd), out_dtype),
                   jax.ShapeDtypeStruct((m, h * hd), F32)),
        grid=(batch, h),
        in_specs=[
            pl.BlockSpec((t_rows, hd), lambda b, hh: (b, hh)),
            pl.BlockSpec((t_rows, hd), lambda b, hh: (b, h + hh)),
            pl.BlockSpec((t_rows, hd), lambda b, hh: (b, hh)),
            pl.BlockSpec((t_rows, hd), lambda b, hh: (b, hh)),
            pl.BlockSpec((1, hd), lambda b, hh: (0, 0)),
            pl.BlockSpec((1, hd), lambda b, hh: (0, 0)),
        ],
        out_specs=(pl.BlockSpec((t_rows, hd), lambda b, hh: (b, hh)),
                   pl.BlockSpec((t_rows, hd), lambda b, hh: (b, hh))),
        scratch_shapes=[
            pltpu.VMEM((t_rows, 2 * hd), BF16),
            pltpu.VMEM((t_rows, hd), BF16),
            pltpu.VMEM((LANES, hd), F32),
        ],
        compiler_params=_cparams(("parallel", "parallel")),
        name="moba_prompt",
    )(pqk, pqk, pv, pg, q_gain.reshape(1, hd), k_gain.reshape(1, hd))


KMEAN_PAGES_PER_STEP = 8


def _kmean_kernel(pt_ref, *refs):
    del pt_ref
    page_refs, o_ref = refs[:-1], refs[-1]
    n = pl.program_id(1)
    blocks_per_step = len(page_refs) // 2
    for j in range(blocks_per_step):
        tot = jnp.sum(page_refs[2 * j][...], axis=0) + jnp.sum(page_refs[2 * j + 1][...], axis=0)
        o_ref[n * blocks_per_step + j] = tot * (1.0 / MOBA_BLOCK)


def _moba_kmean(cache, page_table):
    _, page, heads, hd = cache.shape
    db, n_pages = page_table.shape
    pps = math.gcd(n_pages, KMEAN_PAGES_PER_STEP)
    assert MOBA_BLOCK == 2 * page and pps % 2 == 0
    nb = n_pages // 2

    def page_spec(j):
        return pl.BlockSpec((None, page, heads, hd), lambda b, n, pt: (pt[b, pps * n + j], 0, 0, 0))

    return pl.pallas_call(
        _kmean_kernel,
        out_shape=jax.ShapeDtypeStruct((db, nb, heads, hd), F32),
        grid_spec=pltpu.PrefetchScalarGridSpec(
            num_scalar_prefetch=1,
            grid=(db, n_pages // pps),
            in_specs=[page_spec(j) for j in range(pps)],
            out_specs=pl.BlockSpec((None, nb, heads, hd), lambda b, n, pt: (b, 0, 0, 0)),
        ),
        compiler_params=_cparams(("parallel", "arbitrary")),
        name="moba_past_block_means",
    )(page_table, *([cache] * pps))


def _moba_select_kernel(q_ref, k_ref, km_ref, qg_ref, kg_ref, qn_ref, kn_ref, idx_ref, *, n_blocks):
    hd = MOBA_HD
    t = q_ref.shape[0]
    lane = lax.broadcasted_iota(jnp.int32, (t, LANES), 1)
    for h in range(MOBA_HEADS):
        cols = slice(h * hd, (h + 1) * hd)
        qn = _head_norm(q_ref[:, cols], qg_ref[...])
        kn = _head_norm(k_ref[:, cols], kg_ref[...])
        qn_ref[:, cols] = qn
        kn_ref[:, cols] = kn
        km = _pad_rows(km_ref[:, h, :], LANES)
        gate = _dot_nt_3pass(qn, km)
        gm = jnp.where(lane < n_blocks, gate, NEG_INF)
        picks = jnp.zeros((t, LANES), jnp.int32)
        for j in range(MOBA_TOPK):
            best = jnp.max(gm, axis=-1, keepdims=True)
            ix = jnp.min(jnp.where(gm == best, lane, LANES), axis=-1, keepdims=True)
            picks = jnp.where(lane == j, ix, picks)
            gm = jnp.where(lane == ix, NEG_INF, gm)
        idx_ref[h] = picks


def _moba_select(p, kmean, q_gain, k_gain, batch, t_rows):
    m, n = p.shape
    w = MOBA_HEADS * MOBA_HD
    nb = kmean.shape[1]
    assert nb >= MOBA_TOPK and nb <= LANES
    return pl.pallas_call(
        functools.partial(_moba_select_kernel, n_blocks=nb),
        out_shape=(jax.ShapeDtypeStruct((m, w), F32),
                   jax.ShapeDtypeStruct((m, w), F32),
                   jax.ShapeDtypeStruct((batch, MOBA_HEADS, t_rows, LANES), jnp.int32)),
        grid=(batch,),
        in_specs=[
            pl.BlockSpec((t_rows, w), lambda b: (b, 0)),
            pl.BlockSpec((t_rows, w), lambda b: (b, 1)),
            pl.BlockSpec((None, nb, MOBA_HEADS, MOBA_HD), lambda b: (b, 0, 0, 0)),
            pl.BlockSpec((1, MOBA_HD), lambda b: (0, 0)),
            pl.BlockSpec((1, MOBA_HD), lambda b: (0, 0)),
        ],
        out_specs=(pl.BlockSpec((t_rows, w), lambda b: (b, 0)),
                   pl.BlockSpec((t_rows, w), lambda b: (b, 0)),
                   pl.BlockSpec((None, MOBA_HEADS, t_rows, LANES), lambda b: (b, 0, 0, 0))),
        compiler_params=_cparams(("parallel",)),
        name="moba_sample_select",
    )(p, p, kmean, q_gain.reshape(1, MOBA_HD), k_gain.reshape(1, MOBA_HD))


def _moba_sample_attn_kernel(idx_ref, pt_ref, ck_ref, cv_ref, qn_ref, kn_ref, vn_ref, g_ref, o_ref,
                             kbuf, vbuf, sem, *, past_len, t_rows):
    step = pl.program_id(0)
    n_steps = pl.num_programs(0)
    h = step % MOBA_HEADS
    hd = MOBA_HD
    blk = MOBA_BLOCK
    page = blk // 2
    n_sel = t_rows * MOBA_TOPK
    scale = hd ** -0.5
    slope = _alibi_slope(h, MOBA_HEADS)

    def for_each_copy(s, slot, fn):
        sb = s // MOBA_HEADS
        sh = s % MOBA_HEADS

        def body(u, carry):
            blk_id = idx_ref[s * n_sel + u]
            for half in range(2):
                pg = pt_ref[sb, 2 * blk_id + half]
                rows = pl.ds(half * page, page)
                fn(pltpu.make_async_copy(ck_ref.at[pg, :, sh, :], kbuf.at[slot, u, rows, :], sem.at[slot]), 0)
                fn(pltpu.make_async_copy(cv_ref.at[pg, :, sh, :], vbuf.at[slot, u, rows, :], sem.at[slot]), 1)
            return carry

        lax.fori_loop(0, n_sel, body, 0)

    slot = step % 2

    @pl.when(step == 0)
    def _():
        for_each_copy(step, slot, lambda cp, prio: cp.start(priority=prio))

    @pl.when(step + 1 < n_steps)
    def _():
        for_each_copy(step + 1, 1 - slot, lambda cp, prio: cp.start(priority=prio))

    for_each_copy(step, slot, lambda cp, prio: cp.wait())

    offs = lax.broadcasted_iota(jnp.int32, (1, blk), 1)
    key_i = lax.broadcasted_iota(jnp.int32, (t_rows, 1), 0)
    kn = kn_ref[...]
    vn = vn_ref[...]

    for qi in range(t_rows):
        qrow = qn_ref[qi:qi + 1, :]
        q8 = jnp.broadcast_to(qrow, (8, hd)).astype(BF16)
        t_pos = past_len + qi
        s_own = jnp.sum(kn * qrow, axis=-1, keepdims=True) * scale - slope * (qi - key_i).astype(F32)
        s_own = jnp.where(key_i <= qi, s_own, NEG_INF)
        m = jnp.max(s_own, axis=0, keepdims=True)
        scores = []
        for j in range(MOBA_TOPK):
            u = qi * MOBA_TOPK + j
            dist = (t_pos - idx_ref[step * n_sel + u] * blk - offs).astype(F32)
            s = _dot_nt(q8, kbuf[slot, u].astype(BF16))[0:1, :] * scale - slope * dist
            scores.append(s)
            m = jnp.maximum(m, jnp.max(s, axis=-1, keepdims=True))
        p_own = jnp.exp(s_own - m)
        denom = jnp.sum(p_own, axis=0, keepdims=True)
        out = jnp.sum(p_own * vn, axis=0, keepdims=True)
        for j, s in enumerate(scores):
            u = qi * MOBA_TOPK + j
            pj = jnp.exp(s - m)
            denom = denom + jnp.sum(pj, axis=-1, keepdims=True)
            out = out + _dot(jnp.broadcast_to(pj, (8, blk)).astype(BF16), vbuf[slot, u].astype(BF16))[0:1, :]
        gate = g_ref[qi:qi + 1, :]
        o_ref[qi:qi + 1, :] = out / denom * _silu(gate)


def _moba_sample_attn(idx, page_table, cache_k, cache_v, qn, kn, vn, gate, batch, t_rows):
    h = MOBA_HEADS
    hd = MOBA_HD
    _, page, heads, _ = cache_k.shape
    n_pages = page_table.shape[1]
    assert MOBA_BLOCK == 2 * page and heads == h
    n_sel = t_rows * MOBA_TOPK
    row_spec = pl.BlockSpec((t_rows, hd), lambda s, ix, pt: (s // h, s % h))
    return pl.pallas_call(
        functools.partial(_moba_sample_attn_kernel, past_len=n_pages * page, t_rows=t_rows),
        out_shape=jax.ShapeDtypeStruct((batch * t_rows, h * hd), F32),
        grid_spec=pltpu.PrefetchScalarGridSpec(
            num_scalar_prefetch=2,
            grid=(batch * h,),
            in_specs=[pl.BlockSpec(memory_space=pl.ANY), pl.BlockSpec(memory_space=pl.ANY),
                      row_spec, row_spec, row_spec, row_spec],
            out_specs=row_spec,
            scratch_shapes=[pltpu.VMEM((2, n_sel, MOBA_BLOCK, hd), F32),
                            pltpu.VMEM((2, n_sel, MOBA_BLOCK, hd), F32),
                            pltpu.SemaphoreType.DMA((2,))],
        ),
        compiler_params=_cparams(("arbitrary",)),
        name="moba_sample_attention",
    )(idx, page_table, cache_k, cache_v, qn, kn, vn, gate)


def _retention_layer(hp, hs, gain, w_in, w_out, slot, state, prev_p, prev_s, bp, tp, bs, ts):
    (pp,), (ps,) = _in_proj(hp, hs, gain, w_in, slot, [(w_in.shape[2], BF16)])
    op, st_p = _retention(pp, None, 0, bp, tp, BF16, prev_p)
    o_s, st_s = _retention(ps, state, slot, bs, ts, F32, prev_s)
    hp, hs = _out_proj(op, o_s, w_out, slot, hp, hs)
    return hp, hs, st_p, st_s


def _hgrn_layer(hp, hs, gain, w_in, lb_logits, o_gain, w_out, slot, state, layer, bp, tp, bs, ts):
    w = w_in.shape[2] // 4
    (pq, pf, pig), (sq, sf, sig) = _in_proj(hp, hs, gain, w_in, slot, [(w, BF16), (w, F32), (2 * w, BF16)])
    op, st_p = _hgrn(pq, pf, pig, lb_logits, o_gain, None, 0, layer, bp, tp, BF16)
    o_s, st_s = _hgrn(sq, sf, sig, lb_logits, o_gain, state, slot, layer, bs, ts, F32)
    hp, hs = _out_proj(op, o_s, w_out, slot, hp, hs)
    return hp, hs, st_p, st_s


def _moba_layer(hp, hs, gain, w_in, q_gain, k_gain, w_out, slot, cache_k, cache_v, page_table, bp, tp, bs, ts):
    w = MOBA_HEADS * MOBA_HD
    (pqk, pv, pg), (sqk, sv, sg) = _in_proj(hp, hs, gain, w_in, slot, [(2 * w, F32), (w, F32), (w, BF16)])
    kmean = _moba_kmean(cache_k, page_table)
    qn, kn_s, picks = _moba_select(sqk, kmean, q_gain, k_gain, bs, ts)
    idx = picks[..., :MOBA_TOPK].reshape(-1)
    o_s = _moba_sample_attn(idx, page_table, cache_k, cache_v, qn, kn_s, sv, sg, bs, ts)
    op, kn_p = _moba_prompt(pqk, pv, pg, q_gain, k_gain, bp, tp, BF16)
    hp, hs = _out_proj(op, o_s, w_out, slot, hp, hs)
    return hp, hs, kn_p, pv, kn_s, sv


def kernel(x_prompt, x_sample, state_ret, cache_k, cache_v, state_hgrn, page_table, norm_gain, ret_w_in, ret_w_out, moba_w_in, moba_q_gain, moba_k_gain, moba_w_out, hgrn_w_in, hgrn_lb_logits, hgrn_o_gain, hgrn_w_out):
    bp, tp, d = x_prompt.shape
    bs, ts, _ = x_sample.shape
    depth = norm_gain.shape[0]
    n_pool = cache_k.shape[1]
    hp = x_prompt.reshape(bp * tp, d)
    hs = x_sample.reshape(bs * ts, d)
    ck = cache_k.reshape((-1,) + cache_k.shape[2:])
    cv = cache_v.reshape((-1,) + cache_v.shape[2:])
    kp_l, vp_l, ks_l, vs_l, gp_l, gs_l = [], [], [], [], [], []
    ret_p = ret_s = None
    for layer in range(depth):
        kind, slot = layer % N_MIXERS, layer // N_MIXERS
        gain = norm_gain[layer]
        if kind == 0:
            hp, hs, ret_p, ret_s = _retention_layer(hp, hs, gain, ret_w_in, ret_w_out, slot, state_ret,
                                                    ret_p, ret_s, bp, tp, bs, ts)
        elif kind == 1:
            hp, hs, kp, vp, ks_, vs_ = _moba_layer(hp, hs, gain, moba_w_in, moba_q_gain[slot], moba_k_gain[slot],
                                                   moba_w_out, slot, ck, cv, page_table + slot * n_pool,
                                                   bp, tp, bs, ts)
            kp, vp, hp = lax.optimization_barrier((kp, vp, hp))
            shp = (MOBA_HEADS, MOBA_HD)
            kp_l.append(kp.reshape(bp, tp, *shp))
            vp_l.append(vp.reshape(bp, tp, *shp))
            ks_l.append(ks_.reshape(bs, ts, *shp))
            vs_l.append(vs_.reshape(bs, ts, *shp))
        else:
            hp, hs, st_p, st_s = _hgrn_layer(hp, hs, gain, hgrn_w_in, hgrn_lb_logits, hgrn_o_gain[slot],
                                             hgrn_w_out, slot, state_hgrn, layer, bp, tp, bs, ts)
            gp_l.append(st_p)
            gs_l.append(st_s)
    return (hp.reshape(bp, tp, d), hs.reshape(bs, ts, d),
            jnp.stack(kp_l), jnp.stack(vp_l), jnp.stack(ks_l), jnp.stack(vs_l),
            ret_p, ret_s, jnp.stack(gp_l), jnp.stack(gs_l))
```
